```python
import jax, jax.numpy as jnp
from jax import lax
import numpy as np

D_MODEL = 2048
BATCH = 1
SEQ = 8192
DEPTH = 1
DEC_BATCH = 2
DEC_SEQ = 4096
PAST_LEN = 128

N_Q_HEADS = 16
N_KV_HEADS = 4
HEAD_DIM = 128
GQA_GROUP = N_Q_HEADS // N_KV_HEADS
ATTN_W = N_Q_HEADS * HEAD_DIM
KV_W = N_KV_HEADS * HEAD_DIM
Q_BLOCK = 128
GRID_W = 64
ROPE_THETA = 10000.0
ROPE_AXIS_DIM = HEAD_DIM // 2
N_FOURIER_GROUPS = 4
FOURIER_GROUP_W = 256
FOURIER_W = N_FOURIER_GROUPS * FOURIER_GROUP_W
N_BRANCHES = 2
GATE_W = N_BRANCHES * D_MODEL
IN_W = ATTN_W + 2 * KV_W + FOURIER_W + GATE_W
N_EXPERTS = 16
EXPERT_FF = 1024
CAPACITY_FACTOR = 2
EPS = 1e-6

kernel_name = "hybrid_gqa_fnet_ec_encoder"


def rmsnorm(x, g):
    xf = x.astype(jnp.float32)
    y = xf * lax.rsqrt(jnp.mean(xf * xf, axis=-1, keepdims=True) + EPS) * g.astype(jnp.float32)
    return y.astype(x.dtype)


def axial_rope_tables(seq_len):
    rows = seq_len // GRID_W
    row = jnp.broadcast_to(jnp.arange(rows, dtype=jnp.float32)[:, None], (rows, GRID_W)).reshape(seq_len)
    col = jnp.broadcast_to(jnp.arange(GRID_W, dtype=jnp.float32)[None, :], (rows, GRID_W)).reshape(seq_len)
    inv_freq = ROPE_THETA ** (-jnp.arange(0, ROPE_AXIS_DIM, 2, dtype=jnp.float32) / ROPE_AXIS_DIM)
    ang_r = row[:, None] * inv_freq
    ang_c = col[:, None] * inv_freq
    return jnp.cos(ang_r), jnp.sin(ang_r), jnp.cos(ang_c), jnp.sin(ang_c)


def rotate(x, cos, sin):
    half = x.shape[-1] // 2
    x1, x2 = x[..., :half], x[..., half:]
    c = cos[None, :, None, :]
    s = sin[None, :, None, :]
    return jnp.concatenate([x1 * c - x2 * s, x2 * c + x1 * s], axis=-1)


def apply_axial_rope(x, cos_r, sin_r, cos_c, sin_c):
    xf = x.astype(jnp.float32)
    out = jnp.concatenate([rotate(xf[..., :ROPE_AXIS_DIM], cos_r, sin_r),
                           rotate(xf[..., ROPE_AXIS_DIM:], cos_c, sin_c)], axis=-1)
    return out.astype(x.dtype)


def attention_branch(q, k, v, q_norm_g, k_norm_g):
    B, S, _ = q.shape
    q = rmsnorm(q.reshape(B, S, N_Q_HEADS, HEAD_DIM), q_norm_g)
    k = rmsnorm(k.reshape(B, S, N_KV_HEADS, HEAD_DIM), k_norm_g)
    v = v.reshape(B, S, N_KV_HEADS, HEAD_DIM)
    tabs = axial_rope_tables(S)
    q = apply_axial_rope(q, *tabs)
    k = apply_axial_rope(k, *tabs)
    n_blk = S // Q_BLOCK
    qb = q.reshape(B, n_blk, Q_BLOCK, N_KV_HEADS, GQA_GROUP, HEAD_DIM).transpose(1, 0, 2, 3, 4, 5)
    scale = HEAD_DIM ** -0.5

    def block(q_blk):
        s = jnp.einsum('bqhgd,bkhd->bhgqk', q_blk, k, preferred_element_type=jnp.float32) * scale
        p = jax.nn.softmax(s, axis=-1).astype(v.dtype)
        return jnp.einsum('bhgqk,bkhd->bqhgd', p, v)

    ob = lax.map(block, qb)
    return ob.transpose(1, 0, 2, 3, 4, 5).reshape(B, S, ATTN_W)


def fourier_branch(u):
    B, S, _ = u.shape
    ug = u.reshape(B, S, N_FOURIER_GROUPS, FOURIER_GROUP_W).astype(jnp.float32)
    f = jnp.fft.fft2(ug, axes=(1, 3), norm="ortho").real
    return f.reshape(B, S, FOURIER_W).astype(u.dtype)


def mixer_sublayer(x, norm_g, w_in, q_norm_g, k_norm_g, w_attn_proj, w_fourier_proj, w_out):
    h = rmsnorm(x, norm_g)
    proj = h @ w_in
    o1 = ATTN_W
    o2 = o1 + KV_W
    o3 = o2 + KV_W
    o4 = o3 + FOURIER_W
    q, k, v, u, gates = jnp.split(proj, [o1, o2, o3, o4], axis=-1)
    gates = jax.nn.sigmoid(gates.astype(jnp.float32)).astype(x.dtype)
    g_attn, g_four = jnp.split(gates, N_BRANCHES, axis=-1)
    a = attention_branch(q, k, v, q_norm_g, k_norm_g) @ w_attn_proj
    f = fourier_branch(u) @ w_fourier_proj
    return x + (g_attn * a + g_four * f) @ w_out


def expert_choice_moe(h, w_router, w_gate, w_up, w_down):
    B, S, D = h.shape
    T = B * S
    cap = CAPACITY_FACTOR * T // N_EXPERTS
    ht = h.reshape(T, D)
    aff = jax.nn.softmax((ht @ w_router).astype(jnp.float32), axis=-1)
    gate_vals, tok_idx = lax.top_k(aff.T, cap)
    xg = ht[tok_idx]
    hid = jax.nn.silu(jnp.einsum('ecd,edf->ecf', xg, w_gate)) * jnp.einsum('ecd,edf->ecf', xg, w_up)
    out = jnp.einsum('ecf,efd->ecd', hid, w_down) * gate_vals[..., None].astype(h.dtype)
    y = jnp.zeros_like(ht).at[tok_idx.reshape(-1)].add(out.reshape(-1, D))
    return y.reshape(B, S, D)


def encoder_forward(x, norm_mix_g, w_in, q_norm_g, k_norm_g, w_attn_proj, w_fourier_proj, w_out,
                    norm_moe_g, w_router, w_expert_gate, w_expert_up, w_expert_down, norm_final_g):
    for l in range(DEPTH):
        x = mixer_sublayer(x, norm_mix_g[l], w_in[l], q_norm_g[l], k_norm_g[l],
                           w_attn_proj[l], w_fourier_proj[l], w_out[l])
        x = x + expert_choice_moe(rmsnorm(x, norm_moe_g[l]), w_router[l],
                                  w_expert_gate[l], w_expert_up[l], w_expert_down[l])
    return rmsnorm(x, norm_final_g)


def setup_inputs(seed: int = 0) -> dict:
    key = jax.random.key(seed)
    ks = jax.random.split(key, 16)
    f32 = jnp.float32

    def nrm(k, shape, scale):
        return jax.random.normal(k, shape, f32) * scale

    def gain(k, shape):
        return 1.0 + 0.02 * jax.random.normal(k, shape, f32)

    return {
        "x_prompt": jax.random.normal(ks[0], (BATCH, SEQ, D_MODEL), f32),
        "x_sample": jax.random.normal(ks[1], (DEC_BATCH, DEC_SEQ, D_MODEL), f32),
        "norm_mix_g": gain(ks[2], (DEPTH, D_MODEL)),
        "w_in": nrm(ks[3], (DEPTH, D_MODEL, IN_W), D_MODEL ** -0.5),
        "q_norm_g": gain(ks[4], (DEPTH, HEAD_DIM)),
        "k_norm_g": gain(ks[5], (DEPTH, HEAD_DIM)),
        "w_attn_proj": nrm(ks[6], (DEPTH, ATTN_W, D_MODEL), ATTN_W ** -0.5),
        "w_fourier_proj": nrm(ks[7], (DEPTH, FOURIER_W, D_MODEL), FOURIER_W ** -0.5),
        "w_out": nrm(ks[8], (DEPTH, D_MODEL, D_MODEL), D_MODEL ** -0.5),
        "norm_moe_g": gain(ks[9], (DEPTH, D_MODEL)),
        "w_router": nrm(ks[10], (DEPTH, D_MODEL, N_EXPERTS), D_MODEL ** -0.5),
        "w_expert_gate": nrm(ks[11], (DEPTH, N_EXPERTS, D_MODEL, EXPERT_FF), D_MODEL ** -0.5),
        "w_expert_up": nrm(ks[12], (DEPTH, N_EXPERTS, D_MODEL, EXPERT_FF), D_MODEL ** -0.5),
        "w_expert_down": nrm(ks[13], (DEPTH, N_EXPERTS, EXPERT_FF, D_MODEL), EXPERT_FF ** -0.5),
        "norm_final_g": gain(ks[14], (D_MODEL,)),
    }


def reference(x_prompt, x_sample, norm_mix_g, w_in, q_norm_g, k_norm_g, w_attn_proj, w_fourier_proj,
              w_out, norm_moe_g, w_router, w_expert_gate, w_expert_up, w_expert_down, norm_final_g):
    y_prompt = encoder_forward(x_prompt, norm_mix_g, w_in, q_norm_g, k_norm_g, w_attn_proj,
                               w_fourier_proj, w_out, norm_moe_g, w_router, w_expert_gate,
                               w_expert_up, w_expert_down, norm_final_g)
    y_sample = encoder_forward(x_sample, norm_mix_g, w_in, q_norm_g, k_norm_g, w_attn_proj,
                               w_fourier_proj, w_out, norm_moe_g, w_router, w_expert_gate,
                               w_expert_up, w_expert_down, norm_final_g)
    return (y_prompt, y_sample)
```

```python
import functools
import math

import jax
import jax.numpy as jnp
from jax import lax
from jax.experimental import pallas as pl
from jax.experimental.pallas import tpu as pltpu

F32 = jnp.float32
BF16 = jnp.bfloat16
I32 = jnp.int32

EPS = 1e-6
N_Q_HEADS = 16
N_KV_HEADS = 4
ROPE_GRID_W = 64
ROPE_THETA = 10000.0
N_FOURIER_GROUPS = 4
FOURIER_GROUP_W = 256
CAPACITY_FACTOR = 2
LANES = 128
VMEM_LIMIT = 56 * 1024 * 1024


def _cparams(sem):
    return pltpu.CompilerParams(dimension_semantics=sem, vmem_limit_bytes=VMEM_LIMIT)


def _dot(a, b):
    return jnp.dot(a, b, preferred_element_type=F32)


def _dot_nt(a, b):
    return lax.dot_general(a, b, (((1,), (1,)), ((), ())), preferred_element_type=F32)


def _rms(x, g):
    ms = jnp.mean(x * x, axis=-1, keepdims=True)
    return x * lax.rsqrt(ms + EPS) * g


def _inproj_kernel(x_ref, g_ref, w_ref, q_ref, k_ref, v_ref, u_ref, gt_ref, h_ref, *, edges):
    j = pl.program_id(1)

    @pl.when(j == 0)
    def _():
        h_ref[...] = _rms(x_ref[...], g_ref[...]).astype(BF16)

    acc = _dot(h_ref[...], w_ref[...])
    e_q, e_k, e_v, e_u = edges

    @pl.when(j < e_q)
    def _():
        q_ref[...] = acc

    @pl.when((j >= e_q) & (j < e_k))
    def _():
        k_ref[...] = acc

    @pl.when((j >= e_k) & (j < e_v))
    def _():
        v_ref[...] = acc.astype(v_ref.dtype)

    @pl.when((j >= e_v) & (j < e_u))
    def _():
        u_ref[...] = acc.astype(u_ref.dtype)

    @pl.when(j >= e_u)
    def _():
        gt_ref[...] = acc.astype(gt_ref.dtype)


def _inproj(x, g, w, widths, tm=1024, tn=512):
    T, D = x.shape
    tm = min(tm, T)
    wq, wk, wv, wu, wg = widths
    nq, nk, nv, nu, ng = (w_ // tn for w_ in widths)
    assert all(w_ % tn == 0 for w_ in widths)
    e_q, e_k, e_v, e_u = nq, nq + nk, nq + nk + nv, nq + nk + nv + nu
    nj = e_u + ng

    def cmap(lo, n):
        return lambda i, j: (i, jnp.clip(j - lo, 0, n - 1))

    return pl.pallas_call(
        functools.partial(_inproj_kernel, edges=(e_q, e_k, e_v, e_u)),
        grid=(T // tm, nj),
        in_specs=[
            pl.BlockSpec((tm, D), lambda i, j: (i, 0)),
            pl.BlockSpec((1, D), lambda i, j: (0, 0)),
            pl.BlockSpec((D, tn), lambda i, j: (0, j)),
        ],
        out_specs=[
            pl.BlockSpec((tm, tn), cmap(0, nq)),
            pl.BlockSpec((tm, tn), cmap(e_q, nk)),
            pl.BlockSpec((tm, tn), cmap(e_k, nv)),
            pl.BlockSpec((tm, tn), cmap(e_v, nu)),
            pl.BlockSpec((tm, tn), cmap(e_u, ng)),
        ],
        out_shape=[
            jax.ShapeDtypeStruct((T, wq), F32),
            jax.ShapeDtypeStruct((T, wk), F32),
            jax.ShapeDtypeStruct((T, wv), BF16),
            jax.ShapeDtypeStruct((T, wu), BF16),
            jax.ShapeDtypeStruct((T, wg), BF16),
        ],
        scratch_shapes=[pltpu.VMEM((tm, D), BF16)],
        compiler_params=_cparams(("parallel", "arbitrary")),
        name="inproj",
    )(x, g, w)


def _qkrope_kernel(q_ref, k_ref, qg_ref, kg_ref, c_ref, s_ref, qo_ref, ko_ref, *, hd, scale):
    c = c_ref[...]
    s = s_ref[...]
    lane = lax.broadcasted_iota(I32, (1, hd), 1)
    low_half = (lane % (hd // 2)) < (hd // 4)

    def one(x, g):
        y = _rms(x, g)
        sw = jnp.where(low_half, pltpu.roll(y, hd - hd // 4, 1), pltpu.roll(y, hd // 4, 1))
        return y * c + sw * s

    for h in range(q_ref.shape[1] // hd):
        sl = slice(h * hd, (h + 1) * hd)
        qo_ref[:, sl] = (one(q_ref[:, sl], qg_ref[...]) * scale).astype(qo_ref.dtype)
    for h in range(k_ref.shape[1] // hd):
        sl = slice(h * hd, (h + 1) * hd)
        ko_ref[:, sl] = one(k_ref[:, sl], kg_ref[...]).astype(ko_ref.dtype)


def _qkrope(q, k, qg, kg, cos_t, sin_t, S, ts=512):
    T, wq = q.shape
    wk = k.shape[1]
    hd = qg.shape[-1]
    ts = min(ts, S)
    ns = S // ts
    return pl.pallas_call(
        functools.partial(_qkrope_kernel, hd=hd, scale=hd ** -0.5),
        grid=(T // ts,),
        in_specs=[
            pl.BlockSpec((ts, wq), lambda i: (i, 0)),
            pl.BlockSpec((ts, wk), lambda i: (i, 0)),
            pl.BlockSpec((1, hd), lambda i: (0, 0)),
            pl.BlockSpec((1, hd), lambda i: (0, 0)),
            pl.BlockSpec((ts, hd), lambda i: (i % ns, 0)),
            pl.BlockSpec((ts, hd), lambda i: (i % ns, 0)),
        ],
        out_specs=[
            pl.BlockSpec((ts, wq), lambda i: (i, 0)),
            pl.BlockSpec((ts, wk), lambda i: (i, 0)),
        ],
        out_shape=[
            jax.ShapeDtypeStruct((T, wq), BF16),
            jax.ShapeDtypeStruct((T, wk), BF16),
        ],
        compiler_params=_cparams(("parallel",)),
        name="qkrope",
    )(q, k, qg, kg, cos_t, sin_t)


def _rope_tables(S, hd):
    axis = hd // 2
    pos = jnp.arange(S, dtype=F32)
    row = jnp.floor(pos / ROPE_GRID_W)
    col = pos - row * ROPE_GRID_W
    inv_freq = ROPE_THETA ** (-jnp.arange(0, axis, 2, dtype=F32) / axis)
    ang_r = row[:, None] * inv_freq
    ang_c = col[:, None] * inv_freq
    cr, sr, cc, sc = jnp.cos(ang_r), jnp.sin(ang_r), jnp.cos(ang_c), jnp.sin(ang_c)
    cos_t = jnp.concatenate([cr, cr, cc, cc], axis=-1)
    sin_t = jnp.concatenate([-sr, sr, -sc, sc], axis=-1)
    return cos_t, sin_t


def _attn_kernel(q_ref, k_ref, v_ref, o_ref, *, tk, hd):
    tq = q_ref.shape[0]
    n_chunks = k_ref.shape[0] // tk
    for hl in range(q_ref.shape[1] // hd):
        sl = slice(hl * hd, (hl + 1) * hd)
        q = q_ref[:, sl]

        def body(c, carry, q=q):
            m, l, acc = carry
            off = pl.multiple_of(c * tk, tk)
            kc = k_ref[pl.ds(off, tk), :]
            vc = v_ref[pl.ds(off, tk), :]
            s = _dot_nt(q, kc)
            m_new = jnp.maximum(m, jnp.max(s, axis=1, keepdims=True))
            p = jnp.exp(s - m_new)
            alpha = jnp.exp(m - m_new)
            l = alpha * l + jnp.sum(p, axis=1, keepdims=True)
            acc = alpha * acc + _dot(p.astype(BF16), vc)
            return m_new, l, acc

        init = (jnp.full((tq, 1), -jnp.inf, F32), jnp.zeros((tq, 1), F32), jnp.zeros((tq, hd), F32))
        _, l, acc = lax.fori_loop(0, n_chunks, body, init)
        o_ref[:, sl] = (acc / l).astype(o_ref.dtype)


def _attention(q, k, v, B, S, hd, tq=256, tk=512):
    T, wq = q.shape
    n_kv = k.shape[1] // hd
    gw = wq // n_kv
    tq = min(tq, S)
    tk = min(tk, S)
    nq = S // tq
    return pl.pallas_call(
        functools.partial(_attn_kernel, tk=tk, hd=hd),
        grid=(B, n_kv, nq),
        in_specs=[
            pl.BlockSpec((tq, gw), lambda b, g, i: (b * nq + i, g)),
            pl.BlockSpec((S, hd), lambda b, g, i: (b, g)),
            pl.BlockSpec((S, hd), lambda b, g, i: (b, g)),
        ],
        out_specs=pl.BlockSpec((tq, gw), lambda b, g, i: (b * nq + i, g)),
        out_shape=jax.ShapeDtypeStruct((T, wq), BF16),
        compiler_params=_cparams(("parallel", "parallel", "parallel")),
        name="attention",
    )(q, k, v)


def _dft_split(S):
    lg = int(math.log2(S))
    assert 1 << lg == S
    n2 = 1 << ((lg + 1) // 2)
    return S // n2, n2


def _cos_sin(n, m, period):
    ang = (2.0 * math.pi / period) * ((jnp.arange(n, dtype=I32)[:, None] * jnp.arange(m, dtype=I32)[None, :]) % period).astype(F32)
    return jnp.cos(ang), jnp.sin(ang)


def _f0_kernel(u_ref, cs_ref, a_ref, b_ref, *, gw):
    cs = cs_ref[...]
    for g in range(u_ref.shape[1] // gw):
        sl = slice(g * gw, (g + 1) * gw)
        ab = _dot(u_ref[:, sl], cs)
        a_ref[:, sl] = ab[:, :gw]
        b_ref[:, sl] = ab[:, gw:]


def _fa_kernel(a_ref, b_ref, ca_ref, sa_ref, ct_ref, st_ref, tr_ref, ti_ref):
    ca = ca_ref[...]
    sa = sa_ref[...]
    reps = a_ref.shape[2] // LANES
    for j in range(a_ref.shape[1]):
        a = a_ref[:, j, :].astype(BF16)
        b = b_ref[:, j, :].astype(BF16)
        tr = _dot(ca, a) - _dot(sa, b)
        ti = -(_dot(sa, a) + _dot(ca, b))
        ct = jnp.tile(ct_ref[j], (1, reps))
        st = jnp.tile(st_ref[j], (1, reps))
        tr_ref[:, j, :] = tr * ct + ti * st
        ti_ref[:, j, :] = ti * ct - tr * st


def _fb_kernel(tr_ref, ti_ref, c1_ref, s1_ref, f_ref, *, scale):
    c1 = c1_ref[...]
    s1 = s1_ref[...]
    for kk in range(tr_ref.shape[0]):
        xr = tr_ref[kk].astype(BF16)
        xi = ti_ref[kk].astype(BF16)
        f_ref[:, kk, :] = (_dot(c1, xr) + _dot(s1, xi)) * scale


def _fourier(u, B, S, ts=512, r=8):
    T, W = u.shape
    gw = FOURIER_GROUP_W
    ts = min(ts, T)
    n1, n2 = _dft_split(S)

    cc, sc = _cos_sin(gw, gw, gw)
    cs = jnp.concatenate([cc, sc], axis=1).astype(BF16)
    a, b = pl.pallas_call(
        functools.partial(_f0_kernel, gw=gw),
        grid=(T // ts,),
        in_specs=[pl.BlockSpec((ts, W), lambda i: (i, 0)), pl.BlockSpec((gw, 2 * gw), lambda i: (0, 0))],
        out_specs=[pl.BlockSpec((ts, W), lambda i: (i, 0))] * 2,
        out_shape=[jax.ShapeDtypeStruct((T, W), F32)] * 2,
        compiler_params=_cparams(("parallel",)),
        name="fourier_channels",
    )(u, cs)

    a3 = a.reshape(B * n2, n1, W)
    b3 = b.reshape(B * n2, n1, W)
    ca, sa = _cos_sin(n2, n2, n2)
    ctw, stw = _cos_sin(n1, n2, S)
    ctw = jnp.broadcast_to(ctw[:, :, None], (n1, n2, LANES))
    stw = jnp.broadcast_to(stw[:, :, None], (n1, n2, LANES))
    tr, ti = pl.pallas_call(
        _fa_kernel,
        grid=(B, n1 // r),
        in_specs=[
            pl.BlockSpec((n2, r, W), lambda bb, j: (bb, j, 0)),
            pl.BlockSpec((n2, r, W), lambda bb, j: (bb, j, 0)),
            pl.BlockSpec((n2, n2), lambda bb, j: (0, 0)),
            pl.BlockSpec((n2, n2), lambda bb, j: (0, 0)),
            pl.BlockSpec((r, n2, LANES), lambda bb, j: (j, 0, 0)),
            pl.BlockSpec((r, n2, LANES), lambda bb, j: (j, 0, 0)),
        ],
        out_specs=[pl.BlockSpec((n2, r, W), lambda bb, j: (bb, j, 0))] * 2,
        out_shape=[jax.ShapeDtypeStruct((B * n2, n1, W), F32)] * 2,
        compiler_params=_cparams(("parallel", "parallel")),
        name="fourier_stage_a",
    )(a3, b3, ca.astype(BF16), sa.astype(BF16), ctw, stw)

    c1, s1 = _cos_sin(n1, n1, n1)
    f3 = pl.pallas_call(
        functools.partial(_fb_kernel, scale=1.0 / math.sqrt(S * gw)),
        grid=(B, n2 // r),
        in_specs=[
            pl.BlockSpec((r, n1, W), lambda bb, k: (bb * (n2 // r) + k, 0, 0)),
            pl.BlockSpec((r, n1, W), lambda bb, k: (bb * (n2 // r) + k, 0, 0)),
            pl.BlockSpec((n1, n1), lambda bb, k: (0, 0)),
            pl.BlockSpec((n1, n1), lambda bb, k: (0, 0)),
        ],
        out_specs=pl.BlockSpec((n1, r, W), lambda bb, k: (bb, k, 0)),
        out_shape=jax.ShapeDtypeStruct((B * n1, n2, W), F32),
        compiler_params=_cparams(("parallel", "parallel")),
        name="fourier_stage_b",
    )(tr, ti, c1.astype(BF16), s1.astype(BF16))
    return f3.reshape(T, W)


def _merge_kernel(at_ref, f_ref, ga_ref, gf_ref, wap_ref, wfp_ref, o_ref):
    a = _dot(at_ref[...], wap_ref[...])
    fo = _dot(f_ref[...].astype(BF16), wfp_ref[...])
    ga = jax.nn.sigmoid(ga_ref[...].astype(F32))
    gf = jax.nn.sigmoid(gf_ref[...].astype(F32))
    o_ref[...] = (ga * a + gf * fo).astype(o_ref.dtype)


def _merge(attn, f, gates, w_ap, w_fp, tm=512, tn=512):
    T, wa = attn.shape
    wf = f.shape[1]
    D = w_ap.shape[1]
    tm = min(tm, T)
    nj = D // tn
    return pl.pallas_call(
        _merge_kernel,
        grid=(T // tm, nj),
        in_specs=[
            pl.BlockSpec((tm, wa), lambda i, j: (i, 0)),
            pl.BlockSpec((tm, wf), lambda i, j: (i, 0)),
            pl.BlockSpec((tm, tn), lambda i, j: (i, j)),
            pl.BlockSpec((tm, tn), lambda i, j: (i, j + nj)),
            pl.BlockSpec((wa, tn), lambda i, j: (0, j)),
            pl.BlockSpec((wf, tn), lambda i, j: (0, j)),
        ],
        out_specs=pl.BlockSpec((tm, tn), lambda i, j: (i, j)),
        out_shape=jax.ShapeDtypeStruct((T, D), BF16),
        compiler_params=_cparams(("parallel", "parallel")),
        name="merge",
    )(attn, f, gates, gates, w_ap, w_fp)


def _outproj_kernel(m_ref, x_ref, w_ref, g_ref, wr_ref, x1_ref, h_ref, aff_ref):
    x1 = x_ref[...] + _dot(m_ref[...], w_ref[...])
    x1_ref[...] = x1
    h = _rms(x1, g_ref[...])
    h_ref[...] = h
    logits = _dot_nt(wr_ref[...], h.astype(BF16))
    ex = jnp.exp(logits - jnp.max(logits, axis=0, keepdims=True))
    aff_ref[...] = ex / jnp.sum(ex, axis=0, keepdims=True)


def _outproj(merged, x, w_out, g, w_router_t, tm=256):
    T, D = x.shape
    E = w_router_t.shape[0]
    tm = min(tm, T)
    return pl.pallas_call(
        _outproj_kernel,
        grid=(T // tm,),
        in_specs=[
            pl.BlockSpec((tm, D), lambda i: (i, 0)),
            pl.BlockSpec((tm, D), lambda i: (i, 0)),
            pl.BlockSpec((D, D), lambda i: (0, 0)),
            pl.BlockSpec((1, D), lambda i: (0, 0)),
            pl.BlockSpec((E, D), lambda i: (0, 0)),
        ],
        out_specs=[
            pl.BlockSpec((tm, D), lambda i: (i, 0)),
            pl.BlockSpec((tm, D), lambda i: (i, 0)),
            pl.BlockSpec((E, tm), lambda i: (0, i)),
        ],
        out_shape=[
            jax.ShapeDtypeStruct((T, D), F32),
            jax.ShapeDtypeStruct((T, D), F32),
            jax.ShapeDtypeStruct((E, T), F32),
        ],
        compiler_params=_cparams(("parallel",)),
        name="outproj_router",
    )(merged, x, w_out, g, w_router_t)


def _route_kernel(aff_ref, idx_ref, gate_ref, q_ref, *, cap):
    E, nb, L = aff_ref.shape
    rows = E * nb
    aff = aff_ref[...]

    def count(mask):
        c = jnp.sum(mask.astype(F32), axis=2, keepdims=True)
        return jnp.sum(c, axis=1, keepdims=True)

    def search(i, prefix):
        cand = prefix | (jnp.int32(1) << (30 - i))
        ge = aff >= lax.bitcast_convert_type(cand, F32)
        return jnp.where(count(ge) >= cap, cand, prefix)

    thr = lax.bitcast_convert_type(lax.fori_loop(0, 31, search, jnp.zeros((E, 1, 1), I32)), F32)

    ri = lax.broadcasted_iota(I32, (L, L), 0)
    ci = lax.broadcasted_iota(I32, (L, L), 1)
    tri = (ri <= ci).astype(BF16)
    ones = jnp.ones((L, L), BF16)
    rr = lax.broadcasted_iota(I32, (rows, rows), 0)
    rc = lax.broadcasted_iota(I32, (rows, rows), 1)
    sh = nb.bit_length() - 1
    same_expert = lax.shift_right_logical(rr, sh) == lax.shift_right_logical(rc, sh)
    earlier = (same_expert & (rc < rr)).astype(BF16)

    def cumsum(mask):
        x = mask.astype(F32).reshape(rows, L).astype(BF16)
        within = _dot(x, tri)
        tot = _dot(x, ones)
        offs = _dot(earlier, tot.astype(BF16))
        return (within + offs).reshape(E, nb, L)

    gt = aff > thr
    eq = aff == thr
    need = cap - count(gt)
    sel = gt | (eq & (cumsum(eq) <= need))
    q_ref[...] = jnp.where(sel, cumsum(sel) - 1.0, -1.0)

    lane = lax.broadcasted_iota(I32, (1, L), 1).astype(F32)
    slot0 = lax.broadcasted_iota(I32, (L, L), 0).astype(F32)

    def per_expert(e, carry):
        for c in range(cap // L):
            slot = slot0 + float(c * L)

            def per_block(tb, acc):
                ai, ag = acc
                hit = q_ref[e, pl.ds(tb, 1), :] == slot
                tok = lane + lax.convert_element_type(tb * L, F32)
                ai = ai + jnp.where(hit, tok, 0.0)
                ag = ag + jnp.where(hit, aff_ref[e, pl.ds(tb, 1), :], 0.0)
                return ai, ag

            zero = jnp.zeros((L, L), F32)
            ai, ag = lax.fori_loop(0, nb, per_block, (zero, zero))
            idx_row = jnp.sum(ai.T, axis=0, keepdims=True)
            idx_ref[e, :, c * L:(c + 1) * L] = idx_row.astype(I32)
            row0 = pl.multiple_of(e * cap + c * L, L)
            gate_ref[pl.ds(row0, L), :] = jnp.sum(ag, axis=1, keepdims=True)
        return carry

    lax.fori_loop(0, E, per_expert, 0)


def _route(aff_t, cap):
    E, T = aff_t.shape
    nb = T // LANES
    aff3 = aff_t.reshape(E, nb, LANES)
    return pl.pallas_call(
        functools.partial(_route_kernel, cap=cap),
        out_shape=[
            jax.ShapeDtypeStruct((E, 1, cap), I32),
            jax.ShapeDtypeStruct((E * cap, 1), F32),
        ],
        scratch_shapes=[pltpu.VMEM((E, nb, LANES), F32)],
        compiler_params=pltpu.CompilerParams(vmem_limit_bytes=VMEM_LIMIT),
        name="route",
    )(aff3)


def _moe_kernel(idx_ref, gate_ref, h_hbm, wg_ref, wu_ref, wd_ref, y_in_hbm, y_hbm, xbuf, ybuf, sems, *, chunk):
    del y_in_hbm
    base = (pl.program_id(0) * pl.num_programs(1) + pl.program_id(1)) * chunk

    def gather_copies(r):
        tok = idx_ref[base + r]
        return (
            pltpu.make_async_copy(h_hbm.at[pl.ds(tok, 1), :], xbuf.at[pl.ds(r, 1), :], sems.at[0]),
            pltpu.make_async_copy(y_hbm.at[pl.ds(tok, 1), :], ybuf.at[pl.ds(r, 1), :], sems.at[1]),
        )

    def scatter_copy(r):
        tok = idx_ref[base + r]
        return pltpu.make_async_copy(ybuf.at[pl.ds(r, 1), :], y_hbm.at[pl.ds(tok, 1), :], sems.at[2])

    def start_gather(r, c):
        for cp in gather_copies(r):
            cp.start()
        return c

    def wait_gather(r, c):
        for cp in gather_copies(r):
            cp.wait()
        return c

    def start_scatter(r, c):
        scatter_copy(r).start()
        return c

    def wait_scatter(r, c):
        scatter_copy(r).wait()
        return c

    lax.fori_loop(0, chunk, start_gather, 0)
    lax.fori_loop(0, chunk, wait_gather, 0)
    xb = xbuf[...].astype(BF16)
    hid = jax.nn.silu(_dot(xb, wg_ref[...])) * _dot(xb, wu_ref[...])
    out = _dot(hid.astype(BF16), wd_ref[...])
    ybuf[...] = ybuf[...] + out * gate_ref[...]
    lax.fori_loop(0, chunk, start_scatter, 0)
    lax.fori_loop(0, chunk, wait_scatter, 0)


def _moe(idx_flat, gate_col, h, y, wg, wu, wd, cap, chunk=512):
    T, D = h.shape
    E, _, Fw = wg.shape
    chunk = min(chunk, cap)
    nc = cap // chunk
    grid_spec = pltpu.PrefetchScalarGridSpec(
        num_scalar_prefetch=1,
        grid=(E, nc),
        in_specs=[
            pl.BlockSpec((chunk, 1), lambda e, c, idx: (e * nc + c, 0)),
            pl.BlockSpec(memory_space=pl.ANY),
            pl.BlockSpec((None, D, Fw), lambda e, c, idx: (e, 0, 0)),
            pl.BlockSpec((None, D, Fw), lambda e, c, idx: (e, 0, 0)),
            pl.BlockSpec((None, Fw, D), lambda e, c, idx: (e, 0, 0)),
            pl.BlockSpec(memory_space=pl.ANY),
        ],
        out_specs=pl.BlockSpec(memory_space=pl.ANY),
        scratch_shapes=[
            pltpu.VMEM((chunk, D), F32),
            pltpu.VMEM((chunk, D), F32),
            pltpu.SemaphoreType.DMA((3,)),
        ],
    )
    return pl.pallas_call(
        functools.partial(_moe_kernel, chunk=chunk),
        grid_spec=grid_spec,
        out_shape=jax.ShapeDtypeStruct((T, D), F32),
        input_output_aliases={6: 0},
        compiler_params=_cparams(("arbitrary", "arbitrary")),
        name="moe_experts",
    )(idx_flat, gate_col, h, wg, wu, wd, y)


def _final_kernel(x_ref, g_ref, o_ref):
    o_ref[...] = _rms(x_ref[...], g_ref[...])


def _final_norm(x, g, tm=512):
    T, D = x.shape
    tm = min(tm, T)
    return pl.pallas_call(
        _final_kernel,
        grid=(T // tm,),
        in_specs=[pl.BlockSpec((tm, D), lambda i: (i, 0)), pl.BlockSpec((1, D), lambda i: (0, 0))],
        out_specs=pl.BlockSpec((tm, D), lambda i: (i, 0)),
        out_shape=jax.ShapeDtypeStruct((T, D), F32),
        compiler_params=_cparams(("parallel",)),
        name="final_norm",
    )(x, g)


def _encoder_group(x, p):
    B, S, D = x.shape
    T = B * S
    hd = p["q_norm_g"].shape[-1]
    wq, wk = N_Q_HEADS * hd, N_KV_HEADS * hd
    wu = N_FOURIER_GROUPS * FOURIER_GROUP_W
    xt = x.reshape(T, D)

    q, k, v, u, gates = _inproj(xt, p["norm_mix_g"], p["w_in"], (wq, wk, wk, wu, 2 * D))
    cos_t, sin_t = _rope_tables(S, hd)
    qr, kr = _qkrope(q, k, p["q_norm_g"], p["k_norm_g"], cos_t, sin_t, S)
    attn = _attention(qr, kr, v, B, S, hd)
    f = _fourier(u, B, S)
    merged = _merge(attn, f, gates, p["w_attn_proj"], p["w_fourier_proj"])
    x1, h2, aff_t = _outproj(merged, xt, p["w_out"], p["norm_moe_g"], p["w_router_t"])

    E = aff_t.shape[0]
    cap = CAPACITY_FACTOR * T // E
    idx, gate_col = _route(aff_t, cap)
    x2 = _moe(idx.reshape(E * cap), gate_col, h2, x1, p["w_expert_gate"], p["w_expert_up"], p["w_expert_down"], cap)
    return _final_norm(x2, p["norm_final_g"]).reshape(B, S, D)


def kernel(x_prompt, x_sample, norm_mix_g, w_in, q_norm_g, k_norm_g, w_attn_proj, w_fourier_proj, w_out, norm_moe_g, w_router, w_expert_gate, w_expert_up, w_expert_down, norm_final_g):
    p = {
        "norm_mix_g": norm_mix_g[0][None, :],
        "w_in": w_in[0].astype(BF16),
        "q_norm_g": q_norm_g[0][None, :],
        "k_norm_g": k_norm_g[0][None, :],
        "w_attn_proj": w_attn_proj[0].astype(BF16),
        "w_fourier_proj": w_fourier_proj[0].astype(BF16),
        "w_out": w_out[0].astype(BF16),
        "norm_moe_g": norm_moe_g[0][None, :],
        "w_router_t": w_router[0].T.astype(BF16),
        "w_expert_gate": w_expert_gate[0].astype(BF16),
        "w_expert_up": w_expert_up[0].astype(BF16),
        "w_expert_down": w_expert_down[0].astype(BF16),
        "norm_final_g": norm_final_g[None, :],
    }
    return (_encoder_group(x_prompt, p), _encoder_group(x_sample, p))
```

```python
import functools
import math

import jax
import jax.numpy as jnp
from jax import lax
from jax.experimental import pallas as pl
from jax.experimental.pallas import tpu as pltpu

F32 = jnp.float32
BF16 = jnp.bfloat16
I32 = jnp.int32

EPS = 1e-6
N_Q_HEADS = 16
N_KV_HEADS = 4
ROPE_GRID_W = 64
ROPE_THETA = 10000.0
N_FOURIER_GROUPS = 4
FOURIER_GROUP_W = 256
CAPACITY_FACTOR = 2
LANES = 128
ATTN_KV_CHUNK = 512
VMEM_LIMIT = 56 * 1024 * 1024


def _cparams(sem):
    return pltpu.CompilerParams(dimension_semantics=sem, vmem_limit_bytes=VMEM_LIMIT)


def _dot(a, b):
    return jnp.dot(a, b, preferred_element_type=F32)


def _dot_nt(a, b):
    return lax.dot_general(a, b, (((1,), (1,)), ((), ())), preferred_element_type=F32)


def _rms(x, g):
    ms = jnp.mean(x * x, axis=-1, keepdims=True)
    return x * lax.rsqrt(ms + EPS) * g


def _inproj_kernel(x_ref, g_ref, w_ref, q_ref, k_ref, v_ref, u_ref, gt_ref, h_ref, *, edges):
    j = pl.program_id(1)

    @pl.when(j == 0)
    def _():
        h_ref[...] = _rms(x_ref[...], g_ref[...]).astype(BF16)

    acc = _dot(h_ref[...], w_ref[...])
    e_q, e_k, e_v, e_u = edges

    @pl.when(j < e_q)
    def _():
        q_ref[...] = acc

    @pl.when((j >= e_q) & (j < e_k))
    def _():
        k_ref[...] = acc

    @pl.when((j >= e_k) & (j < e_v))
    def _():
        v_ref[...] = acc.astype(v_ref.dtype)

    @pl.when((j >= e_v) & (j < e_u))
    def _():
        u_ref[...] = acc.astype(u_ref.dtype)

    @pl.when(j >= e_u)
    def _():
        gt_ref[...] = acc.astype(gt_ref.dtype)


def _inproj(x, g, w, widths, tm=1024, tn=512):
    T, D = x.shape
    tm = min(tm, T)
    wq, wk, wv, wu, wg = widths
    nq, nk, nv, nu, ng = (w_ // tn for w_ in widths)
    assert all(w_ % tn == 0 for w_ in widths)
    e_q, e_k, e_v, e_u = nq, nq + nk, nq + nk + nv, nq + nk + nv + nu
    nj = e_u + ng

    def cmap(lo, n):
        return lambda i, j: (i, jnp.clip(j - lo, 0, n - 1))

    return pl.pallas_call(
        functools.partial(_inproj_kernel, edges=(e_q, e_k, e_v, e_u)),
        grid=(T // tm, nj),
        in_specs=[
            pl.BlockSpec((tm, D), lambda i, j: (i, 0)),
            pl.BlockSpec((1, D), lambda i, j: (0, 0)),
            pl.BlockSpec((D, tn), lambda i, j: (0, j)),
        ],
        out_specs=[
            pl.BlockSpec((tm, tn), cmap(0, nq)),
            pl.BlockSpec((tm, tn), cmap(e_q, nk)),
            pl.BlockSpec((tm, tn), cmap(e_k, nv)),
            pl.BlockSpec((tm, tn), cmap(e_v, nu)),
            pl.BlockSpec((tm, tn), cmap(e_u, ng)),
        ],
        out_shape=[
            jax.ShapeDtypeStruct((T, wq), F32),
            jax.ShapeDtypeStruct((T, wk), F32),
            jax.ShapeDtypeStruct((T, wv), BF16),
            jax.ShapeDtypeStruct((T, wu), BF16),
            jax.ShapeDtypeStruct((T, wg), BF16),
        ],
        scratch_shapes=[pltpu.VMEM((tm, D), BF16)],
        compiler_params=_cparams(("parallel", "arbitrary")),
        name="inproj",
    )(x, g, w)


def _qkrope_kernel(q_ref, k_ref, v_ref, qg_ref, kg_ref, c_ref, s_ref, qo_ref, kto_ref, vo_ref, *, hd, scale):
    c = c_ref[...]
    s = s_ref[...]
    lane = lax.broadcasted_iota(I32, (1, hd), 1)
    low_half = (lane % (hd // 2)) < (hd // 4)

    def one(x, g):
        y = _rms(x, g)
        sw = jnp.where(low_half, pltpu.roll(y, hd - hd // 4, 1), pltpu.roll(y, hd // 4, 1))
        return y * c + sw * s

    for h in range(q_ref.shape[1] // hd):
        sl = slice(h * hd, (h + 1) * hd)
        qo_ref[h] = (one(q_ref[:, sl], qg_ref[...]) * scale).astype(qo_ref.dtype)
    ones = jnp.ones((v_ref.shape[0], hd), vo_ref.dtype)
    for h in range(k_ref.shape[1] // hd):
        sl = slice(h * hd, (h + 1) * hd)
        kto_ref[h, 0] = one(k_ref[:, sl], kg_ref[...]).T.astype(kto_ref.dtype)
        vo_ref[h, :, :hd] = v_ref[:, sl]
        vo_ref[h, :, hd:] = ones


def _qkrope(q, k, v, qg, kg, cos_t, sin_t, S, ts):
    T, wq = q.shape
    wk = k.shape[1]
    hd = qg.shape[-1]
    n_q, n_kv = wq // hd, wk // hd
    ns = S // ts
    return pl.pallas_call(
        functools.partial(_qkrope_kernel, hd=hd, scale=hd ** -0.5 * math.log2(math.e)),
        grid=(T // ts,),
        in_specs=[
            pl.BlockSpec((ts, wq), lambda i: (i, 0)),
            pl.BlockSpec((ts, wk), lambda i: (i, 0)),
            pl.BlockSpec((ts, wk), lambda i: (i, 0)),
            pl.BlockSpec((1, hd), lambda i: (0, 0)),
            pl.BlockSpec((1, hd), lambda i: (0, 0)),
            pl.BlockSpec((ts, hd), lambda i: (i % ns, 0)),
            pl.BlockSpec((ts, hd), lambda i: (i % ns, 0)),
        ],
        out_specs=[
            pl.BlockSpec((n_q, ts, hd), lambda i: (0, i, 0)),
            pl.BlockSpec((n_kv, 1, hd, ts), lambda i: (0, i, 0, 0)),
            pl.BlockSpec((n_kv, ts, 2 * hd), lambda i: (0, i, 0)),
        ],
        out_shape=[
            jax.ShapeDtypeStruct((n_q, T, hd), BF16),
            jax.ShapeDtypeStruct((n_kv, T // ts, hd, ts), BF16),
            jax.ShapeDtypeStruct((n_kv, T, 2 * hd), BF16),
        ],
        compiler_params=_cparams(("parallel",)),
        name="qkrope",
    )(q, k, v, qg, kg, cos_t, sin_t)


def _rope_tables(S, hd):
    axis = hd // 2
    pos = jnp.arange(S, dtype=F32)
    row = jnp.floor(pos / ROPE_GRID_W)
    col = pos - row * ROPE_GRID_W
    inv_freq = ROPE_THETA ** (-jnp.arange(0, axis, 2, dtype=F32) / axis)
    ang_r = row[:, None] * inv_freq
    ang_c = col[:, None] * inv_freq
    cr, sr, cc, sc = jnp.cos(ang_r), jnp.sin(ang_r), jnp.cos(ang_c), jnp.sin(ang_c)
    cos_t = jnp.concatenate([cr, cr, cc, cc], axis=-1)
    sin_t = jnp.concatenate([-sr, sr, -sc, sc], axis=-1)
    return cos_t, sin_t


def _attn_finish(acc, o_ref, G, tq, hd):
    for h in range(G):
        a = acc[h * tq:(h + 1) * tq]
        o_ref[:, h * hd:(h + 1) * hd] = (a[:, :hd] / a[:, hd:hd + 1]).astype(o_ref.dtype)


def _attn_bounded_kernel(q_ref, kt_ref, v_ref, o_ref, acc_ref):
    G, tq, hd = q_ref.shape
    n_chunks, _, tk = kt_ref.shape
    q = q_ref[...].reshape(G * tq, hd)
    acc_ref[...] = jnp.zeros_like(acc_ref)

    def body(c, carry):
        off = pl.multiple_of(c * tk, tk)
        p = jnp.exp2(_dot(q, kt_ref[c])).astype(BF16)
        acc_ref[...] += _dot(p, v_ref[pl.ds(off, tk), :])
        return carry

    lax.fori_loop(0, n_chunks, body, 0)
    _attn_finish(acc_ref[...], o_ref, G, tq, hd)


def _attn_online_kernel(q_ref, kt_ref, v_ref, o_ref, acc_ref, m_ref):
    G, tq, hd = q_ref.shape
    n_chunks, _, tk = kt_ref.shape
    q = q_ref[...].reshape(G * tq, hd)
    acc_ref[...] = jnp.zeros_like(acc_ref)
    m_ref[...] = jnp.full_like(m_ref, -jnp.inf)

    def body(c, carry):
        off = pl.multiple_of(c * tk, tk)
        s = _dot(q, kt_ref[c])
        m = m_ref[...]
        m_new = jnp.maximum(m, jnp.max(s, axis=1, keepdims=True))
        p = jnp.exp2(s - m_new).astype(BF16)
        acc_ref[...] = jnp.exp2(m - m_new) * acc_ref[...] + _dot(p, v_ref[pl.ds(off, tk), :])
        m_ref[...] = m_new
        return carry

    lax.fori_loop(0, n_chunks, body, 0)
    _attn_finish(acc_ref[...], o_ref, G, tq, hd)


def _attention(q, kt, v, B, S, bounded, tq=256):
    n_q, T, hd = q.shape
    n_kv, _, _, tk = kt.shape
    G = n_q // n_kv
    tq = min(tq, S)
    nq = S // tq
    nc = S // tk
    scratch = [pltpu.VMEM((G * tq, 2 * hd), F32)]
    if not bounded:
        scratch.append(pltpu.VMEM((G * tq, 1), F32))
    return pl.pallas_call(
        _attn_bounded_kernel if bounded else _attn_online_kernel,
        grid=(B, n_kv, nq),
        in_specs=[
            pl.BlockSpec((G, tq, hd), lambda b, g, i: (g, b * nq + i, 0)),
            pl.BlockSpec((None, nc, hd, tk), lambda b, g, i: (g, b, 0, 0)),
            pl.BlockSpec((None, S, 2 * hd), lambda b, g, i: (g, b, 0)),
        ],
        out_specs=pl.BlockSpec((tq, G * hd), lambda b, g, i: (b * nq + i, g)),
        out_shape=jax.ShapeDtypeStruct((T, n_q * hd), BF16),
        scratch_shapes=scratch,
        compiler_params=_cparams(("parallel", "parallel", "parallel")),
        name="attention_bounded" if bounded else "attention_online",
    )(q, kt, v)


MAX_UNSHIFTED_LOG2_SCORE = 100.0


def _attention_any(q, kt, v, qg, kg, B, S):
    hd = qg.shape[-1]
    bound = 1.02 * math.sqrt(hd) * math.log2(math.e) * jnp.max(jnp.abs(qg)) * jnp.max(jnp.abs(kg))
    return lax.cond(
        bound <= MAX_UNSHIFTED_LOG2_SCORE,
        lambda: _attention(q, kt, v, B, S, True),
        lambda: _attention(q, kt, v, B, S, False),
    )


def _dft_split(S):
    lg = int(math.log2(S))
    assert 1 << lg == S
    n2 = 1 << ((lg + 1) // 2)
    return S // n2, n2


def _cos_sin(n, m, period):
    ang = (2.0 * math.pi / period) * ((jnp.arange(n, dtype=I32)[:, None] * jnp.arange(m, dtype=I32)[None, :]) % period).astype(F32)
    return jnp.cos(ang), jnp.sin(ang)


def _f0_kernel(u_ref, cs_ref, a_ref, b_ref, *, gw):
    cs = cs_ref[...]
    for g in range(u_ref.shape[1] // gw):
        sl = slice(g * gw, (g + 1) * gw)
        ab = _dot(u_ref[:, sl], cs)
        a_ref[:, sl] = ab[:, :gw]
        b_ref[:, sl] = ab[:, gw:]


def _fa_kernel(a_ref, b_ref, ca_ref, sa_ref, ct_ref, st_ref, tr_ref, ti_ref):
    ca = ca_ref[...]
    sa = sa_ref[...]
    reps = a_ref.shape[2] // LANES
    for j in range(a_ref.shape[1]):
        a = a_ref[:, j, :].astype(BF16)
        b = b_ref[:, j, :].astype(BF16)
        tr = _dot(ca, a) - _dot(sa, b)
        ti = -(_dot(sa, a) + _dot(ca, b))
        ct = jnp.tile(ct_ref[j], (1, reps))
        st = jnp.tile(st_ref[j], (1, reps))
        tr_ref[:, j, :] = tr * ct + ti * st
        ti_ref[:, j, :] = ti * ct - tr * st


def _fb_kernel(tr_ref, ti_ref, c1_ref, s1_ref, f_ref, *, scale):
    c1 = c1_ref[...]
    s1 = s1_ref[...]
    for kk in range(tr_ref.shape[0]):
        xr = tr_ref[kk].astype(BF16)
        xi = ti_ref[kk].astype(BF16)
        f_ref[:, kk, :] = (_dot(c1, xr) + _dot(s1, xi)) * scale


def _fourier(u, B, S, ts=512, r=8):
    T, W = u.shape
    gw = FOURIER_GROUP_W
    ts = min(ts, T)
    n1, n2 = _dft_split(S)

    cc, sc = _cos_sin(gw, gw, gw)
    cs = jnp.concatenate([cc, sc], axis=1).astype(BF16)
    a, b = pl.pallas_call(
        functools.partial(_f0_kernel, gw=gw),
        grid=(T // ts,),
        in_specs=[pl.BlockSpec((ts, W), lambda i: (i, 0)), pl.BlockSpec((gw, 2 * gw), lambda i: (0, 0))],
        out_specs=[pl.BlockSpec((ts, W), lambda i: (i, 0))] * 2,
        out_shape=[jax.ShapeDtypeStruct((T, W), F32)] * 2,
        compiler_params=_cparams(("parallel",)),
        name="fourier_channels",
    )(u, cs)

    a3 = a.reshape(B * n2, n1, W)
    b3 = b.reshape(B * n2, n1, W)
    ca, sa = _cos_sin(n2, n2, n2)
    ctw, stw = _cos_sin(n1, n2, S)
    ctw = jnp.broadcast_to(ctw[:, :, None], (n1, n2, LANES))
    stw = jnp.broadcast_to(stw[:, :, None], (n1, n2, LANES))
    tr, ti = pl.pallas_call(
        _fa_kernel,
        grid=(B, n1 // r),
        in_specs=[
            pl.BlockSpec((n2, r, W), lambda bb, j: (bb, j, 0)),
            pl.BlockSpec((n2, r, W), lambda bb, j: (bb, j, 0)),
            pl.BlockSpec((n2, n2), lambda bb, j: (0, 0)),
            pl.BlockSpec((n2, n2), lambda bb, j: (0, 0)),
            pl.BlockSpec((r, n2, LANES), lambda bb, j: (j, 0, 0)),
            pl.BlockSpec((r, n2, LANES), lambda bb, j: (j, 0, 0)),
        ],
        out_specs=[pl.BlockSpec((n2, r, W), lambda bb, j: (bb, j, 0))] * 2,
        out_shape=[jax.ShapeDtypeStruct((B * n2, n1, W), F32)] * 2,
        compiler_params=_cparams(("parallel", "parallel")),
        name="fourier_stage_a",
    )(a3, b3, ca.astype(BF16), sa.astype(BF16), ctw, stw)

    c1, s1 = _cos_sin(n1, n1, n1)
    f3 = pl.pallas_call(
        functools.partial(_fb_kernel, scale=1.0 / math.sqrt(S * gw)),
        grid=(B, n2 // r),
        in_specs=[
            pl.BlockSpec((r, n1, W), lambda bb, k: (bb * (n2 // r) + k, 0, 0)),
            pl.BlockSpec((r, n1, W), lambda bb, k: (bb * (n2 // r) + k, 0, 0)),
            pl.BlockSpec((n1, n1), lambda bb, k: (0, 0)),
            pl.BlockSpec((n1, n1), lambda bb, k: (0, 0)),
        ],
        out_specs=pl.BlockSpec((n1, r, W), lambda bb, k: (bb, k, 0)),
        out_shape=jax.ShapeDtypeStruct((B * n1, n2, W), F32),
        compiler_params=_cparams(("parallel", "parallel")),
        name="fourier_stage_b",
    )(tr, ti, c1.astype(BF16), s1.astype(BF16))
    return f3.reshape(T, W)


def _merge_kernel(at_ref, f_ref, ga_ref, gf_ref, wap_ref, wfp_ref, o_ref):
    a = _dot(at_ref[...], wap_ref[...])
    fo = _dot(f_ref[...].astype(BF16), wfp_ref[...])
    ga = jax.nn.sigmoid(ga_ref[...].astype(F32))
    gf = jax.nn.sigmoid(gf_ref[...].astype(F32))
    o_ref[...] = (ga * a + gf * fo).astype(o_ref.dtype)


def _merge(attn, f, gates, w_ap, w_fp, tm=512, tn=512):
    T, wa = attn.shape
    wf = f.shape[1]
    D = w_ap.shape[1]
    tm = min(tm, T)
    nj = D // tn
    return pl.pallas_call(
        _merge_kernel,
        grid=(T // tm, nj),
        in_specs=[
            pl.BlockSpec((tm, wa), lambda i, j: (i, 0)),
            pl.BlockSpec((tm, wf), lambda i, j: (i, 0)),
            pl.BlockSpec((tm, tn), lambda i, j: (i, j)),
            pl.BlockSpec((tm, tn), lambda i, j: (i, j + nj)),
            pl.BlockSpec((wa, tn), lambda i, j: (0, j)),
            pl.BlockSpec((wf, tn), lambda i, j: (0, j)),
        ],
        out_specs=pl.BlockSpec((tm, tn), lambda i, j: (i, j)),
        out_shape=jax.ShapeDtypeStruct((T, D), BF16),
        compiler_params=_cparams(("parallel", "parallel")),
        name="merge",
    )(attn, f, gates, gates, w_ap, w_fp)


def _outproj_kernel(m_ref, x_ref, w_ref, g_ref, wr_ref, x1_ref, h_ref, aff_ref):
    x1 = x_ref[...] + _dot(m_ref[...], w_ref[...])
    x1_ref[...] = x1
    h = _rms(x1, g_ref[...])
    h_ref[...] = h
    logits = _dot_nt(wr_ref[...], h.astype(BF16))
    ex = jnp.exp(logits - jnp.max(logits, axis=0, keepdims=True))
    aff_ref[...] = ex / jnp.sum(ex, axis=0, keepdims=True)


def _outproj(merged, x, w_out, g, w_router_t, tm=256):
    T, D = x.shape
    E = w_router_t.shape[0]
    tm = min(tm, T)
    return pl.pallas_call(
        _outproj_kernel,
        grid=(T // tm,),
        in_specs=[
            pl.BlockSpec((tm, D), lambda i: (i, 0)),
            pl.BlockSpec((tm, D), lambda i: (i, 0)),
            pl.BlockSpec((D, D), lambda i: (0, 0)),
            pl.BlockSpec((1, D), lambda i: (0, 0)),
            pl.BlockSpec((E, D), lambda i: (0, 0)),
        ],
        out_specs=[
            pl.BlockSpec((tm, D), lambda i: (i, 0)),
            pl.BlockSpec((tm, D), lambda i: (i, 0)),
            pl.BlockSpec((E, tm), lambda i: (0, i)),
        ],
        out_shape=[
            jax.ShapeDtypeStruct((T, D), F32),
            jax.ShapeDtypeStruct((T, D), F32),
            jax.ShapeDtypeStruct((E, T), F32),
        ],
        compiler_params=_cparams(("parallel",)),
        name="outproj_router",
    )(merged, x, w_out, g, w_router_t)


def _route_kernel(aff_ref, idx_ref, gate_ref, q_ref, *, cap):
    E, nb, L = aff_ref.shape
    rows = E * nb
    aff = aff_ref[...]

    def count(mask):
        c = jnp.sum(mask.astype(F32), axis=2, keepdims=True)
        return jnp.sum(c, axis=1, keepdims=True)

    def search(i, prefix):
        cand = prefix | (jnp.int32(1) << (30 - i))
        ge = aff >= lax.bitcast_convert_type(cand, F32)
        return jnp.where(count(ge) >= cap, cand, prefix)

    thr = lax.bitcast_convert_type(lax.fori_loop(0, 31, search, jnp.zeros((E, 1, 1), I32)), F32)

    ri = lax.broadcasted_iota(I32, (L, L), 0)
    ci = lax.broadcasted_iota(I32, (L, L), 1)
    tri = (ri <= ci).astype(BF16)
    ones = jnp.ones((L, L), BF16)
    rr = lax.broadcasted_iota(I32, (rows, rows), 0)
    rc = lax.broadcasted_iota(I32, (rows, rows), 1)
    sh = nb.bit_length() - 1
    same_expert = lax.shift_right_logical(rr, sh) == lax.shift_right_logical(rc, sh)
    earlier = (same_expert & (rc < rr)).astype(BF16)

    def cumsum(mask):
        x = mask.astype(F32).reshape(rows, L).astype(BF16)
        within = _dot(x, tri)
        tot = _dot(x, ones)
        offs = _dot(earlier, tot.astype(BF16))
        return (within + offs).reshape(E, nb, L)

    gt = aff > thr
    eq = aff == thr
    need = cap - count(gt)
    sel = gt | (eq & (cumsum(eq) <= need))
    q_ref[...] = jnp.where(sel, cumsum(sel) - 1.0, -1.0)

    lane = lax.broadcasted_iota(I32, (1, L), 1).astype(F32)
    slot0 = lax.broadcasted_iota(I32, (L, L), 0).astype(F32)

    def per_expert(e, carry):
        for c in range(cap // L):
            slot = slot0 + float(c * L)

            def per_block(tb, acc):
                ai, ag = acc
                hit = q_ref[e, pl.ds(tb, 1), :] == slot
                tok = lane + lax.convert_element_type(tb * L, F32)
                ai = ai + jnp.where(hit, tok, 0.0)
                ag = ag + jnp.where(hit, aff_ref[e, pl.ds(tb, 1), :], 0.0)
                return ai, ag

            zero = jnp.zeros((L, L), F32)
            ai, ag = lax.fori_loop(0, nb, per_block, (zero, zero))
            idx_row = jnp.sum(ai.T, axis=0, keepdims=True)
            idx_ref[e, :, c * L:(c + 1) * L] = idx_row.astype(I32)
            row0 = pl.multiple_of(e * cap + c * L, L)
            gate_ref[pl.ds(row0, L), :] = jnp.sum(ag, axis=1, keepdims=True)
        return carry

    lax.fori_loop(0, E, per_expert, 0)


def _route(aff_t, cap):
    E, T = aff_t.shape
    nb = T // LANES
    aff3 = aff_t.reshape(E, nb, LANES)
    return pl.pallas_call(
        functools.partial(_route_kernel, cap=cap),
        out_shape=[
            jax.ShapeDtypeStruct((E, 1, cap), I32),
            jax.ShapeDtypeStruct((E * cap, 1), F32),
        ],
        scratch_shapes=[pltpu.VMEM((E, nb, LANES), F32)],
        compiler_params=pltpu.CompilerParams(vmem_limit_bytes=VMEM_LIMIT),
        name="route",
    )(aff3)


def _moe_kernel(idx_ref, gate_ref, h_hbm, wg_ref, wu_ref, wd_ref, y_in_hbm, y_hbm, xbuf, ybuf, sems, *, chunk):
    del y_in_hbm
    base = (pl.program_id(0) * pl.num_programs(1) + pl.program_id(1)) * chunk

    def gather_copies(r):
        tok = idx_ref[base + r]
        return (
            pltpu.make_async_copy(h_hbm.at[pl.ds(tok, 1), :], xbuf.at[pl.ds(r, 1), :], sems.at[0]),
            pltpu.make_async_copy(y_hbm.at[pl.ds(tok, 1), :], ybuf.at[pl.ds(r, 1), :], sems.at[1]),
        )

    def scatter_copy(r):
        tok = idx_ref[base + r]
        return pltpu.make_async_copy(ybuf.at[pl.ds(r, 1), :], y_hbm.at[pl.ds(tok, 1), :], sems.at[2])

    def start_gather(r, c):
        for cp in gather_copies(r):
            cp.start()
        return c

    def wait_gather(r, c):
        for cp in gather_copies(r):
            cp.wait()
        return c

    def start_scatter(r, c):
        scatter_copy(r).start()
        return c

    def wait_scatter(r, c):
        scatter_copy(r).wait()
        return c

    lax.fori_loop(0, chunk, start_gather, 0)
    lax.fori_loop(0, chunk, wait_gather, 0)
    xb = xbuf[...].astype(BF16)
    hid = jax.nn.silu(_dot(xb, wg_ref[...])) * _dot(xb, wu_ref[...])
    out = _dot(hid.astype(BF16), wd_ref[...])
    ybuf[...] = ybuf[...] + out * gate_ref[...]
    lax.fori_loop(0, chunk, start_scatter, 0)
    lax.fori_loop(0, chunk, wait_scatter, 0)


def _moe(idx_flat, gate_col, h, y, wg, wu, wd, cap, chunk=512):
    T, D = h.shape
    E, _, Fw = wg.shape
    chunk = min(chunk, cap)
    nc = cap // chunk
    grid_spec = pltpu.PrefetchScalarGridSpec(
        num_scalar_prefetch=1,
        grid=(E, nc),
        in_specs=[
            pl.BlockSpec((chunk, 1), lambda e, c, idx: (e * nc + c, 0)),
            pl.BlockSpec(memory_space=pl.ANY),
            pl.BlockSpec((None, D, Fw), lambda e, c, idx: (e, 0, 0)),
            pl.BlockSpec((None, D, Fw), lambda e, c, idx: (e, 0, 0)),
            pl.BlockSpec((None, Fw, D), lambda e, c, idx: (e, 0, 0)),
            pl.BlockSpec(memory_space=pl.ANY),
        ],
        out_specs=pl.BlockSpec(memory_space=pl.ANY),
        scratch_shapes=[
            pltpu.VMEM((chunk, D), F32),
            pltpu.VMEM((chunk, D), F32),
            pltpu.SemaphoreType.DMA((3,)),
        ],
    )
    return pl.pallas_call(
        functools.partial(_moe_kernel, chunk=chunk),
        grid_spec=grid_spec,
        out_shape=jax.ShapeDtypeStruct((T, D), F32),
        input_output_aliases={6: 0},
        compiler_params=_cparams(("arbitrary", "arbitrary")),
        name="moe_experts",
    )(idx_flat, gate_col, h, wg, wu, wd, y)


def _final_kernel(x_ref, g_ref, o_ref):
    o_ref[...] = _rms(x_ref[...], g_ref[...])


def _final_norm(x, g, tm=512):
    T, D = x.shape
    tm = min(tm, T)
    return pl.pallas_call(
        _final_kernel,
        grid=(T // tm,),
        in_specs=[pl.BlockSpec((tm, D), lambda i: (i, 0)), pl.BlockSpec((1, D), lambda i: (0, 0))],
        out_specs=pl.BlockSpec((tm, D), lambda i: (i, 0)),
        out_shape=jax.ShapeDtypeStruct((T, D), F32),
        compiler_params=_cparams(("parallel",)),
        name="final_norm",
    )(x, g)


def _encoder_group(x, p):
    B, S, D = x.shape
    T = B * S
    hd = p["q_norm_g"].shape[-1]
    wq, wk = N_Q_HEADS * hd, N_KV_HEADS * hd
    wu = N_FOURIER_GROUPS * FOURIER_GROUP_W
    xt = x.reshape(T, D)

    q, k, v, u, gates = _inproj(xt, p["norm_mix_g"], p["w_in"], (wq, wk, wk, wu, 2 * D))
    cos_t, sin_t = _rope_tables(S, hd)
    qr, kt, va = _qkrope(q, k, v, p["q_norm_g"], p["k_norm_g"], cos_t, sin_t, S, ts=min(ATTN_KV_CHUNK, S))
    attn = _attention_any(qr, kt, va, p["q_norm_g"], p["k_norm_g"], B, S)
    f = _fourier(u, B, S)
    merged = _merge(attn, f, gates, p["w_attn_proj"], p["w_fourier_proj"])
    x1, h2, aff_t = _outproj(merged, xt, p["w_out"], p["norm_moe_g"], p["w_router_t"])

    E = aff_t.shape[0]
    cap = CAPACITY_FACTOR * T // E
    idx, gate_col = _route(aff_t, cap)
    x2 = _moe(idx.reshape(E * cap), gate_col, h2, x1, p["w_expert_gate"], p["w_expert_up"], p["w_expert_down"], cap)
    return _final_norm(x2, p["norm_final_g"]).reshape(B, S, D)


def kernel(x_prompt, x_sample, norm_mix_g, w_in, q_norm_g, k_norm_g, w_attn_proj, w_fourier_proj, w_out, norm_moe_g, w_router, w_expert_gate, w_expert_up, w_expert_down, norm_final_g):
    p = {
        "norm_mix_g": norm_mix_g[0][None, :],
        "w_in": w_in[0].astype(BF16),
        "q_norm_g": q_norm_g[0][None, :],
        "k_norm_g": k_norm_g[0][None, :],
        "w_attn_proj": w_attn_proj[0].astype(BF16),
        "w_fourier_proj": w_fourier_proj[0].astype(BF16),
        "w_out": w_out[0].astype(BF16),
        "norm_moe_g": norm_moe_g[0][None, :],
        "w_router_t": w_router[0].T.astype(BF16),
        "w_expert_gate": w_expert_gate[0].astype(BF16),
        "w_expert_up": w_expert_up[0].astype(BF16),
        "w_expert_down": w_expert_down[0].astype(BF16),
        "norm_final_g": norm_final_g[None, :],
    }
    return (_encoder_group(x_prompt, p), _encoder_group(x_sample, p))
```

```python
import functools
import math

import jax
import jax.numpy as jnp
from jax import lax
from jax.experimental import pallas as pl
from jax.experimental.pallas import tpu as pltpu

F32 = jnp.float32
BF16 = jnp.bfloat16
I32 = jnp.int32

EPS = 1e-6
N_Q_HEADS = 16
N_KV_HEADS = 4
ROPE_GRID_W = 64
ROPE_THETA = 10000.0
N_FOURIER_GROUPS = 4
FOURIER_GROUP_W = 256
CAPACITY_FACTOR = 2
LANES = 128
ATTN_KV_CHUNK = 512
VMEM_LIMIT = 56 * 1024 * 1024


def _cparams(sem):
    return pltpu.CompilerParams(dimension_semantics=sem, vmem_limit_bytes=VMEM_LIMIT)


def _dot(a, b):
    return jnp.dot(a, b, preferred_element_type=F32)


def _dot_nt(a, b):
    return lax.dot_general(a, b, (((1,), (1,)), ((), ())), preferred_element_type=F32)


def _rms(x, g):
    ms = jnp.mean(x * x, axis=-1, keepdims=True)
    return x * lax.rsqrt(ms + EPS) * g


def _inproj_kernel(x_ref, g_ref, w_ref, q_ref, k_ref, v_ref, u_ref, gt_ref, h_ref, *, edges):
    j = pl.program_id(1)

    @pl.when(j == 0)
    def _():
        h_ref[...] = _rms(x_ref[...], g_ref[...]).astype(BF16)

    acc = _dot(h_ref[...], w_ref[...].astype(BF16))
    e_q, e_k, e_v, e_u = edges

    @pl.when(j < e_q)
    def _():
        q_ref[...] = acc

    @pl.when((j >= e_q) & (j < e_k))
    def _():
        k_ref[...] = acc

    @pl.when((j >= e_k) & (j < e_v))
    def _():
        v_ref[...] = acc.astype(v_ref.dtype)

    @pl.when((j >= e_v) & (j < e_u))
    def _():
        u_ref[...] = acc.astype(u_ref.dtype)

    @pl.when(j >= e_u)
    def _():
        gt_ref[...] = acc.astype(gt_ref.dtype)


def _inproj(x, g, w, widths, tm=1024, tn=512):
    T, D = x.shape
    tm = min(tm, T)
    wq, wk, wv, wu, wg = widths
    nq, nk, nv, nu, ng = (w_ // tn for w_ in widths)
    assert all(w_ % tn == 0 for w_ in widths)
    e_q, e_k, e_v, e_u = nq, nq + nk, nq + nk + nv, nq + nk + nv + nu
    nj = e_u + ng

    def cmap(lo, n):
        return lambda i, j: (i, jnp.clip(j - lo, 0, n - 1))

    return pl.pallas_call(
        functools.partial(_inproj_kernel, edges=(e_q, e_k, e_v, e_u)),
        grid=(T // tm, nj),
        in_specs=[
            pl.BlockSpec((tm, D), lambda i, j: (i, 0)),
            pl.BlockSpec((1, D), lambda i, j: (0, 0)),
            pl.BlockSpec((D, tn), lambda i, j: (0, j)),
        ],
        out_specs=[
            pl.BlockSpec((tm, tn), cmap(0, nq)),
            pl.BlockSpec((tm, tn), cmap(e_q, nk)),
            pl.BlockSpec((tm, tn), cmap(e_k, nv)),
            pl.BlockSpec((tm, tn), cmap(e_v, nu)),
            pl.BlockSpec((tm, tn), cmap(e_u, ng)),
        ],
        out_shape=[
            jax.ShapeDtypeStruct((T, wq), F32),
            jax.ShapeDtypeStruct((T, wk), F32),
            jax.ShapeDtypeStruct((T, wv), BF16),
            jax.ShapeDtypeStruct((T, wu), BF16),
            jax.ShapeDtypeStruct((T, wg), BF16),
        ],
        scratch_shapes=[pltpu.VMEM((tm, D), BF16)],
        compiler_params=_cparams(("parallel", "arbitrary")),
        name="inproj",
    )(x, g, w)


def _qkrope_kernel(q_ref, k_ref, v_ref, qg_ref, kg_ref, c_ref, s_ref, qo_ref, kto_ref, vo_ref, *, hd, scale):
    c = c_ref[...]
    s = s_ref[...]
    lane = lax.broadcasted_iota(I32, (1, hd), 1)
    low_half = (lane % (hd // 2)) < (hd // 4)

    def one(x, g):
        y = _rms(x, g)
        sw = jnp.where(low_half, pltpu.roll(y, hd - hd // 4, 1), pltpu.roll(y, hd // 4, 1))
        return y * c + sw * s

    for h in range(q_ref.shape[1] // hd):
        sl = slice(h * hd, (h + 1) * hd)
        qo_ref[h] = (one(q_ref[:, sl], qg_ref[...]) * scale).astype(qo_ref.dtype)
    ones = jnp.ones((v_ref.shape[0], hd), vo_ref.dtype)
    for h in range(k_ref.shape[1] // hd):
        sl = slice(h * hd, (h + 1) * hd)
        kto_ref[h, 0] = one(k_ref[:, sl], kg_ref[...]).T.astype(kto_ref.dtype)
        vo_ref[h, :, :hd] = v_ref[:, sl]
        vo_ref[h, :, hd:] = ones


def _qkrope(q, k, v, qg, kg, cos_t, sin_t, S, ts):
    T, wq = q.shape
    wk = k.shape[1]
    hd = qg.shape[-1]
    n_q, n_kv = wq // hd, wk // hd
    ns = S // ts
    return pl.pallas_call(
        functools.partial(_qkrope_kernel, hd=hd, scale=hd ** -0.5 * math.log2(math.e)),
        grid=(T // ts,),
        in_specs=[
            pl.BlockSpec((ts, wq), lambda i: (i, 0)),
            pl.BlockSpec((ts, wk), lambda i: (i, 0)),
            pl.BlockSpec((ts, wk), lambda i: (i, 0)),
            pl.BlockSpec((1, hd), lambda i: (0, 0)),
            pl.BlockSpec((1, hd), lambda i: (0, 0)),
            pl.BlockSpec((ts, hd), lambda i: (i % ns, 0)),
            pl.BlockSpec((ts, hd), lambda i: (i % ns, 0)),
        ],
        out_specs=[
            pl.BlockSpec((n_q, ts, hd), lambda i: (0, i, 0)),
            pl.BlockSpec((n_kv, 1, hd, ts), lambda i: (0, i, 0, 0)),
            pl.BlockSpec((n_kv, ts, 2 * hd), lambda i: (0, i, 0)),
        ],
        out_shape=[
            jax.ShapeDtypeStruct((n_q, T, hd), BF16),
            jax.ShapeDtypeStruct((n_kv, T // ts, hd, ts), BF16),
            jax.ShapeDtypeStruct((n_kv, T, 2 * hd), BF16),
        ],
        compiler_params=_cparams(("parallel",)),
        name="qkrope",
    )(q, k, v, qg, kg, cos_t, sin_t)


def _rope_tables(S, hd):
    axis = hd // 2
    pos = jnp.arange(S, dtype=F32)
    row = jnp.floor(pos / ROPE_GRID_W)
    col = pos - row * ROPE_GRID_W
    inv_freq = ROPE_THETA ** (-jnp.arange(0, axis, 2, dtype=F32) / axis)
    ang_r = row[:, None] * inv_freq
    ang_c = col[:, None] * inv_freq
    cr, sr, cc, sc = jnp.cos(ang_r), jnp.sin(ang_r), jnp.cos(ang_c), jnp.sin(ang_c)
    cos_t = jnp.concatenate([cr, cr, cc, cc], axis=-1)
    sin_t = jnp.concatenate([-sr, sr, -sc, sc], axis=-1)
    return cos_t, sin_t


def _attn_finish(acc, o_ref, G, tq, hd):
    for h in range(G):
        a = acc[h * tq:(h + 1) * tq]
        o_ref[:, h * hd:(h + 1) * hd] = (a[:, :hd] / a[:, hd:hd + 1]).astype(o_ref.dtype)


def _attn_bounded_kernel(q_ref, kt_ref, v_ref, o_ref, acc_ref):
    G, tq, hd = q_ref.shape
    n_chunks, _, tk = kt_ref.shape
    q = q_ref[...].reshape(G * tq, hd)
    acc_ref[...] = jnp.zeros_like(acc_ref)

    def body(c, carry):
        off = pl.multiple_of(c * tk, tk)
        p = jnp.exp2(_dot(q, kt_ref[c])).astype(BF16)
        acc_ref[...] += _dot(p, v_ref[pl.ds(off, tk), :])
        return carry

    lax.fori_loop(0, n_chunks, body, 0)
    _attn_finish(acc_ref[...], o_ref, G, tq, hd)


def _attn_online_kernel(q_ref, kt_ref, v_ref, o_ref, acc_ref, m_ref):
    G, tq, hd = q_ref.shape
    n_chunks, _, tk = kt_ref.shape
    q = q_ref[...].reshape(G * tq, hd)
    acc_ref[...] = jnp.zeros_like(acc_ref)
    m_ref[...] = jnp.full_like(m_ref, -jnp.inf)

    def body(c, carry):
        off = pl.multiple_of(c * tk, tk)
        s = _dot(q, kt_ref[c])
        m = m_ref[...]
        m_new = jnp.maximum(m, jnp.max(s, axis=1, keepdims=True))
        p = jnp.exp2(s - m_new).astype(BF16)
        acc_ref[...] = jnp.exp2(m - m_new) * acc_ref[...] + _dot(p, v_ref[pl.ds(off, tk), :])
        m_ref[...] = m_new
        return carry

    lax.fori_loop(0, n_chunks, body, 0)
    _attn_finish(acc_ref[...], o_ref, G, tq, hd)


def _attention(q, kt, v, B, S, bounded, tq=256):
    n_q, T, hd = q.shape
    n_kv, _, _, tk = kt.shape
    G = n_q // n_kv
    tq = min(tq, S)
    nq = S // tq
    nc = S // tk
    scratch = [pltpu.VMEM((G * tq, 2 * hd), F32)]
    if not bounded:
        scratch.append(pltpu.VMEM((G * tq, 1), F32))
    return pl.pallas_call(
        _attn_bounded_kernel if bounded else _attn_online_kernel,
        grid=(B, n_kv, nq),
        in_specs=[
            pl.BlockSpec((G, tq, hd), lambda b, g, i: (g, b * nq + i, 0)),
            pl.BlockSpec((None, nc, hd, tk), lambda b, g, i: (g, b, 0, 0)),
            pl.BlockSpec((None, S, 2 * hd), lambda b, g, i: (g, b, 0)),
        ],
        out_specs=pl.BlockSpec((tq, G * hd), lambda b, g, i: (b * nq + i, g)),
        out_shape=jax.ShapeDtypeStruct((T, n_q * hd), BF16),
        scratch_shapes=scratch,
        compiler_params=_cparams(("parallel", "parallel", "parallel")),
        name="attention_bounded" if bounded else "attention_online",
    )(q, kt, v)


MAX_UNSHIFTED_LOG2_SCORE = 100.0


def _attention_any(q, kt, v, qg, kg, B, S):
    hd = qg.shape[-1]
    bound = 1.02 * math.sqrt(hd) * math.log2(math.e) * jnp.max(jnp.abs(qg)) * jnp.max(jnp.abs(kg))
    return lax.cond(
        bound <= MAX_UNSHIFTED_LOG2_SCORE,
        lambda: _attention(q, kt, v, B, S, True),
        lambda: _attention(q, kt, v, B, S, False),
    )


def _dft_split(S):
    lg = int(math.log2(S))
    assert 1 << lg == S
    n2 = 1 << ((lg + 1) // 2)
    return S // n2, n2


def _cos_sin(n, m, period):
    ang = (2.0 * math.pi / period) * ((jnp.arange(n, dtype=I32)[:, None] * jnp.arange(m, dtype=I32)[None, :]) % period).astype(F32)
    return jnp.cos(ang), jnp.sin(ang)


def _f0_kernel(u_ref, cs_ref, a_ref, b_ref, *, gw):
    cs = cs_ref[...]
    for g in range(u_ref.shape[1] // gw):
        sl = slice(g * gw, (g + 1) * gw)
        ab = _dot(u_ref[:, sl], cs)
        a_ref[:, sl] = ab[:, :gw]
        b_ref[:, sl] = ab[:, gw:]


def _fa_kernel(a_ref, b_ref, ca_ref, sa_ref, ct_ref, st_ref, tr_ref, ti_ref):
    ca = ca_ref[...]
    sa = sa_ref[...]
    reps = a_ref.shape[2] // LANES
    for j in range(a_ref.shape[1]):
        a = a_ref[:, j, :].astype(BF16)
        b = b_ref[:, j, :].astype(BF16)
        tr = _dot(ca, a) - _dot(sa, b)
        ti = -(_dot(sa, a) + _dot(ca, b))
        ct = jnp.tile(ct_ref[j], (1, reps))
        st = jnp.tile(st_ref[j], (1, reps))
        tr_ref[:, j, :] = tr * ct + ti * st
        ti_ref[:, j, :] = ti * ct - tr * st


def _fb_kernel(tr_ref, ti_ref, c1_ref, s1_ref, f_ref, *, scale):
    c1 = c1_ref[...]
    s1 = s1_ref[...]
    for kk in range(tr_ref.shape[0]):
        xr = tr_ref[kk].astype(BF16)
        xi = ti_ref[kk].astype(BF16)
        f_ref[:, kk, :] = (_dot(c1, xr) + _dot(s1, xi)) * scale


def _fourier(u, B, S, ts=512, r=8):
    T, W = u.shape
    gw = FOURIER_GROUP_W
    ts = min(ts, T)
    n1, n2 = _dft_split(S)

    cc, sc = _cos_sin(gw, gw, gw)
    cs = jnp.concatenate([cc, sc], axis=1).astype(BF16)
    a, b = pl.pallas_call(
        functools.partial(_f0_kernel, gw=gw),
        grid=(T // ts,),
        in_specs=[pl.BlockSpec((ts, W), lambda i: (i, 0)), pl.BlockSpec((gw, 2 * gw), lambda i: (0, 0))],
        out_specs=[pl.BlockSpec((ts, W), lambda i: (i, 0))] * 2,
        out_shape=[jax.ShapeDtypeStruct((T, W), F32)] * 2,
        compiler_params=_cparams(("parallel",)),
        name="fourier_channels",
    )(u, cs)

    a3 = a.reshape(B * n2, n1, W)
    b3 = b.reshape(B * n2, n1, W)
    ca, sa = _cos_sin(n2, n2, n2)
    ctw, stw = _cos_sin(n1, n2, S)
    ctw = jnp.broadcast_to(ctw[:, :, None], (n1, n2, LANES))
    stw = jnp.broadcast_to(stw[:, :, None], (n1, n2, LANES))
    tr, ti = pl.pallas_call(
        _fa_kernel,
        grid=(B, n1 // r),
        in_specs=[
            pl.BlockSpec((n2, r, W), lambda bb, j: (bb, j, 0)),
            pl.BlockSpec((n2, r, W), lambda bb, j: (bb, j, 0)),
            pl.BlockSpec((n2, n2), lambda bb, j: (0, 0)),
            pl.BlockSpec((n2, n2), lambda bb, j: (0, 0)),
            pl.BlockSpec((r, n2, LANES), lambda bb, j: (j, 0, 0)),
            pl.BlockSpec((r, n2, LANES), lambda bb, j: (j, 0, 0)),
        ],
        out_specs=[pl.BlockSpec((n2, r, W), lambda bb, j: (bb, j, 0))] * 2,
        out_shape=[jax.ShapeDtypeStruct((B * n2, n1, W), F32)] * 2,
        compiler_params=_cparams(("parallel", "parallel")),
        name="fourier_stage_a",
    )(a3, b3, ca.astype(BF16), sa.astype(BF16), ctw, stw)

    c1, s1 = _cos_sin(n1, n1, n1)
    f3 = pl.pallas_call(
        functools.partial(_fb_kernel, scale=1.0 / math.sqrt(S * gw)),
        grid=(B, n2 // r),
        in_specs=[
            pl.BlockSpec((r, n1, W), lambda bb, k: (bb * (n2 // r) + k, 0, 0)),
            pl.BlockSpec((r, n1, W), lambda bb, k: (bb * (n2 // r) + k, 0, 0)),
            pl.BlockSpec((n1, n1), lambda bb, k: (0, 0)),
            pl.BlockSpec((n1, n1), lambda bb, k: (0, 0)),
        ],
        out_specs=pl.BlockSpec((n1, r, W), lambda bb, k: (bb, k, 0)),
        out_shape=jax.ShapeDtypeStruct((B * n1, n2, W), F32),
        compiler_params=_cparams(("parallel", "parallel")),
        name="fourier_stage_b",
    )(tr, ti, c1.astype(BF16), s1.astype(BF16))
    return f3.reshape(T, W)


def _merge_kernel(at_ref, f_ref, ga_ref, gf_ref, wap_ref, wfp_ref, o_ref):
    a = _dot(at_ref[...], wap_ref[...])
    fo = _dot(f_ref[...].astype(BF16), wfp_ref[...])
    ga = jax.nn.sigmoid(ga_ref[...].astype(F32))
    gf = jax.nn.sigmoid(gf_ref[...].astype(F32))
    o_ref[...] = (ga * a + gf * fo).astype(o_ref.dtype)


def _merge(attn, f, gates, w_ap, w_fp, tm=512, tn=512):
    T, wa = attn.shape
    wf = f.shape[1]
    D = w_ap.shape[1]
    tm = min(tm, T)
    nj = D // tn
    return pl.pallas_call(
        _merge_kernel,
        grid=(T // tm, nj),
        in_specs=[
            pl.BlockSpec((tm, wa), lambda i, j: (i, 0)),
            pl.BlockSpec((tm, wf), lambda i, j: (i, 0)),
            pl.BlockSpec((tm, tn), lambda i, j: (i, j)),
            pl.BlockSpec((tm, tn), lambda i, j: (i, j + nj)),
            pl.BlockSpec((wa, tn), lambda i, j: (0, j)),
            pl.BlockSpec((wf, tn), lambda i, j: (0, j)),
        ],
        out_specs=pl.BlockSpec((tm, tn), lambda i, j: (i, j)),
        out_shape=jax.ShapeDtypeStruct((T, D), BF16),
        compiler_params=_cparams(("parallel", "parallel")),
        name="merge",
    )(attn, f, gates, gates, w_ap, w_fp)


def _outproj_kernel(m_ref, x_ref, w_ref, g_ref, wrt_ref, wr_ref, x1_ref, h_ref, afft_ref, aff_ref):
    x1 = x_ref[...] + _dot(m_ref[...], w_ref[...])
    x1_ref[...] = x1
    h = _rms(x1, g_ref[...]).astype(BF16)
    h_ref[...] = h
    lt = _dot_nt(wrt_ref[...], h)
    et = jnp.exp(lt - jnp.max(lt, axis=0, keepdims=True))
    afft_ref[...] = et / jnp.sum(et, axis=0, keepdims=True)
    lg = _dot(h, wr_ref[...])
    eg = jnp.exp(lg - jnp.max(lg, axis=1, keepdims=True))
    aff_ref[...] = eg / jnp.sum(eg, axis=1, keepdims=True)


def _outproj(merged, x, w_out, g, w_router_t, w_router, tm=256):
    T, D = x.shape
    E = w_router_t.shape[0]
    tm = min(tm, T)
    return pl.pallas_call(
        _outproj_kernel,
        grid=(T // tm,),
        in_specs=[
            pl.BlockSpec((tm, D), lambda i: (i, 0)),
            pl.BlockSpec((tm, D), lambda i: (i, 0)),
            pl.BlockSpec((D, D), lambda i: (0, 0)),
            pl.BlockSpec((1, D), lambda i: (0, 0)),
            pl.BlockSpec((E, D), lambda i: (0, 0)),
            pl.BlockSpec((D, E), lambda i: (0, 0)),
        ],
        out_specs=[
            pl.BlockSpec((tm, D), lambda i: (i, 0)),
            pl.BlockSpec((tm, D), lambda i: (i, 0)),
            pl.BlockSpec((E, tm), lambda i: (0, i)),
            pl.BlockSpec((tm, E), lambda i: (i, 0)),
        ],
        out_shape=[
            jax.ShapeDtypeStruct((T, D), F32),
            jax.ShapeDtypeStruct((T, D), BF16),
            jax.ShapeDtypeStruct((E, T), F32),
            jax.ShapeDtypeStruct((T, E), F32),
        ],
        compiler_params=_cparams(("parallel",)),
        name="outproj_router",
    )(merged, x, w_out, g, w_router_t, w_router)


def _route_kernel(aff_ref, q_ref, qt_ref, offs_ref, *, cap):
    E, nb, L = aff_ref.shape
    rows = E * nb
    aff = aff_ref[...]

    def count(mask):
        c = jnp.sum(mask.astype(F32), axis=2, keepdims=True)
        return jnp.sum(c, axis=1, keepdims=True)

    def search(i, prefix):
        cand = prefix | (jnp.int32(1) << (30 - i))
        ge = aff >= lax.bitcast_convert_type(cand, F32)
        return jnp.where(count(ge) >= cap, cand, prefix)

    thr = lax.bitcast_convert_type(lax.fori_loop(0, 31, search, jnp.zeros((E, 1, 1), I32)), F32)

    ri = lax.broadcasted_iota(I32, (L, L), 0)
    ci = lax.broadcasted_iota(I32, (L, L), 1)
    tri = (ri <= ci).astype(BF16)
    ones = jnp.ones((L, L), BF16)
    rr = lax.broadcasted_iota(I32, (rows, rows), 0)
    rc = lax.broadcasted_iota(I32, (rows, rows), 1)
    sh = nb.bit_length() - 1
    same_expert = lax.shift_right_logical(rr, sh) == lax.shift_right_logical(rc, sh)
    earlier = (same_expert & (rc < rr)).astype(BF16)

    def cumsum(mask):
        x = mask.astype(F32).reshape(rows, L).astype(BF16)
        within = _dot(x, tri)
        tot = _dot(x, ones)
        offs = _dot(earlier, tot.astype(BF16))
        return (within + offs).reshape(E, nb, L), offs

    gt = aff > thr
    eq = aff == thr
    need = cap - count(gt)
    sel = gt | (eq & (cumsum(eq)[0] <= need))
    rank, offs = cumsum(sel)
    q_ref[...] = jnp.where(sel, rank - 1.0, -1.0)
    offs_ref[...] = offs

    filler = jnp.full((L - E, L), -1.0, F32)
    for b in range(nb):
        tile = jnp.concatenate([q_ref[:, b, :], filler], axis=0)
        qt_ref[b * L:(b + 1) * L, :] = tile.T


def _route(aff_t, cap):
    E, T = aff_t.shape
    nb = T // LANES
    aff3 = aff_t.reshape(E, nb, LANES)
    return pl.pallas_call(
        functools.partial(_route_kernel, cap=cap),
        out_shape=[
            jax.ShapeDtypeStruct((E, nb, LANES), F32),
            jax.ShapeDtypeStruct((T, LANES), F32),
            jax.ShapeDtypeStruct((E * nb, LANES), F32),
        ],
        compiler_params=pltpu.CompilerParams(vmem_limit_bytes=VMEM_LIMIT),
        name="route",
    )(aff3)


MOE_TOKEN_BLOCK = 256
MOE_SLOT_CHUNK = 256
MOE_WINDOW = 64


def _gather_kernel(cnt_ref, q_ref, h_ref, o_ref, acc_ref, *, cap):
    e = pl.program_id(1)
    nbk, kb = q_ref.shape
    sc = acc_ref.shape[0]
    base = e * (nbk + 1)
    slot_col = lax.broadcasted_iota(I32, (sc, 1), 0).astype(F32)
    for c in range(cap // sc):
        lo_slot, hi_slot = c * sc, (c + 1) * sc
        b_lo = lax.fori_loop(0, nbk, lambda b, n: n + jnp.where(cnt_ref[base + b + 1] <= lo_slot, 1, 0), 0)
        b_hi = lax.fori_loop(0, nbk, lambda b, n: n + jnp.where(cnt_ref[base + b] < hi_slot, 1, 0), 0)
        acc_ref[...] = jnp.zeros_like(acc_ref)

        def body(b, carry, lo_slot=lo_slot):
            hit = q_ref[pl.ds(b, 1), :] == slot_col + float(lo_slot)
            rows = h_ref[pl.ds(pl.multiple_of(b * kb, kb), kb), :]
            acc_ref[...] += _dot(jnp.where(hit, 1.0, 0.0).astype(BF16), rows)
            return carry

        lax.fori_loop(b_lo, b_hi, body, 0)
        o_ref[lo_slot:hi_slot, :] = acc_ref[...].astype(o_ref.dtype)


def _gather(cnt, q, h, cap, dn=1024):
    E, nbk, kb = q.shape
    T, D = h.shape
    sc = min(MOE_SLOT_CHUNK, cap)
    dn = min(dn, D)
    grid_spec = pltpu.PrefetchScalarGridSpec(
        num_scalar_prefetch=1,
        grid=(D // dn, E),
        in_specs=[
            pl.BlockSpec((None, nbk, kb), lambda j, e, cnt: (e, 0, 0)),
            pl.BlockSpec((T, dn), lambda j, e, cnt: (0, j)),
        ],
        out_specs=pl.BlockSpec((cap, dn), lambda j, e, cnt: (e, j)),
        scratch_shapes=[pltpu.VMEM((sc, dn), F32)],
    )
    return pl.pallas_call(
        functools.partial(_gather_kernel, cap=cap),
        grid_spec=grid_spec,
        out_shape=jax.ShapeDtypeStruct((E * cap, D), BF16),
        compiler_params=_cparams(("parallel", "parallel")),
        name="moe_gather",
    )(cnt, q, h)


def _ffn_kernel(x_ref, wg_ref, wu_ref, wd_ref, o_ref, acc_ref):
    f = pl.program_id(1)
    x = x_ref[...]
    hid = jax.nn.silu(_dot(x, wg_ref[...].astype(BF16))) * _dot(x, wu_ref[...].astype(BF16))
    part = _dot(hid.astype(BF16), wd_ref[...].astype(BF16))

    @pl.when(f == 0)
    def _():
        acc_ref[...] = part

    @pl.when(f > 0)
    def _():
        acc_ref[...] += part

    @pl.when(f == pl.num_programs(1) - 1)
    def _():
        o_ref[...] = acc_ref[...].astype(o_ref.dtype)


def _ffn(xg, wg, wu, wd, cap, fn=256):
    E, D, Fw = wg.shape
    fn = min(fn, Fw)
    return pl.pallas_call(
        _ffn_kernel,
        grid=(E, Fw // fn),
        in_specs=[
            pl.BlockSpec((cap, D), lambda e, f: (e, 0)),
            pl.BlockSpec((None, D, fn), lambda e, f: (e, 0, f)),
            pl.BlockSpec((None, D, fn), lambda e, f: (e, 0, f)),
            pl.BlockSpec((None, fn, D), lambda e, f: (e, f, 0)),
        ],
        out_specs=pl.BlockSpec((cap, D), lambda e, f: (e, 0)),
        out_shape=jax.ShapeDtypeStruct((E * cap, D), BF16),
        scratch_shapes=[pltpu.VMEM((cap, D), F32)],
        compiler_params=_cparams(("parallel", "arbitrary")),
        name="moe_ffn",
    )(xg, wg, wu, wd)


def _combine_kernel(cnt_ref, x_ref, qt_ref, aff_ref, o_hbm, g_ref, y_ref, obuf, acc_ref, sem, *, cap, nbk):
    b = pl.program_id(0)
    E = aff_ref.shape[1]
    W = obuf.shape[0] // E
    lo = [cnt_ref[e * (nbk + 1) + b] for e in range(E)]
    hi = [cnt_ref[e * (nbk + 1) + b + 1] for e in range(E)]
    lo = [lax.shift_left(lax.shift_right_logical(v, 4), 4) for v in lo]
    span = functools.reduce(jnp.maximum, [h - l for h, l in zip(hi, lo)])
    n_pass = lax.shift_right_logical(span + (W - 1), W.bit_length() - 1)
    acc_ref[...] = x_ref[...]
    lane = lax.broadcasted_iota(I32, (1, 2 * W), 1)
    first = lane < W
    j = jnp.where(first, lane, lane - W).astype(F32)

    def one_pass(p, carry):
        nominal = [l + p * W for l in lo]
        start = [jnp.minimum(n, cap - W) for n in nominal]
        copies = [
            pltpu.make_async_copy(
                o_hbm.at[pl.ds(pl.multiple_of(e * cap + start[e], 16), W), :],
                obuf.at[pl.ds(e * W, W), :],
                sem,
            )
            for e in range(E)
        ]
        for cp in copies:
            cp.start()
        seg_hi, seg_lo = [], []
        for e in range(0, E, 2):
            slot = jnp.where(first, qt_ref[:, e:e + 1], qt_ref[:, e + 1:e + 2])
            gate = jnp.where(first, aff_ref[:, e:e + 1], aff_ref[:, e + 1:e + 2])
            row0 = jnp.where(first, start[e].astype(F32), start[e + 1].astype(F32))
            own0 = jnp.where(first, nominal[e].astype(F32), nominal[e + 1].astype(F32))
            hit = (slot == row0 + j) & (slot >= own0) & (slot < own0 + float(W))
            val = jnp.where(hit, gate, 0.0)
            vhi = val.astype(BF16)
            seg_hi.append(vhi)
            seg_lo.append((val - vhi.astype(F32)).astype(BF16))
        for cp in copies:
            cp.wait()
        rows = obuf[...]
        acc_ref[...] += _dot(jnp.concatenate(seg_hi, axis=1), rows) + _dot(jnp.concatenate(seg_lo, axis=1), rows)
        return carry

    lax.fori_loop(0, n_pass, one_pass, 0)
    y_ref[...] = _rms(acc_ref[...], g_ref[...])


def _combine(cnt, x1, qt, aff, o, g, cap):
    T, D = x1.shape
    E = aff.shape[1]
    tb = min(MOE_TOKEN_BLOCK, T)
    W = min(MOE_WINDOW, cap)
    grid_spec = pltpu.PrefetchScalarGridSpec(
        num_scalar_prefetch=1,
        grid=(T // tb,),
        in_specs=[
            pl.BlockSpec((tb, D), lambda b, cnt: (b, 0)),
            pl.BlockSpec((tb, LANES), lambda b, cnt: (b, 0)),
            pl.BlockSpec((tb, E), lambda b, cnt: (b, 0)),
            pl.BlockSpec(memory_space=pl.ANY),
            pl.BlockSpec((1, D), lambda b, cnt: (0, 0)),
        ],
        out_specs=pl.BlockSpec((tb, D), lambda b, cnt: (b, 0)),
        scratch_shapes=[
            pltpu.VMEM((E * W, D), BF16),
            pltpu.VMEM((tb, D), F32),
            pltpu.SemaphoreType.DMA(()),
        ],
    )
    return pl.pallas_call(
        functools.partial(_combine_kernel, cap=cap, nbk=T // tb),
        grid_spec=grid_spec,
        out_shape=jax.ShapeDtypeStruct((T, D), F32),
        compiler_params=_cparams(("parallel",)),
        name="moe_combine_norm",
    )(cnt, x1, qt, aff, o, g)


def _expert_choice_moe(x1, h2, aff_t, aff, p):
    T, D = x1.shape
    E = aff_t.shape[0]
    cap = CAPACITY_FACTOR * T // E
    q, qt, offs = _route(aff_t, cap)
    nb = T // LANES
    per_blk = min(MOE_TOKEN_BLOCK, T) // LANES
    cnt = offs[:, 0].reshape(E, nb)[:, ::per_blk]
    cnt = jnp.concatenate([cnt, jnp.full((E, 1), cap, F32)], axis=1).astype(I32).reshape(-1)
    xg = _gather(cnt, q.reshape(E, nb // per_blk, per_blk * LANES), h2, cap)
    o = _ffn(xg, p["w_expert_gate"], p["w_expert_up"], p["w_expert_down"], cap)
    return _combine(cnt, x1, qt, aff, o, p["norm_final_g"], cap)


def _encoder_group(x, p):
    B, S, D = x.shape
    T = B * S
    hd = p["q_norm_g"].shape[-1]
    wq, wk = N_Q_HEADS * hd, N_KV_HEADS * hd
    wu = N_FOURIER_GROUPS * FOURIER_GROUP_W
    xt = x.reshape(T, D)

    q, k, v, u, gates = _inproj(xt, p["norm_mix_g"], p["w_in"], (wq, wk, wk, wu, 2 * D))
    cos_t, sin_t = _rope_tables(S, hd)
    qr, kt, va = _qkrope(q, k, v, p["q_norm_g"], p["k_norm_g"], cos_t, sin_t, S, ts=min(ATTN_KV_CHUNK, S))
    attn = _attention_any(qr, kt, va, p["q_norm_g"], p["k_norm_g"], B, S)
    f = _fourier(u, B, S)
    merged = _merge(attn, f, gates, p["w_attn_proj"], p["w_fourier_proj"])
    x1, h2, aff_t, aff = _outproj(merged, xt, p["w_out"], p["norm_moe_g"], p["w_router_t"], p["w_router"])
    return _expert_choice_moe(x1, h2, aff_t, aff, p).reshape(B, S, D)


def kernel(x_prompt, x_sample, norm_mix_g, w_in, q_norm_g, k_norm_g, w_attn_proj, w_fourier_proj, w_out, norm_moe_g, w_router, w_expert_gate, w_expert_up, w_expert_down, norm_final_g):
    p = {
        "norm_mix_g": norm_mix_g[0][None, :],
        "w_in": w_in[0],
        "q_norm_g": q_norm_g[0][None, :],
        "k_norm_g": k_norm_g[0][None, :],
        "w_attn_proj": w_attn_proj[0].astype(BF16),
        "w_fourier_proj": w_fourier_proj[0].astype(BF16),
        "w_out": w_out[0].astype(BF16),
        "norm_moe_g": norm_moe_g[0][None, :],
        "w_router_t": w_router[0].T.astype(BF16),
        "w_router": w_router[0].astype(BF16),
        "w_expert_gate": w_expert_gate[0],
        "w_expert_up": w_expert_up[0],
        "w_expert_down": w_expert_down[0],
        "norm_final_g": norm_final_g[None, :],
    }
    return (_encoder_group(x_prompt, p), _encoder_group(x_sample, p))
```

```python
import functools
import math

import jax
import jax.numpy as jnp
from jax import lax
from jax.experimental import pallas as pl
from jax.experimental.pallas import tpu as pltpu

F32 = jnp.float32
BF16 = jnp.bfloat16
I32 = jnp.int32

EPS = 1e-6
N_Q_HEADS = 16
N_KV_HEADS = 4
ROPE_GRID_W = 64
ROPE_THETA = 10000.0
N_FOURIER_GROUPS = 4
FOURIER_GROUP_W = 256
CAPACITY_FACTOR = 2
LANES = 128
ATTN_KV_CHUNK = 512
VMEM_LIMIT = 56 * 1024 * 1024


def _cparams(sem):
    return pltpu.CompilerParams(dimension_semantics=sem, vmem_limit_bytes=VMEM_LIMIT)


def _dot(a, b):
    return jnp.dot(a, b, preferred_element_type=F32)


def _dot_nt(a, b):
    return lax.dot_general(a, b, (((1,), (1,)), ((), ())), preferred_element_type=F32)


def _rms(x, g):
    ms = jnp.mean(x * x, axis=-1, keepdims=True)
    return x * lax.rsqrt(ms + EPS) * g


def _inproj_kernel(x_ref, g_ref, w_ref, q_ref, k_ref, v_ref, u_ref, gt_ref, h_ref, *, edges):
    j = pl.program_id(1)

    @pl.when(j == 0)
    def _():
        h_ref[...] = _rms(x_ref[...], g_ref[...]).astype(BF16)

    acc = _dot(h_ref[...], w_ref[...].astype(BF16))
    e_q, e_k, e_v, e_u = edges

    @pl.when(j < e_q)
    def _():
        q_ref[...] = acc

    @pl.when((j >= e_q) & (j < e_k))
    def _():
        k_ref[...] = acc

    @pl.when((j >= e_k) & (j < e_v))
    def _():
        v_ref[...] = acc.astype(v_ref.dtype)

    @pl.when((j >= e_v) & (j < e_u))
    def _():
        u_ref[...] = acc.astype(u_ref.dtype)

    @pl.when(j >= e_u)
    def _():
        gt_ref[...] = acc.astype(gt_ref.dtype)


def _inproj(x, g, w, widths, tm=1024, tn=512):
    T, D = x.shape
    tm = min(tm, T)
    wq, wk, wv, wu, wg = widths
    nq, nk, nv, nu, ng = (w_ // tn for w_ in widths)
    assert all(w_ % tn == 0 for w_ in widths)
    e_q, e_k, e_v, e_u = nq, nq + nk, nq + nk + nv, nq + nk + nv + nu
    nj = e_u + ng

    def cmap(lo, n):
        return lambda i, j: (i, jnp.clip(j - lo, 0, n - 1))

    return pl.pallas_call(
        functools.partial(_inproj_kernel, edges=(e_q, e_k, e_v, e_u)),
        grid=(T // tm, nj),
        in_specs=[
            pl.BlockSpec((tm, D), lambda i, j: (i, 0)),
            pl.BlockSpec((1, D), lambda i, j: (0, 0)),
            pl.BlockSpec((D, tn), lambda i, j: (0, j)),
        ],
        out_specs=[
            pl.BlockSpec((tm, tn), cmap(0, nq)),
            pl.BlockSpec((tm, tn), cmap(e_q, nk)),
            pl.BlockSpec((tm, tn), cmap(e_k, nv)),
            pl.BlockSpec((tm, tn), cmap(e_v, nu)),
            pl.BlockSpec((tm, tn), cmap(e_u, ng)),
        ],
        out_shape=[
            jax.ShapeDtypeStruct((T, wq), F32),
            jax.ShapeDtypeStruct((T, wk), F32),
            jax.ShapeDtypeStruct((T, wv), BF16),
            jax.ShapeDtypeStruct((T, wu), BF16),
            jax.ShapeDtypeStruct((T, wg), BF16),
        ],
        scratch_shapes=[pltpu.VMEM((tm, D), BF16)],
        compiler_params=_cparams(("parallel", "arbitrary")),
        name="inproj",
    )(x, g, w)


def _qkrope_kernel(q_ref, k_ref, v_ref, qg_ref, kg_ref, c_ref, s_ref, qo_ref, kto_ref, vo_ref, *, hd, scale):
    c = c_ref[...]
    s = s_ref[...]
    lane = lax.broadcasted_iota(I32, (1, hd), 1)
    low_half = (lane % (hd // 2)) < (hd // 4)

    def one(x, g):
        y = _rms(x, g)
        sw = jnp.where(low_half, pltpu.roll(y, hd - hd // 4, 1), pltpu.roll(y, hd // 4, 1))
        return y * c + sw * s

    for h in range(q_ref.shape[1] // hd):
        sl = slice(h * hd, (h + 1) * hd)
        qo_ref[h] = (one(q_ref[:, sl], qg_ref[...]) * scale).astype(qo_ref.dtype)
    ones = jnp.ones((v_ref.shape[0], hd), vo_ref.dtype)
    for h in range(k_ref.shape[1] // hd):
        sl = slice(h * hd, (h + 1) * hd)
        kto_ref[h, 0] = one(k_ref[:, sl], kg_ref[...]).T.astype(kto_ref.dtype)
        vo_ref[h, :, :hd] = v_ref[:, sl]
        vo_ref[h, :, hd:] = ones


def _qkrope(q, k, v, qg, kg, cos_t, sin_t, S, ts):
    T, wq = q.shape
    wk = k.shape[1]
    hd = qg.shape[-1]
    n_q, n_kv = wq // hd, wk // hd
    ns = S // ts
    return pl.pallas_call(
        functools.partial(_qkrope_kernel, hd=hd, scale=hd ** -0.5 * math.log2(math.e)),
        grid=(T // ts,),
        in_specs=[
            pl.BlockSpec((ts, wq), lambda i: (i, 0)),
            pl.BlockSpec((ts, wk), lambda i: (i, 0)),
            pl.BlockSpec((ts, wk), lambda i: (i, 0)),
            pl.BlockSpec((1, hd), lambda i: (0, 0)),
            pl.BlockSpec((1, hd), lambda i: (0, 0)),
            pl.BlockSpec((ts, hd), lambda i: (i % ns, 0)),
            pl.BlockSpec((ts, hd), lambda i: (i % ns, 0)),
        ],
        out_specs=[
            pl.BlockSpec((n_q, ts, hd), lambda i: (0, i, 0)),
            pl.BlockSpec((n_kv, 1, hd, ts), lambda i: (0, i, 0, 0)),
            pl.BlockSpec((n_kv, ts, 2 * hd), lambda i: (0, i, 0)),
        ],
        out_shape=[
            jax.ShapeDtypeStruct((n_q, T, hd), BF16),
            jax.ShapeDtypeStruct((n_kv, T // ts, hd, ts), BF16),
            jax.ShapeDtypeStruct((n_kv, T, 2 * hd), BF16),
        ],
        compiler_params=_cparams(("parallel",)),
        name="qkrope",
    )(q, k, v, qg, kg, cos_t, sin_t)


def _rope_tables(S, hd):
    axis = hd // 2
    pos = jnp.arange(S, dtype=F32)
    row = jnp.floor(pos / ROPE_GRID_W)
    col = pos - row * ROPE_GRID_W
    inv_freq = ROPE_THETA ** (-jnp.arange(0, axis, 2, dtype=F32) / axis)
    ang_r = row[:, None] * inv_freq
    ang_c = col[:, None] * inv_freq
    cr, sr, cc, sc = jnp.cos(ang_r), jnp.sin(ang_r), jnp.cos(ang_c), jnp.sin(ang_c)
    cos_t = jnp.concatenate([cr, cr, cc, cc], axis=-1)
    sin_t = jnp.concatenate([-sr, sr, -sc, sc], axis=-1)
    return cos_t, sin_t


def _attn_finish(acc, o_ref, G, tq, hd):
    for h in range(G):
        a = acc[h * tq:(h + 1) * tq]
        o_ref[:, h * hd:(h + 1) * hd] = (a[:, :hd] / a[:, hd:hd + 1]).astype(o_ref.dtype)


def _attn_bounded_kernel(q_ref, kt_ref, v_ref, o_ref, acc_ref, p_ref):
    G, tq, hd = q_ref.shape
    n_chunks, _, tk = kt_ref.shape
    q = q_ref[...].reshape(G * tq, hd)
    acc_ref[...] = jnp.zeros_like(acc_ref)
    p_ref[...] = jnp.exp2(_dot(q, kt_ref[0])).astype(BF16)

    def body(c, carry):
        s = _dot(q, kt_ref[c])
        off = pl.multiple_of((c - 1) * tk, tk)
        acc_ref[...] += _dot(p_ref[...], v_ref[pl.ds(off, tk), :])
        p_ref[...] = jnp.exp2(s).astype(BF16)
        return carry

    unroll = max(u for u in range(1, 9) if (n_chunks - 1) % u == 0) if n_chunks > 1 else 1
    lax.fori_loop(1, n_chunks, body, 0, unroll=unroll)
    acc_ref[...] += _dot(p_ref[...], v_ref[pl.ds((n_chunks - 1) * tk, tk), :])
    _attn_finish(acc_ref[...], o_ref, G, tq, hd)


def _attn_online_kernel(q_ref, kt_ref, v_ref, o_ref, acc_ref, m_ref):
    G, tq, hd = q_ref.shape
    n_chunks, _, tk = kt_ref.shape
    q = q_ref[...].reshape(G * tq, hd)
    acc_ref[...] = jnp.zeros_like(acc_ref)
    m_ref[...] = jnp.full_like(m_ref, -jnp.inf)

    def body(c, carry):
        off = pl.multiple_of(c * tk, tk)
        s = _dot(q, kt_ref[c])
        m = m_ref[...]
        m_new = jnp.maximum(m, jnp.max(s, axis=1, keepdims=True))
        p = jnp.exp2(s - m_new).astype(BF16)
        acc_ref[...] = jnp.exp2(m - m_new) * acc_ref[...] + _dot(p, v_ref[pl.ds(off, tk), :])
        m_ref[...] = m_new
        return carry

    lax.fori_loop(0, n_chunks, body, 0)
    _attn_finish(acc_ref[...], o_ref, G, tq, hd)


def _attention(q, kt, v, B, S, bounded, tq=256):
    n_q, T, hd = q.shape
    n_kv, _, _, tk = kt.shape
    G = n_q // n_kv
    tq = min(tq, S)
    nq = S // tq
    nc = S // tk
    scratch = [pltpu.VMEM((G * tq, 2 * hd), F32)]
    scratch.append(pltpu.VMEM((G * tq, tk), BF16) if bounded else pltpu.VMEM((G * tq, 1), F32))
    return pl.pallas_call(
        _attn_bounded_kernel if bounded else _attn_online_kernel,
        grid=(B, n_kv, nq),
        in_specs=[
            pl.BlockSpec((G, tq, hd), lambda b, g, i: (g, b * nq + i, 0)),
            pl.BlockSpec((None, nc, hd, tk), lambda b, g, i: (g, b, 0, 0)),
            pl.BlockSpec((None, S, 2 * hd), lambda b, g, i: (g, b, 0)),
        ],
        out_specs=pl.BlockSpec((tq, G * hd), lambda b, g, i: (b * nq + i, g)),
        out_shape=jax.ShapeDtypeStruct((T, n_q * hd), BF16),
        scratch_shapes=scratch,
        compiler_params=_cparams(("parallel", "parallel", "parallel")),
        name="attention_bounded" if bounded else "attention_online",
    )(q, kt, v)


MAX_UNSHIFTED_LOG2_SCORE = 100.0


def _attention_any(q, kt, v, qg, kg, B, S):
    hd = qg.shape[-1]
    bound = 1.02 * math.sqrt(hd) * math.log2(math.e) * jnp.max(jnp.abs(qg)) * jnp.max(jnp.abs(kg))
    return lax.cond(
        bound <= MAX_UNSHIFTED_LOG2_SCORE,
        lambda: _attention(q, kt, v, B, S, True),
        lambda: _attention(q, kt, v, B, S, False),
    )


def _dft_split(S):
    lg = int(math.log2(S))
    assert 1 << lg == S
    n2 = 1 << ((lg + 1) // 2)
    return S // n2, n2


def _cos_sin(n, m, period):
    ang = (2.0 * math.pi / period) * ((jnp.arange(n, dtype=I32)[:, None] * jnp.arange(m, dtype=I32)[None, :]) % period).astype(F32)
    return jnp.cos(ang), jnp.sin(ang)


def _f0_kernel(u_ref, cs_ref, a_ref, b_ref, *, gw):
    cs = cs_ref[...]
    for g in range(u_ref.shape[1] // gw):
        sl = slice(g * gw, (g + 1) * gw)
        ab = _dot(u_ref[:, sl], cs)
        a_ref[:, sl] = ab[:, :gw]
        b_ref[:, sl] = ab[:, gw:]


def _fa_kernel(a_ref, b_ref, ca_ref, sa_ref, ct_ref, st_ref, tr_ref, ti_ref):
    ca = ca_ref[...]
    sa = sa_ref[...]
    reps = a_ref.shape[2] // LANES
    for j in range(a_ref.shape[1]):
        a = a_ref[:, j, :].astype(BF16)
        b = b_ref[:, j, :].astype(BF16)
        tr = _dot(ca, a) - _dot(sa, b)
        ti = -(_dot(sa, a) + _dot(ca, b))
        ct = jnp.tile(ct_ref[j], (1, reps))
        st = jnp.tile(st_ref[j], (1, reps))
        tr_ref[:, j, :] = tr * ct + ti * st
        ti_ref[:, j, :] = ti * ct - tr * st


def _fb_kernel(tr_ref, ti_ref, c1_ref, s1_ref, f_ref, *, scale):
    c1 = c1_ref[...]
    s1 = s1_ref[...]
    for kk in range(tr_ref.shape[0]):
        xr = tr_ref[kk].astype(BF16)
        xi = ti_ref[kk].astype(BF16)
        f_ref[:, kk, :] = (_dot(c1, xr) + _dot(s1, xi)) * scale


def _fourier(u, B, S, ts=512, r=8):
    T, W = u.shape
    gw = FOURIER_GROUP_W
    ts = min(ts, T)
    n1, n2 = _dft_split(S)

    cc, sc = _cos_sin(gw, gw, gw)
    cs = jnp.concatenate([cc, sc], axis=1).astype(BF16)
    a, b = pl.pallas_call(
        functools.partial(_f0_kernel, gw=gw),
        grid=(T // ts,),
        in_specs=[pl.BlockSpec((ts, W), lambda i: (i, 0)), pl.BlockSpec((gw, 2 * gw), lambda i: (0, 0))],
        out_specs=[pl.BlockSpec((ts, W), lambda i: (i, 0))] * 2,
        out_shape=[jax.ShapeDtypeStruct((T, W), F32)] * 2,
        compiler_params=_cparams(("parallel",)),
        name="fourier_channels",
    )(u, cs)

    a3 = a.reshape(B * n2, n1, W)
    b3 = b.reshape(B * n2, n1, W)
    ca, sa = _cos_sin(n2, n2, n2)
    ctw, stw = _cos_sin(n1, n2, S)
    ctw = jnp.broadcast_to(ctw[:, :, None], (n1, n2, LANES))
    stw = jnp.broadcast_to(stw[:, :, None], (n1, n2, LANES))
    tr, ti = pl.pallas_call(
        _fa_kernel,
        grid=(B, n1 // r),
        in_specs=[
            pl.BlockSpec((n2, r, W), lambda bb, j: (bb, j, 0)),
            pl.BlockSpec((n2, r, W), lambda bb, j: (bb, j, 0)),
            pl.BlockSpec((n2, n2), lambda bb, j: (0, 0)),
            pl.BlockSpec((n2, n2), lambda bb, j: (0, 0)),
            pl.BlockSpec((r, n2, LANES), lambda bb, j: (j, 0, 0)),
            pl.BlockSpec((r, n2, LANES), lambda bb, j: (j, 0, 0)),
        ],
        out_specs=[pl.BlockSpec((n2, r, W), lambda bb, j: (bb, j, 0))] * 2,
        out_shape=[jax.ShapeDtypeStruct((B * n2, n1, W), F32)] * 2,
        compiler_params=_cparams(("parallel", "parallel")),
        name="fourier_stage_a",
    )(a3, b3, ca.astype(BF16), sa.astype(BF16), ctw, stw)

    c1, s1 = _cos_sin(n1, n1, n1)
    f3 = pl.pallas_call(
        functools.partial(_fb_kernel, scale=1.0 / math.sqrt(S * gw)),
        grid=(B, n2 // r),
        in_specs=[
            pl.BlockSpec((r, n1, W), lambda bb, k: (bb * (n2 // r) + k, 0, 0)),
            pl.BlockSpec((r, n1, W), lambda bb, k: (bb * (n2 // r) + k, 0, 0)),
            pl.BlockSpec((n1, n1), lambda bb, k: (0, 0)),
            pl.BlockSpec((n1, n1), lambda bb, k: (0, 0)),
        ],
        out_specs=pl.BlockSpec((n1, r, W), lambda bb, k: (bb, k, 0)),
        out_shape=jax.ShapeDtypeStruct((B * n1, n2, W), F32),
        compiler_params=_cparams(("parallel", "parallel")),
        name="fourier_stage_b",
    )(tr, ti, c1.astype(BF16), s1.astype(BF16))
    return f3.reshape(T, W)


def _merge_kernel(at_ref, f_ref, ga_ref, gf_ref, wap_ref, wfp_ref, o_ref):
    a = _dot(at_ref[...], wap_ref[...])
    fo = _dot(f_ref[...].astype(BF16), wfp_ref[...])
    ga = jax.nn.sigmoid(ga_ref[...].astype(F32))
    gf = jax.nn.sigmoid(gf_ref[...].astype(F32))
    o_ref[...] = (ga * a + gf * fo).astype(o_ref.dtype)


def _merge(attn, f, gates, w_ap, w_fp, tm=512, tn=512):
    T, wa = attn.shape
    wf = f.shape[1]
    D = w_ap.shape[1]
    tm = min(tm, T)
    nj = D // tn
    return pl.pallas_call(
        _merge_kernel,
        grid=(T // tm, nj),
        in_specs=[
            pl.BlockSpec((tm, wa), lambda i, j: (i, 0)),
            pl.BlockSpec((tm, wf), lambda i, j: (i, 0)),
            pl.BlockSpec((tm, tn), lambda i, j: (i, j)),
            pl.BlockSpec((tm, tn), lambda i, j: (i, j + nj)),
            pl.BlockSpec((wa, tn), lambda i, j: (0, j)),
            pl.BlockSpec((wf, tn), lambda i, j: (0, j)),
        ],
        out_specs=pl.BlockSpec((tm, tn), lambda i, j: (i, j)),
        out_shape=jax.ShapeDtypeStruct((T, D), BF16),
        compiler_params=_cparams(("parallel", "parallel")),
        name="merge",
    )(attn, f, gates, gates, w_ap, w_fp)


def _outproj_kernel(m_ref, x_ref, w_ref, g_ref, wrt_ref, wr_ref, x1_ref, h_ref, afft_ref, aff_ref):
    x1 = x_ref[...] + _dot(m_ref[...], w_ref[...])
    x1_ref[...] = x1
    h32 = _rms(x1, g_ref[...])
    h_ref[...] = h32
    h = h32.astype(BF16)
    lt = _dot_nt(wrt_ref[...], h)
    et = jnp.exp(lt - jnp.max(lt, axis=0, keepdims=True))
    afft_ref[...] = et / jnp.sum(et, axis=0, keepdims=True)
    lg = _dot(h, wr_ref[...])
    eg = jnp.exp(lg - jnp.max(lg, axis=1, keepdims=True))
    aff_ref[...] = eg / jnp.sum(eg, axis=1, keepdims=True)


def _outproj(merged, x, w_out, g, w_router_t, w_router, tm=256):
    T, D = x.shape
    E = w_router_t.shape[0]
    tm = min(tm, T)
    return pl.pallas_call(
        _outproj_kernel,
        grid=(T // tm,),
        in_specs=[
            pl.BlockSpec((tm, D), lambda i: (i, 0)),
            pl.BlockSpec((tm, D), lambda i: (i, 0)),
            pl.BlockSpec((D, D), lambda i: (0, 0)),
            pl.BlockSpec((1, D), lambda i: (0, 0)),
            pl.BlockSpec((E, D), lambda i: (0, 0)),
            pl.BlockSpec((D, E), lambda i: (0, 0)),
        ],
        out_specs=[
            pl.BlockSpec((tm, D), lambda i: (i, 0)),
            pl.BlockSpec((tm, D), lambda i: (i, 0)),
            pl.BlockSpec((E, tm), lambda i: (0, i)),
            pl.BlockSpec((tm, E), lambda i: (i, 0)),
        ],
        out_shape=[
            jax.ShapeDtypeStruct((T, D), F32),
            jax.ShapeDtypeStruct((T, D), F32),
            jax.ShapeDtypeStruct((E, T), F32),
            jax.ShapeDtypeStruct((T, E), F32),
        ],
        compiler_params=_cparams(("parallel",)),
        name="outproj_router",
    )(merged, x, w_out, g, w_router_t, w_router)


def _route_kernel(aff_ref, q_ref, qt_ref, offs_ref, *, cap):
    E, nb, L = aff_ref.shape
    rows = E * nb
    aff = aff_ref[...]

    def count(mask):
        c = jnp.sum(mask.astype(F32), axis=2, keepdims=True)
        return jnp.sum(c, axis=1, keepdims=True)

    def search(i, prefix):
        cand = prefix | (jnp.int32(1) << (30 - i))
        ge = aff >= lax.bitcast_convert_type(cand, F32)
        return jnp.where(count(ge) >= cap, cand, prefix)

    thr = lax.bitcast_convert_type(lax.fori_loop(0, 31, search, jnp.zeros((E, 1, 1), I32)), F32)

    ri = lax.broadcasted_iota(I32, (L, L), 0)
    ci = lax.broadcasted_iota(I32, (L, L), 1)
    tri = (ri <= ci).astype(BF16)
    ones = jnp.ones((L, L), BF16)
    rr = lax.broadcasted_iota(I32, (rows, rows), 0)
    rc = lax.broadcasted_iota(I32, (rows, rows), 1)
    sh = nb.bit_length() - 1
    same_expert = lax.shift_right_logical(rr, sh) == lax.shift_right_logical(rc, sh)
    earlier = (same_expert & (rc < rr)).astype(BF16)

    def cumsum(mask):
        x = mask.astype(F32).reshape(rows, L).astype(BF16)
        within = _dot(x, tri)
        tot = _dot(x, ones)
        offs = _dot(earlier, tot.astype(BF16))
        return (within + offs).reshape(E, nb, L), offs

    gt = aff > thr
    eq = aff == thr
    need = cap - count(gt)
    sel = gt | (eq & (cumsum(eq)[0] <= need))
    rank, offs = cumsum(sel)
    q_ref[...] = jnp.where(sel, rank - 1.0, -1.0)
    offs_ref[...] = offs

    filler = jnp.full((L - E, L), -1.0, F32)
    for b in range(nb):
        tile = jnp.concatenate([q_ref[:, b, :], filler], axis=0)
        qt_ref[b * L:(b + 1) * L, :] = tile.T


def _route(aff_t, cap):
    E, T = aff_t.shape
    nb = T // LANES
    aff3 = aff_t.reshape(E, nb, LANES)
    return pl.pallas_call(
        functools.partial(_route_kernel, cap=cap),
        out_shape=[
            jax.ShapeDtypeStruct((E, nb, LANES), F32),
            jax.ShapeDtypeStruct((T, LANES), F32),
            jax.ShapeDtypeStruct((E * nb, LANES), F32),
        ],
        compiler_params=pltpu.CompilerParams(vmem_limit_bytes=VMEM_LIMIT),
        name="route",
    )(aff3)


MOE_TOKEN_BLOCK = 256
MOE_WINDOW = 64


def _compact_kernel(cnt_ref, q_ref, idx_ref, *, cap):
    e = pl.program_id(0)
    nbk, kb = q_ref.shape
    L = LANES
    base = e * (nbk + 1)
    slot_col = lax.broadcasted_iota(I32, (L, 1), 0).astype(F32)
    lane_tok = lax.broadcasted_iota(I32, (1, kb), 1).astype(F32)
    for c in range(cap // L):
        lo_slot, hi_slot = c * L, (c + 1) * L
        b_lo = lax.fori_loop(0, nbk, lambda b, n: n + jnp.where(cnt_ref[base + b + 1] <= lo_slot, 1, 0), 0)
        b_hi = lax.fori_loop(0, nbk, lambda b, n: n + jnp.where(cnt_ref[base + b] < hi_slot, 1, 0), 0)

        def body(b, acc, lo_slot=lo_slot):
            hit = q_ref[pl.ds(b, 1), :] == slot_col + float(lo_slot)
            tok = lane_tok + lax.convert_element_type(b * kb, F32)
            return acc + jnp.where(hit, tok, 0.0)

        acc = lax.fori_loop(b_lo, b_hi, body, jnp.zeros((L, kb), F32))
        folded = functools.reduce(jnp.add, [acc[:, i * L:(i + 1) * L] for i in range(kb // L)])
        idx_ref[:, lo_slot:hi_slot] = jnp.sum(folded.T, axis=0, keepdims=True).astype(I32)


def _compact(cnt, q, cap):
    E, nbk, kb = q.shape
    grid_spec = pltpu.PrefetchScalarGridSpec(
        num_scalar_prefetch=1,
        grid=(E,),
        in_specs=[pl.BlockSpec((None, nbk, kb), lambda e, cnt: (e, 0, 0))],
        out_specs=pl.BlockSpec((None, 1, cap), lambda e, cnt: (e, 0, 0)),
    )
    return pl.pallas_call(
        functools.partial(_compact_kernel, cap=cap),
        grid_spec=grid_spec,
        out_shape=jax.ShapeDtypeStruct((E, 1, cap), I32),
        compiler_params=_cparams(("parallel",)),
        name="moe_compact",
    )(cnt, q)


def _ffn_kernel(idx_ref, h_hbm, wg_ref, wu_ref, wd_ref, o_ref, xbuf, x_ref, acc_ref, sem, *, cap, n_f):
    e = pl.program_id(0)
    f = pl.program_id(1)
    per_step = cap // n_f

    def fetch_row(expert, r):
        tok = idx_ref[expert * cap + r]
        pltpu.make_async_copy(h_hbm.at[pl.ds(tok, 1), :], xbuf.at[pl.ds(r, 1), :], sem).start()

    @pl.when(f == 0)
    def _():
        @pl.when(e == 0)
        def _():
            lax.fori_loop(0, cap, lambda r, c: (fetch_row(0, r), c)[1], 0, unroll=8)

        pltpu.make_async_copy(h_hbm.at[pl.ds(0, cap), :], xbuf, sem).wait()
        x_ref[...] = xbuf[...].astype(BF16)
        acc_ref[...] = jnp.zeros_like(acc_ref)

    x = x_ref[...]
    hid = jax.nn.silu(_dot(x, wg_ref[...].astype(BF16))) * _dot(x, wu_ref[...].astype(BF16))
    acc_ref[...] += _dot(hid.astype(BF16), wd_ref[...].astype(BF16))

    last_e = e == pl.num_programs(0) - 1
    nxt = jnp.where(last_e, 0, e + 1)
    for r in range(per_step):
        fetch_row(nxt, f * per_step + r)

    @pl.when(f == n_f - 1)
    def _():
        o_ref[...] = acc_ref[...].astype(o_ref.dtype)

        @pl.when(last_e)
        def _():
            pltpu.make_async_copy(h_hbm.at[pl.ds(0, cap), :], xbuf, sem).wait()


def _ffn(idx, h, wg, wu, wd, cap, fn=256):
    E, D, Fw = wg.shape
    fn = min(fn, Fw)
    grid_spec = pltpu.PrefetchScalarGridSpec(
        num_scalar_prefetch=1,
        grid=(E, Fw // fn),
        in_specs=[
            pl.BlockSpec(memory_space=pl.ANY),
            pl.BlockSpec((None, D, fn), lambda e, f, idx: (e, 0, f)),
            pl.BlockSpec((None, D, fn), lambda e, f, idx: (e, 0, f)),
            pl.BlockSpec((None, fn, D), lambda e, f, idx: (e, f, 0)),
        ],
        out_specs=pl.BlockSpec((cap, D), lambda e, f, idx: (e, 0)),
        scratch_shapes=[
            pltpu.VMEM((cap, D), F32),
            pltpu.VMEM((cap, D), BF16),
            pltpu.VMEM((cap, D), F32),
            pltpu.SemaphoreType.DMA(()),
        ],
    )
    return pl.pallas_call(
        functools.partial(_ffn_kernel, cap=cap, n_f=Fw // fn),
        grid_spec=grid_spec,
        out_shape=jax.ShapeDtypeStruct((E * cap, D), BF16),
        compiler_params=_cparams(("arbitrary", "arbitrary")),
        name="moe_ffn",
    )(idx, h, wg, wu, wd)


def _combine_kernel(cnt_ref, x_ref, qt_ref, aff_ref, o_hbm, g_ref, y_ref, obuf, acc_ref, sem, *, cap, nbk):
    b = pl.program_id(0)
    E = aff_ref.shape[1]
    W = obuf.shape[0] // E
    lo = [cnt_ref[e * (nbk + 1) + b] for e in range(E)]
    hi = [cnt_ref[e * (nbk + 1) + b + 1] for e in range(E)]
    lo = [lax.shift_left(lax.shift_right_logical(v, 4), 4) for v in lo]
    span = functools.reduce(jnp.maximum, [h - l for h, l in zip(hi, lo)])
    n_pass = lax.shift_right_logical(span + (W - 1), W.bit_length() - 1)
    acc_ref[...] = x_ref[...]
    lane = lax.broadcasted_iota(I32, (1, 2 * W), 1)
    first = lane < W
    j = jnp.where(first, lane, lane - W).astype(F32)

    def one_pass(p, carry):
        nominal = [l + p * W for l in lo]
        start = [jnp.minimum(n, cap - W) for n in nominal]
        copies = [
            pltpu.make_async_copy(
                o_hbm.at[pl.ds(pl.multiple_of(e * cap + start[e], 16), W), :],
                obuf.at[pl.ds(e * W, W), :],
                sem,
            )
            for e in range(E)
        ]
        for cp in copies:
            cp.start()
        seg_hi, seg_lo = [], []
        for e in range(0, E, 2):
            slot = jnp.where(first, qt_ref[:, e:e + 1], qt_ref[:, e + 1:e + 2])
            gate = jnp.where(first, aff_ref[:, e:e + 1], aff_ref[:, e + 1:e + 2])
            row0 = jnp.where(first, start[e].astype(F32), start[e + 1].astype(F32))
            own0 = jnp.where(first, nominal[e].astype(F32), nominal[e + 1].astype(F32))
            hit = (slot == row0 + j) & (slot >= own0) & (slot < own0 + float(W))
            val = jnp.where(hit, gate, 0.0)
            vhi = val.astype(BF16)
            seg_hi.append(vhi)
            seg_lo.append((val - vhi.astype(F32)).astype(BF16))
        for cp in copies:
            cp.wait()
        rows = obuf[...]
        acc_ref[...] += _dot(jnp.concatenate(seg_hi, axis=1), rows) + _dot(jnp.concatenate(seg_lo, axis=1), rows)
        return carry

    lax.fori_loop(0, n_pass, one_pass, 0)
    y_ref[...] = _rms(acc_ref[...], g_ref[...])


def _combine(cnt, x1, qt, aff, o, g, cap):
    T, D = x1.shape
    E = aff.shape[1]
    tb = min(MOE_TOKEN_BLOCK, T)
    W = min(MOE_WINDOW, cap)
    grid_spec = pltpu.PrefetchScalarGridSpec(
        num_scalar_prefetch=1,
        grid=(T // tb,),
        in_specs=[
            pl.BlockSpec((tb, D), lambda b, cnt: (b, 0)),
            pl.BlockSpec((tb, LANES), lambda b, cnt: (b, 0)),
            pl.BlockSpec((tb, E), lambda b, cnt: (b, 0)),
            pl.BlockSpec(memory_space=pl.ANY),
            pl.BlockSpec((1, D), lambda b, cnt: (0, 0)),
        ],
        out_specs=pl.BlockSpec((tb, D), lambda b, cnt: (b, 0)),
        scratch_shapes=[
            pltpu.VMEM((E * W, D), BF16),
            pltpu.VMEM((tb, D), F32),
            pltpu.SemaphoreType.DMA(()),
        ],
    )
    return pl.pallas_call(
        functools.partial(_combine_kernel, cap=cap, nbk=T // tb),
        grid_spec=grid_spec,
        out_shape=jax.ShapeDtypeStruct((T, D), F32),
        compiler_params=_cparams(("parallel",)),
        name="moe_combine_norm",
    )(cnt, x1, qt, aff, o, g)


def _expert_choice_moe(x1, h2, aff_t, aff, p):
    T, D = x1.shape
    E = aff_t.shape[0]
    cap = CAPACITY_FACTOR * T // E
    q, qt, offs = _route(aff_t, cap)
    nb = T // LANES
    per_blk = min(MOE_TOKEN_BLOCK, T) // LANES
    cnt = offs[:, 0].reshape(E, nb)[:, ::per_blk]
    cnt = jnp.concatenate([cnt, jnp.full((E, 1), cap, F32)], axis=1).astype(I32).reshape(-1)
    idx = _compact(cnt, q.reshape(E, nb // per_blk, per_blk * LANES), cap)
    o = _ffn(idx.reshape(E * cap), h2, p["w_expert_gate"], p["w_expert_up"], p["w_expert_down"], cap)
    return _combine(cnt, x1, qt, aff, o, p["norm_final_g"], cap)


def _encoder_group(x, p):
    B, S, D = x.shape
    T = B * S
    hd = p["q_norm_g"].shape[-1]
    wq, wk = N_Q_HEADS * hd, N_KV_HEADS * hd
    wu = N_FOURIER_GROUPS * FOURIER_GROUP_W
    xt = x.reshape(T, D)

    q, k, v, u, gates = _inproj(xt, p["norm_mix_g"], p["w_in"], (wq, wk, wk, wu, 2 * D))
    cos_t, sin_t = _rope_tables(S, hd)
    qr, kt, va = _qkrope(q, k, v, p["q_norm_g"], p["k_norm_g"], cos_t, sin_t, S, ts=min(ATTN_KV_CHUNK, S))
    attn = _attention_any(qr, kt, va, p["q_norm_g"], p["k_norm_g"], B, S)
    f = _fourier(u, B, S)
    merged = _merge(attn, f, gates, p["w_attn_proj"], p["w_fourier_proj"])
    x1, h2, aff_t, aff = _outproj(merged, xt, p["w_out"], p["norm_moe_g"], p["w_router_t"], p["w_router"])
    return _expert_choice_moe(x1, h2, aff_t, aff, p).reshape(B, S, D)


def kernel(x_prompt, x_sample, norm_mix_g, w_in, q_norm_g, k_norm_g, w_attn_proj, w_fourier_proj, w_out, norm_moe_g, w_router, w_expert_gate, w_expert_up, w_expert_down, norm_final_g):
    p = {
        "norm_mix_g": norm_mix_g[0][None, :],
        "w_in": w_in[0],
        "q_norm_g": q_norm_g[0][None, :],
        "k_norm_g": k_norm_g[0][None, :],
        "w_attn_proj": w_attn_proj[0].astype(BF16),
        "w_fourier_proj": w_fourier_proj[0].astype(BF16),
        "w_out": w_out[0].astype(BF16),
        "norm_moe_g": norm_moe_g[0][None, :],
        "w_router_t": w_router[0].T.astype(BF16),
        "w_router": w_router[0].astype(BF16),
        "w_expert_gate": w_expert_gate[0],
        "w_expert_up": w_expert_up[0],
        "w_expert_down": w_expert_down[0],
        "norm_final_g": norm_final_g[None, :],
    }
    return (_encoder_group(x_prompt, p), _encoder_group(x_sample, p))
```

```python
import functools
import math

import jax
import jax.numpy as jnp
from jax import lax
from jax.experimental import pallas as pl
from jax.experimental.pallas import tpu as pltpu

F32 = jnp.float32
BF16 = jnp.bfloat16
I32 = jnp.int32

EPS = 1e-6
N_Q_HEADS = 16
N_KV_HEADS = 4
ROPE_GRID_W = 64
ROPE_THETA = 10000.0
N_FOURIER_GROUPS = 4
FOURIER_GROUP_W = 256
CAPACITY_FACTOR = 2
LANES = 128
ATTN_KV_CHUNK = 512
VMEM_LIMIT = 56 * 1024 * 1024


def _cparams(sem):
    return pltpu.CompilerParams(dimension_semantics=sem, vmem_limit_bytes=VMEM_LIMIT)


def _dot(a, b):
    return jnp.dot(a, b, preferred_element_type=F32)


def _dot_nt(a, b):
    return lax.dot_general(a, b, (((1,), (1,)), ((), ())), preferred_element_type=F32)


def _rms(x, g):
    ms = jnp.mean(x * x, axis=-1, keepdims=True)
    return x * lax.rsqrt(ms + EPS) * g


def _inproj_kernel(x_ref, g_ref, w_ref, q_ref, k_ref, v_ref, u_ref, gt_ref, h_ref, *, edges):
    j = pl.program_id(1)

    @pl.when(j == 0)
    def _():
        h_ref[...] = _rms(x_ref[...], g_ref[...]).astype(BF16)

    acc = _dot(h_ref[...], w_ref[...].astype(BF16))
    e_q, e_k, e_v, e_u = edges

    @pl.when(j < e_q)
    def _():
        q_ref[...] = acc

    @pl.when((j >= e_q) & (j < e_k))
    def _():
        k_ref[...] = acc

    @pl.when((j >= e_k) & (j < e_v))
    def _():
        v_ref[...] = acc.astype(v_ref.dtype)

    @pl.when((j >= e_v) & (j < e_u))
    def _():
        u_ref[...] = acc.astype(u_ref.dtype)

    @pl.when(j >= e_u)
    def _():
        gt_ref[...] = acc.astype(gt_ref.dtype)


def _inproj(x, g, w, widths, tm=1024, tn=512):
    T, D = x.shape
    tm = min(tm, T)
    wq, wk, wv, wu, wg = widths
    nq, nk, nv, nu, ng = (w_ // tn for w_ in widths)
    assert all(w_ % tn == 0 for w_ in widths)
    e_q, e_k, e_v, e_u = nq, nq + nk, nq + nk + nv, nq + nk + nv + nu
    nj = e_u + ng

    def cmap(lo, n):
        return lambda i, j: (i, jnp.clip(j - lo, 0, n - 1))

    return pl.pallas_call(
        functools.partial(_inproj_kernel, edges=(e_q, e_k, e_v, e_u)),
        grid=(T // tm, nj),
        in_specs=[
            pl.BlockSpec((tm, D), lambda i, j: (i, 0)),
            pl.BlockSpec((1, D), lambda i, j: (0, 0)),
            pl.BlockSpec((D, tn), lambda i, j: (0, j)),
        ],
        out_specs=[
            pl.BlockSpec((tm, tn), cmap(0, nq)),
            pl.BlockSpec((tm, tn), cmap(e_q, nk)),
            pl.BlockSpec((tm, tn), cmap(e_k, nv)),
            pl.BlockSpec((tm, tn), cmap(e_v, nu)),
            pl.BlockSpec((tm, tn), cmap(e_u, ng)),
        ],
        out_shape=[
            jax.ShapeDtypeStruct((T, wq), F32),
            jax.ShapeDtypeStruct((T, wk), F32),
            jax.ShapeDtypeStruct((T, wv), BF16),
            jax.ShapeDtypeStruct((T, wu), BF16),
            jax.ShapeDtypeStruct((T, wg), BF16),
        ],
        scratch_shapes=[pltpu.VMEM((tm, D), BF16)],
        compiler_params=_cparams(("parallel", "arbitrary")),
        name="inproj",
    )(x, g, w)


def _qkrope_kernel(q_ref, k_ref, v_ref, qg_ref, kg_ref, c_ref, s_ref, qo_ref, kto_ref, vo_ref, *, hd, scale):
    c = c_ref[...]
    s = s_ref[...]
    lane = lax.broadcasted_iota(I32, (1, hd), 1)
    low_half = (lane % (hd // 2)) < (hd // 4)

    def one(x, g):
        y = _rms(x, g)
        sw = jnp.where(low_half, pltpu.roll(y, hd - hd // 4, 1), pltpu.roll(y, hd // 4, 1))
        return y * c + sw * s

    for h in range(q_ref.shape[1] // hd):
        sl = slice(h * hd, (h + 1) * hd)
        qo_ref[h] = (one(q_ref[:, sl], qg_ref[...]) * scale).astype(qo_ref.dtype)
    ones = jnp.ones((v_ref.shape[0], hd), vo_ref.dtype)
    for h in range(k_ref.shape[1] // hd):
        sl = slice(h * hd, (h + 1) * hd)
        kto_ref[h, 0] = one(k_ref[:, sl], kg_ref[...]).T.astype(kto_ref.dtype)
        vo_ref[h, :, :hd] = v_ref[:, sl]
        vo_ref[h, :, hd:] = ones


def _qkrope(q, k, v, qg, kg, cos_t, sin_t, S, ts):
    T, wq = q.shape
    wk = k.shape[1]
    hd = qg.shape[-1]
    n_q, n_kv = wq // hd, wk // hd
    ns = S // ts
    return pl.pallas_call(
        functools.partial(_qkrope_kernel, hd=hd, scale=hd ** -0.5 * math.log2(math.e)),
        grid=(T // ts,),
        in_specs=[
            pl.BlockSpec((ts, wq), lambda i: (i, 0)),
            pl.BlockSpec((ts, wk), lambda i: (i, 0)),
            pl.BlockSpec((ts, wk), lambda i: (i, 0)),
            pl.BlockSpec((1, hd), lambda i: (0, 0)),
            pl.BlockSpec((1, hd), lambda i: (0, 0)),
            pl.BlockSpec((ts, hd), lambda i: (i % ns, 0)),
            pl.BlockSpec((ts, hd), lambda i: (i % ns, 0)),
        ],
        out_specs=[
            pl.BlockSpec((n_q, ts, hd), lambda i: (0, i, 0)),
            pl.BlockSpec((n_kv, 1, hd, ts), lambda i: (0, i, 0, 0)),
            pl.BlockSpec((n_kv, ts, 2 * hd), lambda i: (0, i, 0)),
        ],
        out_shape=[
            jax.ShapeDtypeStruct((n_q, T, hd), BF16),
            jax.ShapeDtypeStruct((n_kv, T // ts, hd, ts), BF16),
            jax.ShapeDtypeStruct((n_kv, T, 2 * hd), BF16),
        ],
        compiler_params=_cparams(("parallel",)),
        name="qkrope",
    )(q, k, v, qg, kg, cos_t, sin_t)


def _rope_tables(S, hd):
    axis = hd // 2
    pos = jnp.arange(S, dtype=F32)
    row = jnp.floor(pos / ROPE_GRID_W)
    col = pos - row * ROPE_GRID_W
    inv_freq = ROPE_THETA ** (-jnp.arange(0, axis, 2, dtype=F32) / axis)
    ang_r = row[:, None] * inv_freq
    ang_c = col[:, None] * inv_freq
    cr, sr, cc, sc = jnp.cos(ang_r), jnp.sin(ang_r), jnp.cos(ang_c), jnp.sin(ang_c)
    cos_t = jnp.concatenate([cr, cr, cc, cc], axis=-1)
    sin_t = jnp.concatenate([-sr, sr, -sc, sc], axis=-1)
    return cos_t, sin_t


def _attn_finish(acc, o_ref, G, tq, hd):
    for h in range(G):
        a = acc[h * tq:(h + 1) * tq]
        o_ref[:, h * hd:(h + 1) * hd] = (a[:, :hd] / a[:, hd:hd + 1]).astype(o_ref.dtype)


def _attn_bounded_kernel(q_ref, kt_ref, v_ref, o_ref, acc_ref, p_ref):
    G, tq, hd = q_ref.shape
    n_chunks, _, tk = kt_ref.shape
    q = q_ref[...].reshape(G * tq, hd)
    acc_ref[...] = jnp.zeros_like(acc_ref)
    p_ref[...] = jnp.exp2(_dot(q, kt_ref[0])).astype(BF16)

    def body(c, carry):
        s = _dot(q, kt_ref[c])
        off = pl.multiple_of((c - 1) * tk, tk)
        acc_ref[...] += _dot(p_ref[...], v_ref[pl.ds(off, tk), :])
        p_ref[...] = jnp.exp2(s).astype(BF16)
        return carry

    unroll = max(u for u in range(1, 9) if (n_chunks - 1) % u == 0) if n_chunks > 1 else 1
    lax.fori_loop(1, n_chunks, body, 0, unroll=unroll)
    acc_ref[...] += _dot(p_ref[...], v_ref[pl.ds((n_chunks - 1) * tk, tk), :])
    _attn_finish(acc_ref[...], o_ref, G, tq, hd)


def _attn_online_kernel(q_ref, kt_ref, v_ref, o_ref, acc_ref, m_ref):
    G, tq, hd = q_ref.shape
    n_chunks, _, tk = kt_ref.shape
    q = q_ref[...].reshape(G * tq, hd)
    acc_ref[...] = jnp.zeros_like(acc_ref)
    m_ref[...] = jnp.full_like(m_ref, -jnp.inf)

    def body(c, carry):
        off = pl.multiple_of(c * tk, tk)
        s = _dot(q, kt_ref[c])
        m = m_ref[...]
        m_new = jnp.maximum(m, jnp.max(s, axis=1, keepdims=True))
        p = jnp.exp2(s - m_new).astype(BF16)
        acc_ref[...] = jnp.exp2(m - m_new) * acc_ref[...] + _dot(p, v_ref[pl.ds(off, tk), :])
        m_ref[...] = m_new
        return carry

    lax.fori_loop(0, n_chunks, body, 0)
    _attn_finish(acc_ref[...], o_ref, G, tq, hd)


def _attention(q, kt, v, B, S, bounded, tq=256):
    n_q, T, hd = q.shape
    n_kv, _, _, tk = kt.shape
    G = n_q // n_kv
    tq = min(tq, S)
    nq = S // tq
    nc = S // tk
    scratch = [pltpu.VMEM((G * tq, 2 * hd), F32)]
    scratch.append(pltpu.VMEM((G * tq, tk), BF16) if bounded else pltpu.VMEM((G * tq, 1), F32))
    return pl.pallas_call(
        _attn_bounded_kernel if bounded else _attn_online_kernel,
        grid=(B, n_kv, nq),
        in_specs=[
            pl.BlockSpec((G, tq, hd), lambda b, g, i: (g, b * nq + i, 0)),
            pl.BlockSpec((None, nc, hd, tk), lambda b, g, i: (g, b, 0, 0)),
            pl.BlockSpec((None, S, 2 * hd), lambda b, g, i: (g, b, 0)),
        ],
        out_specs=pl.BlockSpec((tq, G * hd), lambda b, g, i: (b * nq + i, g)),
        out_shape=jax.ShapeDtypeStruct((T, n_q * hd), BF16),
        scratch_shapes=scratch,
        compiler_params=_cparams(("parallel", "parallel", "parallel")),
        name="attention_bounded" if bounded else "attention_online",
    )(q, kt, v)


MAX_UNSHIFTED_LOG2_SCORE = 100.0


def _attention_any(q, kt, v, qg, kg, B, S):
    hd = qg.shape[-1]
    bound = 1.02 * math.sqrt(hd) * math.log2(math.e) * jnp.max(jnp.abs(qg)) * jnp.max(jnp.abs(kg))
    return lax.cond(
        bound <= MAX_UNSHIFTED_LOG2_SCORE,
        lambda: _attention(q, kt, v, B, S, True),
        lambda: _attention(q, kt, v, B, S, False),
    )


def _dft_split(S):
    lg = int(math.log2(S))
    assert 1 << lg == S
    n2 = 1 << ((lg + 1) // 2)
    return S // n2, n2


def _cos_sin(n, m, period):
    ang = (2.0 * math.pi / period) * ((jnp.arange(n, dtype=I32)[:, None] * jnp.arange(m, dtype=I32)[None, :]) % period).astype(F32)
    return jnp.cos(ang), jnp.sin(ang)


def _f0_kernel(u_ref, cs_ref, a_ref, b_ref, *, gw):
    cs = cs_ref[...]
    for g in range(u_ref.shape[1] // gw):
        sl = slice(g * gw, (g + 1) * gw)
        ab = _dot(u_ref[:, sl], cs)
        a_ref[:, sl] = ab[:, :gw]
        b_ref[:, sl] = ab[:, gw:]


def _fa_kernel(a_hbm, b_hbm, ca_ref, sa_ref, ct_ref, st_ref, tr_hbm, ti_hbm, abuf, bbuf, trbuf, tibuf, sems):
    r, n2, W = abuf.shape
    bb = pl.program_id(0)
    j0 = pl.program_id(1) * r
    rows = pl.ds(bb * n2, n2)

    def loads(jj):
        return (
            pltpu.make_async_copy(a_hbm.at[rows, j0 + jj, :], abuf.at[jj], sems.at[0, jj]),
            pltpu.make_async_copy(b_hbm.at[rows, j0 + jj, :], bbuf.at[jj], sems.at[1, jj]),
        )

    def stores(jj):
        return (
            pltpu.make_async_copy(trbuf.at[jj], tr_hbm.at[rows, j0 + jj, :], sems.at[2, jj]),
            pltpu.make_async_copy(tibuf.at[jj], ti_hbm.at[rows, j0 + jj, :], sems.at[3, jj]),
        )

    for jj in range(r):
        for cp in loads(jj):
            cp.start()
    ca = ca_ref[...]
    sa = sa_ref[...]
    reps = W // LANES
    for jj in range(r):
        for cp in loads(jj):
            cp.wait()
        a = abuf[jj].astype(BF16)
        b = bbuf[jj].astype(BF16)
        tr = _dot(ca, a) - _dot(sa, b)
        ti = -(_dot(sa, a) + _dot(ca, b))
        ct = jnp.tile(ct_ref[jj], (1, reps))
        st = jnp.tile(st_ref[jj], (1, reps))
        trbuf[jj] = tr * ct + ti * st
        tibuf[jj] = ti * ct - tr * st
        for cp in stores(jj):
            cp.start()
    for jj in range(r):
        for cp in stores(jj):
            cp.wait()


def _fb_kernel(tr_ref, ti_ref, c1_ref, s1_ref, f_ref, *, scale):
    c1 = c1_ref[...]
    s1 = s1_ref[...]
    for kk in range(tr_ref.shape[0]):
        xr = tr_ref[kk].astype(BF16)
        xi = ti_ref[kk].astype(BF16)
        f_ref[:, kk, :] = (_dot(c1, xr) + _dot(s1, xi)) * scale


def _fourier(u, B, S, ts=512, r=8):
    T, W = u.shape
    gw = FOURIER_GROUP_W
    ts = min(ts, T)
    n1, n2 = _dft_split(S)

    cc, sc = _cos_sin(gw, gw, gw)
    cs = jnp.concatenate([cc, sc], axis=1).astype(BF16)
    a, b = pl.pallas_call(
        functools.partial(_f0_kernel, gw=gw),
        grid=(T // ts,),
        in_specs=[pl.BlockSpec((ts, W), lambda i: (i, 0)), pl.BlockSpec((gw, 2 * gw), lambda i: (0, 0))],
        out_specs=[pl.BlockSpec((ts, W), lambda i: (i, 0))] * 2,
        out_shape=[jax.ShapeDtypeStruct((T, W), F32)] * 2,
        compiler_params=_cparams(("parallel",)),
        name="fourier_channels",
    )(u, cs)

    a3 = a.reshape(B * n2, n1, W)
    b3 = b.reshape(B * n2, n1, W)
    ca, sa = _cos_sin(n2, n2, n2)
    ctw, stw = _cos_sin(n1, n2, S)
    ctw = jnp.broadcast_to(ctw[:, :, None], (n1, n2, LANES))
    stw = jnp.broadcast_to(stw[:, :, None], (n1, n2, LANES))
    tr, ti = pl.pallas_call(
        _fa_kernel,
        grid=(B, n1 // r),
        in_specs=[
            pl.BlockSpec(memory_space=pl.ANY),
            pl.BlockSpec(memory_space=pl.ANY),
            pl.BlockSpec((n2, n2), lambda bb, j: (0, 0)),
            pl.BlockSpec((n2, n2), lambda bb, j: (0, 0)),
            pl.BlockSpec((r, n2, LANES), lambda bb, j: (j, 0, 0)),
            pl.BlockSpec((r, n2, LANES), lambda bb, j: (j, 0, 0)),
        ],
        out_specs=[pl.BlockSpec(memory_space=pl.ANY)] * 2,
        out_shape=[jax.ShapeDtypeStruct((B * n2, n1, W), F32)] * 2,
        scratch_shapes=[pltpu.VMEM((r, n2, W), F32)] * 4 + [pltpu.SemaphoreType.DMA((4, r))],
        compiler_params=_cparams(("parallel", "parallel")),
        name="fourier_stage_a",
    )(a3, b3, ca.astype(BF16), sa.astype(BF16), ctw, stw)

    c1, s1 = _cos_sin(n1, n1, n1)
    f3 = pl.pallas_call(
        functools.partial(_fb_kernel, scale=1.0 / math.sqrt(S * gw)),
        grid=(B, n2 // r),
        in_specs=[
            pl.BlockSpec((r, n1, W), lambda bb, k: (bb * (n2 // r) + k, 0, 0)),
            pl.BlockSpec((r, n1, W), lambda bb, k: (bb * (n2 // r) + k, 0, 0)),
            pl.BlockSpec((n1, n1), lambda bb, k: (0, 0)),
            pl.BlockSpec((n1, n1), lambda bb, k: (0, 0)),
        ],
        out_specs=pl.BlockSpec((n1, r, W), lambda bb, k: (bb, k, 0)),
        out_shape=jax.ShapeDtypeStruct((B * n1, n2, W), F32),
        compiler_params=_cparams(("parallel", "parallel")),
        name="fourier_stage_b",
    )(tr, ti, c1.astype(BF16), s1.astype(BF16))
    return f3.reshape(T, W)


def _merge_kernel(at_ref, f_ref, ga_ref, gf_ref, wap_ref, wfp_ref, o_ref):
    a = _dot(at_ref[...], wap_ref[...])
    fo = _dot(f_ref[...].astype(BF16), wfp_ref[...])
    ga = jax.nn.sigmoid(ga_ref[...].astype(F32))
    gf = jax.nn.sigmoid(gf_ref[...].astype(F32))
    o_ref[...] = (ga * a + gf * fo).astype(o_ref.dtype)


def _merge(attn, f, gates, w_ap, w_fp, tm=1024, tn=1024):
    T, wa = attn.shape
    wf = f.shape[1]
    D = w_ap.shape[1]
    tm = min(tm, T)
    nj = D // tn
    return pl.pallas_call(
        _merge_kernel,
        grid=(T // tm, nj),
        in_specs=[
            pl.BlockSpec((tm, wa), lambda i, j: (i, 0)),
            pl.BlockSpec((tm, wf), lambda i, j: (i, 0)),
            pl.BlockSpec((tm, tn), lambda i, j: (i, j)),
            pl.BlockSpec((tm, tn), lambda i, j: (i, j + nj)),
            pl.BlockSpec((wa, tn), lambda i, j: (0, j)),
            pl.BlockSpec((wf, tn), lambda i, j: (0, j)),
        ],
        out_specs=pl.BlockSpec((tm, tn), lambda i, j: (i, j)),
        out_shape=jax.ShapeDtypeStruct((T, D), BF16),
        compiler_params=_cparams(("parallel", "parallel")),
        name="merge",
    )(attn, f, gates, gates, w_ap, w_fp)


def _outproj_kernel(m_ref, x_ref, w_ref, g_ref, wrt_ref, x1_ref, h_ref, afft_ref):
    x1 = x_ref[...] + _dot(m_ref[...], w_ref[...])
    x1_ref[...] = x1
    h32 = _rms(x1, g_ref[...])
    h_ref[...] = h32
    lt = _dot_nt(wrt_ref[...], h32.astype(BF16))
    et = jnp.exp(lt - jnp.max(lt, axis=0, keepdims=True))
    afft_ref[...] = et / jnp.sum(et, axis=0, keepdims=True)


def _outproj(merged, x, w_out, g, w_router_t, tm=256):
    T, D = x.shape
    E = w_router_t.shape[0]
    tm = min(tm, T)
    return pl.pallas_call(
        _outproj_kernel,
        grid=(T // tm,),
        in_specs=[
            pl.BlockSpec((tm, D), lambda i: (i, 0)),
            pl.BlockSpec((tm, D), lambda i: (i, 0)),
            pl.BlockSpec((D, D), lambda i: (0, 0)),
            pl.BlockSpec((1, D), lambda i: (0, 0)),
            pl.BlockSpec((E, D), lambda i: (0, 0)),
        ],
        out_specs=[
            pl.BlockSpec((tm, D), lambda i: (i, 0)),
            pl.BlockSpec((tm, D), lambda i: (i, 0)),
            pl.BlockSpec((E, tm), lambda i: (0, i)),
        ],
        out_shape=[
            jax.ShapeDtypeStruct((T, D), F32),
            jax.ShapeDtypeStruct((T, D), F32),
            jax.ShapeDtypeStruct((E, T), F32),
        ],
        compiler_params=_cparams(("parallel",)),
        name="outproj_router",
    )(merged, x, w_out, g, w_router_t)


def _route_kernel(aff_ref, q_ref, qt_ref, offs_ref, *, cap):
    E, nb, L = aff_ref.shape
    rows = E * nb
    aff = aff_ref[...]

    def count(mask):
        c = jnp.sum(mask.astype(F32), axis=2, keepdims=True)
        return jnp.sum(c, axis=1, keepdims=True)

    def search(i, prefix):
        cand = prefix | (jnp.int32(1) << (30 - i))
        ge = aff >= lax.bitcast_convert_type(cand, F32)
        return jnp.where(count(ge) >= cap, cand, prefix)

    thr = lax.bitcast_convert_type(lax.fori_loop(0, 31, search, jnp.zeros((E, 1, 1), I32)), F32)

    ri = lax.broadcasted_iota(I32, (L, L), 0)
    ci = lax.broadcasted_iota(I32, (L, L), 1)
    tri = (ri <= ci).astype(BF16)
    ones = jnp.ones((L, L), BF16)
    rr = lax.broadcasted_iota(I32, (rows, rows), 0)
    rc = lax.broadcasted_iota(I32, (rows, rows), 1)
    sh = nb.bit_length() - 1
    same_expert = lax.shift_right_logical(rr, sh) == lax.shift_right_logical(rc, sh)
    earlier = (same_expert & (rc < rr)).astype(BF16)

    def cumsum(mask):
        x = mask.astype(F32).reshape(rows, L).astype(BF16)
        within = _dot(x, tri)
        tot = _dot(x, ones)
        offs = _dot(earlier, tot.astype(BF16))
        return (within + offs).reshape(E, nb, L), offs

    gt = aff > thr
    eq = aff == thr
    need = cap - count(gt)
    sel = gt | (eq & (cumsum(eq)[0] <= need))
    rank, offs = cumsum(sel)
    q_ref[...] = jnp.where(sel, rank - 1.0, -1.0)
    offs_ref[...] = offs

    filler = jnp.full((L - E, L), -1.0, F32)
    for b in range(nb):
        tile = jnp.concatenate([q_ref[:, b, :], filler], axis=0)
        qt_ref[b * L:(b + 1) * L, :] = tile.T


def _route(aff_t, cap):
    E, T = aff_t.shape
    nb = T // LANES
    aff3 = aff_t.reshape(E, nb, LANES)
    return pl.pallas_call(
        functools.partial(_route_kernel, cap=cap),
        out_shape=[
            jax.ShapeDtypeStruct((E, nb, LANES), F32),
            jax.ShapeDtypeStruct((T, LANES), F32),
            jax.ShapeDtypeStruct((E * nb, LANES), F32),
        ],
        compiler_params=pltpu.CompilerParams(vmem_limit_bytes=VMEM_LIMIT),
        name="route",
    )(aff3)


MOE_TOKEN_BLOCK = 256
MOE_WINDOW = 64


def _compact_kernel(cnt_ref, q_ref, aff_ref, idx_ref, gate_ref, *, cap):
    e = pl.program_id(0)
    nbk, kb = q_ref.shape
    L = LANES
    base = e * (nbk + 1)
    slot_col = lax.broadcasted_iota(I32, (L, 1), 0).astype(F32)
    lane_tok = lax.broadcasted_iota(I32, (1, kb), 1).astype(F32)
    b_lo = b_hi = jnp.int32(0)
    for c in range(cap // L):
        lo_slot, hi_slot = c * L, (c + 1) * L
        b_lo = lax.while_loop(lambda b: cnt_ref[base + b + 1] <= lo_slot, lambda b: b + 1, b_lo)
        b_hi = lax.while_loop(lambda b: (b < nbk) & (cnt_ref[base + jnp.minimum(b, nbk)] < hi_slot), lambda b: b + 1, b_hi)

        def body(b, acc, lo_slot=lo_slot):
            ai, ag = acc
            hit = q_ref[pl.ds(b, 1), :] == slot_col + float(lo_slot)
            tok = lane_tok + lax.convert_element_type(b * kb, F32)
            return ai + jnp.where(hit, tok, 0.0), ag + jnp.where(hit, aff_ref[pl.ds(b, 1), :], 0.0)

        zero = jnp.zeros((L, kb), F32)
        ai, ag = lax.fori_loop(b_lo, b_hi, body, (zero, zero))
        folded = functools.reduce(jnp.add, [ai[:, i * L:(i + 1) * L] for i in range(kb // L)])
        idx_ref[:, lo_slot:hi_slot] = jnp.sum(folded.T, axis=0, keepdims=True).astype(I32)
        gate_ref[lo_slot:hi_slot, :] = jnp.sum(ag, axis=1, keepdims=True)


def _compact(cnt, q, aff, cap):
    E, nbk, kb = q.shape
    grid_spec = pltpu.PrefetchScalarGridSpec(
        num_scalar_prefetch=1,
        grid=(E,),
        in_specs=[
            pl.BlockSpec((None, nbk, kb), lambda e, cnt: (e, 0, 0)),
            pl.BlockSpec((None, nbk, kb), lambda e, cnt: (e, 0, 0)),
        ],
        out_specs=[
            pl.BlockSpec((None, 1, cap), lambda e, cnt: (e, 0, 0)),
            pl.BlockSpec((cap, 1), lambda e, cnt: (e, 0)),
        ],
    )
    return pl.pallas_call(
        functools.partial(_compact_kernel, cap=cap),
        grid_spec=grid_spec,
        out_shape=[jax.ShapeDtypeStruct((E, 1, cap), I32), jax.ShapeDtypeStruct((E * cap, 1), F32)],
        compiler_params=_cparams(("parallel",)),
        name="moe_compact",
    )(cnt, q, aff)


def _ffn_kernel(idx_ref, h_hbm, gate_ref, wg_ref, wu_ref, wd_ref, o_ref, xbuf, x_ref, acc_ref, sem, *, cap, n_f):
    e = pl.program_id(0)
    f = pl.program_id(1)
    per_step = cap // n_f

    def fetch_row(expert, r):
        tok = idx_ref[expert * cap + r]
        pltpu.make_async_copy(h_hbm.at[pl.ds(tok, 1), :], xbuf.at[pl.ds(r, 1), :], sem).start()

    @pl.when(f == 0)
    def _():
        @pl.when(e == 0)
        def _():
            lax.fori_loop(0, cap, lambda r, c: (fetch_row(0, r), c)[1], 0, unroll=8)

        pltpu.make_async_copy(h_hbm.at[pl.ds(0, cap), :], xbuf, sem).wait()
        x_ref[...] = xbuf[...].astype(BF16)
        acc_ref[...] = jnp.zeros_like(acc_ref)

    x = x_ref[...]
    hid = jax.nn.silu(_dot(x, wg_ref[...].astype(BF16))) * _dot(x, wu_ref[...].astype(BF16))
    acc_ref[...] += _dot(hid.astype(BF16), wd_ref[...].astype(BF16))

    last_e = e == pl.num_programs(0) - 1
    nxt = jnp.where(last_e, 0, e + 1)
    for r in range(per_step):
        fetch_row(nxt, f * per_step + r)

    @pl.when(f == n_f - 1)
    def _():
        o_ref[:cap, :] = (acc_ref[...] * gate_ref[...]).astype(o_ref.dtype)
        o_ref[cap:, :] = jnp.zeros((o_ref.shape[0] - cap, o_ref.shape[1]), o_ref.dtype)

        @pl.when(last_e)
        def _():
            pltpu.make_async_copy(h_hbm.at[pl.ds(0, cap), :], xbuf, sem).wait()


def _ffn(idx, h, gate, wg, wu, wd, cap, pad, fn=256):
    E, D, Fw = wg.shape
    fn = min(fn, Fw)
    grid_spec = pltpu.PrefetchScalarGridSpec(
        num_scalar_prefetch=1,
        grid=(E, Fw // fn),
        in_specs=[
            pl.BlockSpec(memory_space=pl.ANY),
            pl.BlockSpec((cap, 1), lambda e, f, idx: (e, 0)),
            pl.BlockSpec((None, D, fn), lambda e, f, idx: (e, 0, f)),
            pl.BlockSpec((None, D, fn), lambda e, f, idx: (e, 0, f)),
            pl.BlockSpec((None, fn, D), lambda e, f, idx: (e, f, 0)),
        ],
        out_specs=pl.BlockSpec((cap + pad, D), lambda e, f, idx: (e, 0)),
        scratch_shapes=[
            pltpu.VMEM((cap, D), F32),
            pltpu.VMEM((cap, D), BF16),
            pltpu.VMEM((cap, D), F32),
            pltpu.SemaphoreType.DMA(()),
        ],
    )
    return pl.pallas_call(
        functools.partial(_ffn_kernel, cap=cap, n_f=Fw // fn),
        grid_spec=grid_spec,
        out_shape=jax.ShapeDtypeStruct((E * (cap + pad), D), BF16),
        compiler_params=_cparams(("arbitrary", "arbitrary")),
        name="moe_ffn",
    )(idx, h, gate, wg, wu, wd)


def _combine_kernel(cnt_ref, x_ref, qt_ref, o_hbm, g_ref, y_ref, obuf, acc_ref, sems, *, n_exp, stride, nbk):
    b = pl.program_id(0)
    E = n_exp
    W = obuf.shape[1] // E
    tile = 16

    def first_slots(blk):
        return [lax.shift_left(lax.shift_right_logical(cnt_ref[e * (nbk + 1) + blk], 4), 4) for e in range(E)]

    def copies(starts, buf):
        return [
            pltpu.make_async_copy(
                o_hbm.at[pl.ds(pl.multiple_of(e * stride + starts[e], tile), W), :],
                obuf.at[buf, pl.ds(e * W, W), :],
                sems.at[buf],
            )
            for e in range(E)
        ]

    lane = lax.broadcasted_iota(I32, (1, 2 * W), 1)
    first = lane < W
    j = jnp.where(first, lane, lane - W).astype(F32)

    def scatter(starts, buf):
        seg = []
        for e in range(0, E, 2):
            slot = jnp.where(first, qt_ref[:, e:e + 1], qt_ref[:, e + 1:e + 2])
            want = jnp.where(first, starts[e].astype(F32), starts[e + 1].astype(F32)) + j
            seg.append(jnp.where(slot == want, 1.0, 0.0).astype(BF16))
        acc_ref[...] += _dot(jnp.concatenate(seg, axis=1), obuf[buf])

    lo = first_slots(b)
    hi = [cnt_ref[e * (nbk + 1) + b + 1] for e in range(E)]
    span = functools.reduce(jnp.maximum, [h - l for h, l in zip(hi, lo)])
    n_pass = lax.shift_right_logical(span + (W - 1), W.bit_length() - 1)
    buf = lax.rem(b, 2)

    @pl.when(b == 0)
    def _():
        for cp in copies(lo, 0):
            cp.start()

    for cp in copies(lo, buf):
        cp.wait()

    @pl.when(b + 1 < pl.num_programs(0))
    def _():
        for cp in copies(first_slots(b + 1), 1 - buf):
            cp.start()

    acc_ref[...] = x_ref[...]
    scatter(lo, buf)

    def later_pass(p, carry):
        starts = [l + p * W for l in lo]
        for cp in copies(starts, buf):
            cp.start()
        for cp in copies(starts, buf):
            cp.wait()
        scatter(starts, buf)
        return carry

    lax.fori_loop(1, n_pass, later_pass, 0)
    y_ref[...] = _rms(acc_ref[...], g_ref[...])


def _combine(cnt, x1, qt, o, g, n_exp, stride):
    T, D = x1.shape
    tb = min(MOE_TOKEN_BLOCK, T)
    grid_spec = pltpu.PrefetchScalarGridSpec(
        num_scalar_prefetch=1,
        grid=(T // tb,),
        in_specs=[
            pl.BlockSpec((tb, D), lambda b, cnt: (b, 0)),
            pl.BlockSpec((tb, LANES), lambda b, cnt: (b, 0)),
            pl.BlockSpec(memory_space=pl.ANY),
            pl.BlockSpec((1, D), lambda b, cnt: (0, 0)),
        ],
        out_specs=pl.BlockSpec((tb, D), lambda b, cnt: (b, 0)),
        scratch_shapes=[
            pltpu.VMEM((2, n_exp * MOE_WINDOW, D), BF16),
            pltpu.VMEM((tb, D), F32),
            pltpu.SemaphoreType.DMA((2,)),
        ],
    )
    return pl.pallas_call(
        functools.partial(_combine_kernel, n_exp=n_exp, stride=stride, nbk=T // tb),
        grid_spec=grid_spec,
        out_shape=jax.ShapeDtypeStruct((T, D), F32),
        compiler_params=_cparams(("arbitrary",)),
        name="moe_combine_norm",
    )(cnt, x1, qt, o, g)


def _expert_choice_moe(x1, h2, aff_t, p):
    T, D = x1.shape
    E = aff_t.shape[0]
    cap = CAPACITY_FACTOR * T // E
    q, qt, offs = _route(aff_t, cap)
    nb = T // LANES
    per_blk = min(MOE_TOKEN_BLOCK, T) // LANES
    cnt = offs[:, 0].reshape(E, nb)[:, ::per_blk]
    cnt = jnp.concatenate([cnt, jnp.full((E, 1), cap, F32)], axis=1).astype(I32).reshape(-1)
    blocked = (E, nb // per_blk, per_blk * LANES)
    idx, gate = _compact(cnt, q.reshape(blocked), aff_t.reshape(blocked), cap)
    o = _ffn(idx.reshape(E * cap), h2, gate, p["w_expert_gate"], p["w_expert_up"], p["w_expert_down"], cap, MOE_WINDOW)
    return _combine(cnt, x1, qt, o, p["norm_final_g"], E, cap + MOE_WINDOW)


def _encoder_group(x, p):
    B, S, D = x.shape
    T = B * S
    hd = p["q_norm_g"].shape[-1]
    wq, wk = N_Q_HEADS * hd, N_KV_HEADS * hd
    wu = N_FOURIER_GROUPS * FOURIER_GROUP_W
    xt = x.reshape(T, D)

    q, k, v, u, gates = _inproj(xt, p["norm_mix_g"], p["w_in"], (wq, wk, wk, wu, 2 * D))
    cos_t, sin_t = _rope_tables(S, hd)
    qr, kt, va = _qkrope(q, k, v, p["q_norm_g"], p["k_norm_g"], cos_t, sin_t, S, ts=min(ATTN_KV_CHUNK, S))
    attn = _attention_any(qr, kt, va, p["q_norm_g"], p["k_norm_g"], B, S)
    f = _fourier(u, B, S)
    merged = _merge(attn, f, gates, p["w_attn_proj"], p["w_fourier_proj"])
    x1, h2, aff_t = _outproj(merged, xt, p["w_out"], p["norm_moe_g"], p["w_router_t"])
    return _expert_choice_moe(x1, h2, aff_t, p).reshape(B, S, D)


def kernel(x_prompt, x_sample, norm_mix_g, w_in, q_norm_g, k_norm_g, w_attn_proj, w_fourier_proj, w_out, norm_moe_g, w_router, w_expert_gate, w_expert_up, w_expert_down, norm_final_g):
    p = {
        "norm_mix_g": norm_mix_g[0][None, :],
        "w_in": w_in[0],
        "q_norm_g": q_norm_g[0][None, :],
        "k_norm_g": k_norm_g[0][None, :],
        "w_attn_proj": w_attn_proj[0].astype(BF16),
        "w_fourier_proj": w_fourier_proj[0].astype(BF16),
        "w_out": w_out[0].astype(BF16),
        "norm_moe_g": norm_moe_g[0][None, :],
        "w_router_t": w_router[0].T.astype(BF16),
        "w_expert_gate": w_expert_gate[0],
        "w_expert_up": w_expert_up[0],
        "w_expert_down": w_expert_down[0],
        "norm_final_g": norm_final_g[None, :],
    }
    return (_encoder_group(x_prompt, p), _encoder_group(x_sample, p))
```

```python
import functools
import math

import jax
import jax.numpy as jnp
from jax import lax
from jax.experimental import pallas as pl
from jax.experimental.pallas import tpu as pltpu

F32 = jnp.float32
BF16 = jnp.bfloat16
I32 = jnp.int32

EPS = 1e-6
N_Q_HEADS = 16
N_KV_HEADS = 4
ROPE_GRID_W = 64
ROPE_THETA = 10000.0
N_FOURIER_GROUPS = 4
FOURIER_GROUP_W = 256
CAPACITY_FACTOR = 2
LANES = 128
ATTN_KV_CHUNK = 512
VMEM_LIMIT = 56 * 1024 * 1024


def _cparams(sem):
    return pltpu.CompilerParams(dimension_semantics=sem, vmem_limit_bytes=VMEM_LIMIT)


def _dot(a, b):
    return jnp.dot(a, b, preferred_element_type=F32)


def _dot_nt(a, b):
    return lax.dot_general(a, b, (((1,), (1,)), ((), ())), preferred_element_type=F32)


def _rms(x, g):
    ms = jnp.mean(x * x, axis=-1, keepdims=True)
    return x * lax.rsqrt(ms + EPS) * g


def _inproj_kernel(x_ref, g_ref, w_ref, o_ref, h_ref):
    @pl.when(pl.program_id(1) == 0)
    def _():
        h_ref[...] = _rms(x_ref[...], g_ref[...]).astype(BF16)

    o_ref[...] = _dot(h_ref[...], w_ref[...].astype(BF16)).astype(o_ref.dtype)


def _inproj(x, g, w, tm=1024, tn=1024):
    T, D = x.shape
    N = w.shape[1]
    tm = min(tm, T)
    return pl.pallas_call(
        _inproj_kernel,
        grid=(T // tm, N // tn),
        in_specs=[
            pl.BlockSpec((tm, D), lambda i, j: (i, 0)),
            pl.BlockSpec((1, D), lambda i, j: (0, 0)),
            pl.BlockSpec((D, tn), lambda i, j: (0, j)),
        ],
        out_specs=pl.BlockSpec((tm, tn), lambda i, j: (i, j)),
        out_shape=jax.ShapeDtypeStruct((T, N), BF16),
        scratch_shapes=[pltpu.VMEM((tm, D), BF16)],
        compiler_params=_cparams(("parallel", "arbitrary")),
        name="inproj",
    )(x, g, w)


def _qkrope_kernel(q_ref, k_ref, v_ref, qg_ref, kg_ref, c_ref, s_ref, qo_ref, kto_ref, vo_ref, *, hd, scale):
    c = c_ref[...]
    s = s_ref[...]
    lane = lax.broadcasted_iota(I32, (1, hd), 1)
    low_half = (lane % (hd // 2)) < (hd // 4)

    def one(x, g):
        y = _rms(x, g)
        sw = jnp.where(low_half, pltpu.roll(y, hd - hd // 4, 1), pltpu.roll(y, hd // 4, 1))
        return y * c + sw * s

    for h in range(q_ref.shape[1] // hd):
        sl = slice(h * hd, (h + 1) * hd)
        qo_ref[h] = (one(q_ref[:, sl].astype(F32), qg_ref[...]) * scale).astype(qo_ref.dtype)
    ones = jnp.ones((v_ref.shape[0], hd), vo_ref.dtype)
    for h in range(k_ref.shape[1] // hd):
        sl = slice(h * hd, (h + 1) * hd)
        kto_ref[h, 0] = one(k_ref[:, sl].astype(F32), kg_ref[...]).T.astype(kto_ref.dtype)
        vo_ref[h, :, :hd] = v_ref[:, sl]
        vo_ref[h, :, hd:] = ones


def _qkrope(proj, wq, wk, qg, kg, cos_t, sin_t, S, ts):
    T = proj.shape[0]
    hd = qg.shape[-1]
    n_q, n_kv = wq // hd, wk // hd
    ns = S // ts
    k_blk = wq // wk
    return pl.pallas_call(
        functools.partial(_qkrope_kernel, hd=hd, scale=hd ** -0.5 * math.log2(math.e)),
        grid=(T // ts,),
        in_specs=[
            pl.BlockSpec((ts, wq), lambda i: (i, 0)),
            pl.BlockSpec((ts, wk), lambda i: (i, k_blk)),
            pl.BlockSpec((ts, wk), lambda i: (i, k_blk + 1)),
            pl.BlockSpec((1, hd), lambda i: (0, 0)),
            pl.BlockSpec((1, hd), lambda i: (0, 0)),
            pl.BlockSpec((ts, hd), lambda i: (i % ns, 0)),
            pl.BlockSpec((ts, hd), lambda i: (i % ns, 0)),
        ],
        out_specs=[
            pl.BlockSpec((n_q, ts, hd), lambda i: (0, i, 0)),
            pl.BlockSpec((n_kv, 1, hd, ts), lambda i: (0, i, 0, 0)),
            pl.BlockSpec((n_kv, ts, 2 * hd), lambda i: (0, i, 0)),
        ],
        out_shape=[
            jax.ShapeDtypeStruct((n_q, T, hd), BF16),
            jax.ShapeDtypeStruct((n_kv, T // ts, hd, ts), BF16),
            jax.ShapeDtypeStruct((n_kv, T, 2 * hd), BF16),
        ],
        compiler_params=_cparams(("parallel",)),
        name="qkrope",
    )(proj, proj, proj, qg, kg, cos_t, sin_t)


def _rope_tables(S, hd):
    axis = hd // 2
    pos = jnp.arange(S, dtype=F32)
    row = jnp.floor(pos / ROPE_GRID_W)
    col = pos - row * ROPE_GRID_W
    inv_freq = ROPE_THETA ** (-jnp.arange(0, axis, 2, dtype=F32) / axis)
    ang_r = row[:, None] * inv_freq
    ang_c = col[:, None] * inv_freq
    cr, sr, cc, sc = jnp.cos(ang_r), jnp.sin(ang_r), jnp.cos(ang_c), jnp.sin(ang_c)
    cos_t = jnp.concatenate([cr, cr, cc, cc], axis=-1)
    sin_t = jnp.concatenate([-sr, sr, -sc, sc], axis=-1)
    return cos_t, sin_t


def _attn_finish(acc, o_ref, G, tq, hd):
    for h in range(G):
        a = acc[h * tq:(h + 1) * tq]
        o_ref[:, h * hd:(h + 1) * hd] = (a[:, :hd] / a[:, hd:hd + 1]).astype(o_ref.dtype)


def _attn_bounded_kernel(q_ref, kt_ref, v_ref, o_ref, acc_ref, p_ref):
    G, tq, hd = q_ref.shape
    n_chunks, _, tk = kt_ref.shape
    q = q_ref[...].reshape(G * tq, hd)
    acc_ref[...] = jnp.zeros_like(acc_ref)
    p_ref[...] = jnp.exp2(_dot(q, kt_ref[0])).astype(BF16)

    def body(c, carry):
        s = _dot(q, kt_ref[c])
        off = pl.multiple_of((c - 1) * tk, tk)
        acc_ref[...] += _dot(p_ref[...], v_ref[pl.ds(off, tk), :])
        p_ref[...] = jnp.exp2(s).astype(BF16)
        return carry

    unroll = max(u for u in range(1, 9) if (n_chunks - 1) % u == 0) if n_chunks > 1 else 1
    lax.fori_loop(1, n_chunks, body, 0, unroll=unroll)
    acc_ref[...] += _dot(p_ref[...], v_ref[pl.ds((n_chunks - 1) * tk, tk), :])
    _attn_finish(acc_ref[...], o_ref, G, tq, hd)


def _attn_online_kernel(q_ref, kt_ref, v_ref, o_ref, acc_ref, m_ref):
    G, tq, hd = q_ref.shape
    n_chunks, _, tk = kt_ref.shape
    q = q_ref[...].reshape(G * tq, hd)
    acc_ref[...] = jnp.zeros_like(acc_ref)
    m_ref[...] = jnp.full_like(m_ref, -jnp.inf)

    def body(c, carry):
        off = pl.multiple_of(c * tk, tk)
        s = _dot(q, kt_ref[c])
        m = m_ref[...]
        m_new = jnp.maximum(m, jnp.max(s, axis=1, keepdims=True))
        p = jnp.exp2(s - m_new).astype(BF16)
        acc_ref[...] = jnp.exp2(m - m_new) * acc_ref[...] + _dot(p, v_ref[pl.ds(off, tk), :])
        m_ref[...] = m_new
        return carry

    lax.fori_loop(0, n_chunks, body, 0)
    _attn_finish(acc_ref[...], o_ref, G, tq, hd)


def _attention(q, kt, v, B, S, bounded, tq=256):
    n_q, T, hd = q.shape
    n_kv, _, _, tk = kt.shape
    G = n_q // n_kv
    tq = min(tq, S)
    nq = S // tq
    nc = S // tk
    scratch = [pltpu.VMEM((G * tq, 2 * hd), F32)]
    scratch.append(pltpu.VMEM((G * tq, tk), BF16) if bounded else pltpu.VMEM((G * tq, 1), F32))
    return pl.pallas_call(
        _attn_bounded_kernel if bounded else _attn_online_kernel,
        grid=(B, n_kv, nq),
        in_specs=[
            pl.BlockSpec((G, tq, hd), lambda b, g, i: (g, b * nq + i, 0)),
            pl.BlockSpec((None, nc, hd, tk), lambda b, g, i: (g, b, 0, 0)),
            pl.BlockSpec((None, S, 2 * hd), lambda b, g, i: (g, b, 0)),
        ],
        out_specs=pl.BlockSpec((tq, G * hd), lambda b, g, i: (b * nq + i, g)),
        out_shape=jax.ShapeDtypeStruct((T, n_q * hd), BF16),
        scratch_shapes=scratch,
        compiler_params=_cparams(("parallel", "parallel", "parallel")),
        name="attention_bounded" if bounded else "attention_online",
    )(q, kt, v)


MAX_UNSHIFTED_LOG2_SCORE = 100.0


def _attention_any(q, kt, v, qg, kg, B, S):
    hd = qg.shape[-1]
    bound = 1.02 * math.sqrt(hd) * math.log2(math.e) * jnp.max(jnp.abs(qg)) * jnp.max(jnp.abs(kg))
    return lax.cond(
        bound <= MAX_UNSHIFTED_LOG2_SCORE,
        lambda: _attention(q, kt, v, B, S, True),
        lambda: _attention(q, kt, v, B, S, False),
    )


def _dft_split(S):
    lg = int(math.log2(S))
    assert 1 << lg == S
    n2 = 1 << ((lg + 1) // 2)
    return S // n2, n2


def _cos_sin(n, m, period):
    ang = (2.0 * math.pi / period) * ((jnp.arange(n, dtype=I32)[:, None] * jnp.arange(m, dtype=I32)[None, :]) % period).astype(F32)
    return jnp.cos(ang), jnp.sin(ang)


def _f0_kernel(u_ref, cs_ref, a_ref, b_ref, *, gw):
    cs = cs_ref[...]
    for g in range(u_ref.shape[1] // gw):
        sl = slice(g * gw, (g + 1) * gw)
        ab = _dot(u_ref[:, sl], cs)
        a_ref[:, sl] = ab[:, :gw]
        b_ref[:, sl] = ab[:, gw:]


def _fa_kernel(a_hbm, b_hbm, ca_ref, sa_ref, ct_ref, st_ref, tr_hbm, ti_hbm, abuf, bbuf, trbuf, tibuf, sems):
    r, n2, W = abuf.shape
    bb = pl.program_id(0)
    j0 = pl.program_id(1) * r
    rows = pl.ds(bb * n2, n2)

    def loads(jj):
        return (
            pltpu.make_async_copy(a_hbm.at[rows, j0 + jj, :], abuf.at[jj], sems.at[0, jj]),
            pltpu.make_async_copy(b_hbm.at[rows, j0 + jj, :], bbuf.at[jj], sems.at[1, jj]),
        )

    def stores(jj):
        return (
            pltpu.make_async_copy(trbuf.at[jj], tr_hbm.at[rows, j0 + jj, :], sems.at[2, jj]),
            pltpu.make_async_copy(tibuf.at[jj], ti_hbm.at[rows, j0 + jj, :], sems.at[3, jj]),
        )

    for jj in range(r):
        for cp in loads(jj):
            cp.start()
    ca = ca_ref[...]
    sa = sa_ref[...]
    reps = W // LANES
    for jj in range(r):
        for cp in loads(jj):
            cp.wait()
        a = abuf[jj].astype(BF16)
        b = bbuf[jj].astype(BF16)
        tr = _dot(ca, a) - _dot(sa, b)
        ti = -(_dot(sa, a) + _dot(ca, b))
        ct = jnp.tile(ct_ref[jj], (1, reps))
        st = jnp.tile(st_ref[jj], (1, reps))
        trbuf[jj] = tr * ct + ti * st
        tibuf[jj] = ti * ct - tr * st
        for cp in stores(jj):
            cp.start()
    for jj in range(r):
        for cp in stores(jj):
            cp.wait()


def _fb_kernel(tr_ref, ti_ref, c1_ref, s1_ref, f_ref, *, scale):
    c1 = c1_ref[...]
    s1 = s1_ref[...]
    for kk in range(tr_ref.shape[0]):
        xr = tr_ref[kk].astype(BF16)
        xi = ti_ref[kk].astype(BF16)
        f_ref[:, kk, :] = (_dot(c1, xr) + _dot(s1, xi)) * scale


def _fourier(proj, u_blk, W, B, S, ts=512, r=8):
    u = proj
    T = proj.shape[0]
    gw = FOURIER_GROUP_W
    ts = min(ts, T)
    n1, n2 = _dft_split(S)

    cc, sc = _cos_sin(gw, gw, gw)
    cs = jnp.concatenate([cc, sc], axis=1).astype(BF16)
    a, b = pl.pallas_call(
        functools.partial(_f0_kernel, gw=gw),
        grid=(T // ts,),
        in_specs=[pl.BlockSpec((ts, W), lambda i: (i, u_blk)), pl.BlockSpec((gw, 2 * gw), lambda i: (0, 0))],
        out_specs=[pl.BlockSpec((ts, W), lambda i: (i, 0))] * 2,
        out_shape=[jax.ShapeDtypeStruct((T, W), F32)] * 2,
        compiler_params=_cparams(("parallel",)),
        name="fourier_channels",
    )(u, cs)

    a3 = a.reshape(B * n2, n1, W)
    b3 = b.reshape(B * n2, n1, W)
    ca, sa = _cos_sin(n2, n2, n2)
    ctw, stw = _cos_sin(n1, n2, S)
    ctw = jnp.broadcast_to(ctw[:, :, None], (n1, n2, LANES))
    stw = jnp.broadcast_to(stw[:, :, None], (n1, n2, LANES))
    tr, ti = pl.pallas_call(
        _fa_kernel,
        grid=(B, n1 // r),
        in_specs=[
            pl.BlockSpec(memory_space=pl.ANY),
            pl.BlockSpec(memory_space=pl.ANY),
            pl.BlockSpec((n2, n2), lambda bb, j: (0, 0)),
            pl.BlockSpec((n2, n2), lambda bb, j: (0, 0)),
            pl.BlockSpec((r, n2, LANES), lambda bb, j: (j, 0, 0)),
            pl.BlockSpec((r, n2, LANES), lambda bb, j: (j, 0, 0)),
        ],
        out_specs=[pl.BlockSpec(memory_space=pl.ANY)] * 2,
        out_shape=[jax.ShapeDtypeStruct((B * n2, n1, W), F32)] * 2,
        scratch_shapes=[pltpu.VMEM((r, n2, W), F32)] * 4 + [pltpu.SemaphoreType.DMA((4, r))],
        compiler_params=_cparams(("parallel", "parallel")),
        name="fourier_stage_a",
    )(a3, b3, ca.astype(BF16), sa.astype(BF16), ctw, stw)

    c1, s1 = _cos_sin(n1, n1, n1)
    f3 = pl.pallas_call(
        functools.partial(_fb_kernel, scale=1.0 / math.sqrt(S * gw)),
        grid=(B, n2 // r),
        in_specs=[
            pl.BlockSpec((r, n1, W), lambda bb, k: (bb * (n2 // r) + k, 0, 0)),
            pl.BlockSpec((r, n1, W), lambda bb, k: (bb * (n2 // r) + k, 0, 0)),
            pl.BlockSpec((n1, n1), lambda bb, k: (0, 0)),
            pl.BlockSpec((n1, n1), lambda bb, k: (0, 0)),
        ],
        out_specs=pl.BlockSpec((n1, r, W), lambda bb, k: (bb, k, 0)),
        out_shape=jax.ShapeDtypeStruct((B * n1, n2, W), F32),
        compiler_params=_cparams(("parallel", "parallel")),
        name="fourier_stage_b",
    )(tr, ti, c1.astype(BF16), s1.astype(BF16))
    return f3.reshape(T, W)


def _merge_kernel(at_ref, f_ref, ga_ref, gf_ref, wap_ref, wfp_ref, o_ref):
    a = _dot(at_ref[...], wap_ref[...])
    fo = _dot(f_ref[...].astype(BF16), wfp_ref[...])
    ga = jax.nn.sigmoid(ga_ref[...].astype(F32))
    gf = jax.nn.sigmoid(gf_ref[...].astype(F32))
    o_ref[...] = (ga * a + gf * fo).astype(o_ref.dtype)


def _merge(attn, f, gates, g_off, w_ap, w_fp, tm=1024, tn=1024):
    T, wa = attn.shape
    wf = f.shape[1]
    D = w_ap.shape[1]
    tm = min(tm, T)
    nj = D // tn
    g0 = g_off // tn
    return pl.pallas_call(
        _merge_kernel,
        grid=(T // tm, nj),
        in_specs=[
            pl.BlockSpec((tm, wa), lambda i, j: (i, 0)),
            pl.BlockSpec((tm, wf), lambda i, j: (i, 0)),
            pl.BlockSpec((tm, tn), lambda i, j: (i, g0 + j)),
            pl.BlockSpec((tm, tn), lambda i, j: (i, g0 + nj + j)),
            pl.BlockSpec((wa, tn), lambda i, j: (0, j)),
            pl.BlockSpec((wf, tn), lambda i, j: (0, j)),
        ],
        out_specs=pl.BlockSpec((tm, tn), lambda i, j: (i, j)),
        out_shape=jax.ShapeDtypeStruct((T, D), BF16),
        compiler_params=_cparams(("parallel", "parallel")),
        name="merge",
    )(attn, f, gates, gates, w_ap, w_fp)


def _outproj_kernel(m_ref, x_ref, w_ref, g_ref, wrt_ref, x1_ref, h_ref, afft_ref):
    x1 = x_ref[...] + _dot(m_ref[...], w_ref[...])
    x1_ref[...] = x1
    h32 = _rms(x1, g_ref[...])
    h_ref[...] = h32
    lt = _dot_nt(wrt_ref[...], h32.astype(BF16))
    et = jnp.exp(lt - jnp.max(lt, axis=0, keepdims=True))
    afft_ref[...] = et / jnp.sum(et, axis=0, keepdims=True)


def _outproj(merged, x, w_out, g, w_router_t, tm=512):
    T, D = x.shape
    E = w_router_t.shape[0]
    tm = min(tm, T)
    return pl.pallas_call(
        _outproj_kernel,
        grid=(T // tm,),
        in_specs=[
            pl.BlockSpec((tm, D), lambda i: (i, 0)),
            pl.BlockSpec((tm, D), lambda i: (i, 0)),
            pl.BlockSpec((D, D), lambda i: (0, 0), pipeline_mode=pl.Buffered(1)),
            pl.BlockSpec((1, D), lambda i: (0, 0)),
            pl.BlockSpec((E, D), lambda i: (0, 0)),
        ],
        out_specs=[
            pl.BlockSpec((tm, D), lambda i: (i, 0)),
            pl.BlockSpec((tm, D), lambda i: (i, 0)),
            pl.BlockSpec((E, tm), lambda i: (0, i)),
        ],
        out_shape=[
            jax.ShapeDtypeStruct((T, D), F32),
            jax.ShapeDtypeStruct((T, D), F32),
            jax.ShapeDtypeStruct((E, T), F32),
        ],
        compiler_params=_cparams(("parallel",)),
        name="outproj_router",
    )(merged, x, w_out, g, w_router_t)


def _route_kernel(aff_ref, q_ref, qt_ref, offs_ref, *, cap):
    E, nb, L = aff_ref.shape
    rows = E * nb
    aff = aff_ref[...]

    def count(mask):
        c = jnp.sum(mask.astype(F32), axis=2, keepdims=True)
        return jnp.sum(c, axis=1, keepdims=True)

    def search(i, prefix):
        cand = prefix | (jnp.int32(1) << (30 - i))
        ge = aff >= lax.bitcast_convert_type(cand, F32)
        return jnp.where(count(ge) >= cap, cand, prefix)

    thr = lax.bitcast_convert_type(lax.fori_loop(0, 31, search, jnp.zeros((E, 1, 1), I32)), F32)

    ri = lax.broadcasted_iota(I32, (L, L), 0)
    ci = lax.broadcasted_iota(I32, (L, L), 1)
    tri = (ri <= ci).astype(BF16)
    ones = jnp.ones((L, L), BF16)
    rr = lax.broadcasted_iota(I32, (rows, rows), 0)
    rc = lax.broadcasted_iota(I32, (rows, rows), 1)
    sh = nb.bit_length() - 1
    same_expert = lax.shift_right_logical(rr, sh) == lax.shift_right_logical(rc, sh)
    earlier = (same_expert & (rc < rr)).astype(BF16)

    def cumsum(mask):
        x = mask.astype(F32).reshape(rows, L).astype(BF16)
        within = _dot(x, tri)
        tot = _dot(x, ones)
        offs = _dot(earlier, tot.astype(BF16))
        return (within + offs).reshape(E, nb, L), offs

    gt = aff > thr
    eq = aff == thr
    need = cap - count(gt)
    sel = gt | (eq & (cumsum(eq)[0] <= need))
    rank, offs = cumsum(sel)
    q_ref[...] = jnp.where(sel, rank - 1.0, -1.0)
    offs_ref[...] = offs

    filler = jnp.full((L - E, L), -1.0, F32)
    for b in range(nb):
        tile = jnp.concatenate([q_ref[:, b, :], filler], axis=0)
        qt_ref[b * L:(b + 1) * L, :] = tile.T


def _route(aff_t, cap):
    E, T = aff_t.shape
    nb = T // LANES
    aff3 = aff_t.reshape(E, nb, LANES)
    return pl.pallas_call(
        functools.partial(_route_kernel, cap=cap),
        out_shape=[
            jax.ShapeDtypeStruct((E, nb, LANES), F32),
            jax.ShapeDtypeStruct((T, LANES), F32),
            jax.ShapeDtypeStruct((E * nb, LANES), F32),
        ],
        compiler_params=pltpu.CompilerParams(vmem_limit_bytes=VMEM_LIMIT),
        name="route",
    )(aff3)


MOE_TOKEN_BLOCK = 256
MOE_WINDOW = 64


def _compact_kernel(cnt_ref, q_ref, aff_ref, idx_ref, gate_ref, *, cap):
    e = pl.program_id(0)
    nbk, kb = q_ref.shape
    L = LANES
    base = e * (nbk + 1)
    slot_col = lax.broadcasted_iota(I32, (L, 1), 0).astype(F32)
    lane_tok = lax.broadcasted_iota(I32, (1, kb), 1).astype(F32)
    b_lo = b_hi = jnp.int32(0)
    for c in range(cap // L):
        lo_slot, hi_slot = c * L, (c + 1) * L
        b_lo = lax.while_loop(lambda b: cnt_ref[base + b + 1] <= lo_slot, lambda b: b + 1, b_lo)
        b_hi = lax.while_loop(lambda b: (b < nbk) & (cnt_ref[base + jnp.minimum(b, nbk)] < hi_slot), lambda b: b + 1, b_hi)

        def body(b, acc, lo_slot=lo_slot):
            ai, ag = acc
            hit = q_ref[pl.ds(b, 1), :] == slot_col + float(lo_slot)
            tok = lane_tok + lax.convert_element_type(b * kb, F32)
            return ai + jnp.where(hit, tok, 0.0), ag + jnp.where(hit, aff_ref[pl.ds(b, 1), :], 0.0)

        zero = jnp.zeros((L, kb), F32)
        ai, ag = lax.fori_loop(b_lo, b_hi, body, (zero, zero))
        folded = functools.reduce(jnp.add, [ai[:, i * L:(i + 1) * L] for i in range(kb // L)])
        idx_ref[:, lo_slot:hi_slot] = jnp.sum(folded.T, axis=0, keepdims=True).astype(I32)
        gate_ref[lo_slot:hi_slot, :] = jnp.sum(ag, axis=1, keepdims=True)


def _compact(cnt, q, aff, cap):
    E, nbk, kb = q.shape
    grid_spec = pltpu.PrefetchScalarGridSpec(
        num_scalar_prefetch=1,
        grid=(E,),
        in_specs=[
            pl.BlockSpec((None, nbk, kb), lambda e, cnt: (e, 0, 0)),
            pl.BlockSpec((None, nbk, kb), lambda e, cnt: (e, 0, 0)),
        ],
        out_specs=[
            pl.BlockSpec((None, 1, cap), lambda e, cnt: (e, 0, 0)),
            pl.BlockSpec((cap, 1), lambda e, cnt: (e, 0)),
        ],
    )
    return pl.pallas_call(
        functools.partial(_compact_kernel, cap=cap),
        grid_spec=grid_spec,
        out_shape=[jax.ShapeDtypeStruct((E, 1, cap), I32), jax.ShapeDtypeStruct((E * cap, 1), F32)],
        compiler_params=_cparams(("parallel",)),
        name="moe_compact",
    )(cnt, q, aff)


def _ffn_kernel(idx_ref, h_hbm, gate_ref, wg_ref, wu_ref, wd_ref, o_ref, xbuf, x_ref, acc_ref, sem, *, cap, n_f):
    e = pl.program_id(0)
    f = pl.program_id(1)
    per_step = cap // n_f

    def fetch_row(expert, r):
        tok = idx_ref[expert * cap + r]
        pltpu.make_async_copy(h_hbm.at[pl.ds(tok, 1), :], xbuf.at[pl.ds(r, 1), :], sem).start()

    @pl.when(f == 0)
    def _():
        @pl.when(e == 0)
        def _():
            lax.fori_loop(0, cap, lambda r, c: (fetch_row(0, r), c)[1], 0, unroll=8)

        pltpu.make_async_copy(h_hbm.at[pl.ds(0, cap), :], xbuf, sem).wait()
        x_ref[...] = xbuf[...].astype(BF16)
        acc_ref[...] = jnp.zeros_like(acc_ref)

    x = x_ref[...]
    hid = jax.nn.silu(_dot(x, wg_ref[...].astype(BF16))) * _dot(x, wu_ref[...].astype(BF16))
    acc_ref[...] += _dot(hid.astype(BF16), wd_ref[...].astype(BF16))

    last_e = e == pl.num_programs(0) - 1
    nxt = jnp.where(last_e, 0, e + 1)
    for r in range(per_step):
        fetch_row(nxt, f * per_step + r)

    @pl.when(f == n_f - 1)
    def _():
        o_ref[:cap, :] = (acc_ref[...] * gate_ref[...]).astype(o_ref.dtype)
        o_ref[cap:, :] = jnp.zeros((o_ref.shape[0] - cap, o_ref.shape[1]), o_ref.dtype)

        @pl.when(last_e)
        def _():
            pltpu.make_async_copy(h_hbm.at[pl.ds(0, cap), :], xbuf, sem).wait()


def _ffn(idx, h, gate, wg, wu, wd, cap, pad, fn=256):
    E, D, Fw = wg.shape
    fn = min(fn, Fw)
    grid_spec = pltpu.PrefetchScalarGridSpec(
        num_scalar_prefetch=1,
        grid=(E, Fw // fn),
        in_specs=[
            pl.BlockSpec(memory_space=pl.ANY),
            pl.BlockSpec((cap, 1), lambda e, f, idx: (e, 0)),
            pl.BlockSpec((None, D, fn), lambda e, f, idx: (e, 0, f)),
            pl.BlockSpec((None, D, fn), lambda e, f, idx: (e, 0, f)),
            pl.BlockSpec((None, fn, D), lambda e, f, idx: (e, f, 0)),
        ],
        out_specs=pl.BlockSpec((cap + pad, D), lambda e, f, idx: (e, 0)),
        scratch_shapes=[
            pltpu.VMEM((cap, D), F32),
            pltpu.VMEM((cap, D), BF16),
            pltpu.VMEM((cap, D), F32),
            pltpu.SemaphoreType.DMA(()),
        ],
    )
    return pl.pallas_call(
        functools.partial(_ffn_kernel, cap=cap, n_f=Fw // fn),
        grid_spec=grid_spec,
        out_shape=jax.ShapeDtypeStruct((E * (cap + pad), D), BF16),
        compiler_params=_cparams(("arbitrary", "arbitrary")),
        name="moe_ffn",
    )(idx, h, gate, wg, wu, wd)


def _combine_kernel(cnt_ref, x_ref, qt_ref, o_hbm, g_ref, y_ref, obuf, acc_ref, sems, *, n_exp, stride, nbk):
    b = pl.program_id(0)
    E = n_exp
    W = obuf.shape[1] // E
    tile = 16

    def first_slots(blk):
        return [lax.shift_left(lax.shift_right_logical(cnt_ref[e * (nbk + 1) + blk], 4), 4) for e in range(E)]

    def copies(starts, buf):
        return [
            pltpu.make_async_copy(
                o_hbm.at[pl.ds(pl.multiple_of(e * stride + starts[e], tile), W), :],
                obuf.at[buf, pl.ds(e * W, W), :],
                sems.at[buf],
            )
            for e in range(E)
        ]

    lane = lax.broadcasted_iota(I32, (1, 2 * W), 1)
    first = lane < W
    j = jnp.where(first, lane, lane - W).astype(F32)

    def scatter(starts, buf):
        seg = []
        for e in range(0, E, 2):
            slot = jnp.where(first, qt_ref[:, e:e + 1], qt_ref[:, e + 1:e + 2])
            want = jnp.where(first, starts[e].astype(F32), starts[e + 1].astype(F32)) + j
            seg.append(jnp.where(slot == want, 1.0, 0.0).astype(BF16))
        acc_ref[...] += _dot(jnp.concatenate(seg, axis=1), obuf[buf])

    lo = first_slots(b)
    hi = [cnt_ref[e * (nbk + 1) + b + 1] for e in range(E)]
    span = functools.reduce(jnp.maximum, [h - l for h, l in zip(hi, lo)])
    n_pass = lax.shift_right_logical(span + (W - 1), W.bit_length() - 1)
    buf = lax.rem(b, 2)

    @pl.when(b == 0)
    def _():
        for cp in copies(lo, 0):
            cp.start()

    for cp in copies(lo, buf):
        cp.wait()

    @pl.when(b + 1 < pl.num_programs(0))
    def _():
        for cp in copies(first_slots(b + 1), 1 - buf):
            cp.start()

    acc_ref[...] = x_ref[...]
    scatter(lo, buf)

    def later_pass(p, carry):
        starts = [l + p * W for l in lo]
        for cp in copies(starts, buf):
            cp.start()
        for cp in copies(starts, buf):
            cp.wait()
        scatter(starts, buf)
        return carry

    lax.fori_loop(1, n_pass, later_pass, 0)
    y_ref[...] = _rms(acc_ref[...], g_ref[...])


def _combine(cnt, x1, qt, o, g, n_exp, stride):
    T, D = x1.shape
    tb = min(MOE_TOKEN_BLOCK, T)
    grid_spec = pltpu.PrefetchScalarGridSpec(
        num_scalar_prefetch=1,
        grid=(T // tb,),
        in_specs=[
            pl.BlockSpec((tb, D), lambda b, cnt: (b, 0)),
            pl.BlockSpec((tb, LANES), lambda b, cnt: (b, 0)),
            pl.BlockSpec(memory_space=pl.ANY),
            pl.BlockSpec((1, D), lambda b, cnt: (0, 0)),
        ],
        out_specs=pl.BlockSpec((tb, D), lambda b, cnt: (b, 0)),
        scratch_shapes=[
            pltpu.VMEM((2, n_exp * MOE_WINDOW, D), BF16),
            pltpu.VMEM((tb, D), F32),
            pltpu.SemaphoreType.DMA((2,)),
        ],
    )
    return pl.pallas_call(
        functools.partial(_combine_kernel, n_exp=n_exp, stride=stride, nbk=T // tb),
        grid_spec=grid_spec,
        out_shape=jax.ShapeDtypeStruct((T, D), F32),
        compiler_params=_cparams(("arbitrary",)),
        name="moe_combine_norm",
    )(cnt, x1, qt, o, g)


def _expert_choice_moe(x1, h2, aff_t, p):
    T, D = x1.shape
    E = aff_t.shape[0]
    cap = CAPACITY_FACTOR * T // E
    q, qt, offs = _route(aff_t, cap)
    nb = T // LANES
    per_blk = min(MOE_TOKEN_BLOCK, T) // LANES
    cnt = offs[:, 0].reshape(E, nb)[:, ::per_blk]
    cnt = jnp.concatenate([cnt, jnp.full((E, 1), cap, F32)], axis=1).astype(I32).reshape(-1)
    blocked = (E, nb // per_blk, per_blk * LANES)
    idx, gate = _compact(cnt, q.reshape(blocked), aff_t.reshape(blocked), cap)
    o = _ffn(idx.reshape(E * cap), h2, gate, p["w_expert_gate"], p["w_expert_up"], p["w_expert_down"], cap, MOE_WINDOW)
    return _combine(cnt, x1, qt, o, p["norm_final_g"], E, cap + MOE_WINDOW)


def _encoder_group(x, p):
    B, S, D = x.shape
    T = B * S
    hd = p["q_norm_g"].shape[-1]
    wq, wk = N_Q_HEADS * hd, N_KV_HEADS * hd
    wu = N_FOURIER_GROUPS * FOURIER_GROUP_W
    xt = x.reshape(T, D)

    proj = _inproj(xt, p["norm_mix_g"], p["w_in"])
    u_off = wq + 2 * wk
    cos_t, sin_t = _rope_tables(S, hd)
    qr, kt, va = _qkrope(proj, wq, wk, p["q_norm_g"], p["k_norm_g"], cos_t, sin_t, S, ts=min(ATTN_KV_CHUNK, S))
    attn = _attention_any(qr, kt, va, p["q_norm_g"], p["k_norm_g"], B, S)
    f = _fourier(proj, u_off // wu, wu, B, S)
    merged = _merge(attn, f, proj, u_off + wu, p["w_attn_proj"], p["w_fourier_proj"])
    x1, h2, aff_t = _outproj(merged, xt, p["w_out"], p["norm_moe_g"], p["w_router_t"])
    return _expert_choice_moe(x1, h2, aff_t, p).reshape(B, S, D)


def kernel(x_prompt, x_sample, norm_mix_g, w_in, q_norm_g, k_norm_g, w_attn_proj, w_fourier_proj, w_out, norm_moe_g, w_router, w_expert_gate, w_expert_up, w_expert_down, norm_final_g):
    p = {
        "norm_mix_g": norm_mix_g[0][None, :],
        "w_in": w_in[0],
        "q_norm_g": q_norm_g[0][None, :],
        "k_norm_g": k_norm_g[0][None, :],
        "w_attn_proj": w_attn_proj[0].astype(BF16),
        "w_fourier_proj": w_fourier_proj[0].astype(BF16),
        "w_out": w_out[0].astype(BF16),
        "norm_moe_g": norm_moe_g[0][None, :],
        "w_router_t": w_router[0].T.astype(BF16),
        "w_expert_gate": w_expert_gate[0],
        "w_expert_up": w_expert_up[0],
        "w_expert_down": w_expert_down[0],
        "norm_final_g": norm_final_g[None, :],
    }
    return (_encoder_group(x_prompt, p), _encoder_group(x_sample, p))
```

```python
import functools
import math

import jax
import jax.numpy as jnp
from jax import lax
from jax.experimental import pallas as pl
from jax.experimental.pallas import tpu as pltpu

F32 = jnp.float32
BF16 = jnp.bfloat16
I32 = jnp.int32

EPS = 1e-6
N_Q_HEADS = 16
N_KV_HEADS = 4
ROPE_GRID_W = 64
ROPE_THETA = 10000.0
N_FOURIER_GROUPS = 4
FOURIER_GROUP_W = 256
CAPACITY_FACTOR = 2
LANES = 128
BF16_ROWS = 16
ATTN_KV_CHUNK = 512
VMEM_LIMIT = 56 * 1024 * 1024


def _cparams(sem):
    return pltpu.CompilerParams(dimension_semantics=sem, vmem_limit_bytes=VMEM_LIMIT)


def _dot(a, b):
    return jnp.dot(a, b, preferred_element_type=F32)


def _dot_nt(a, b):
    return lax.dot_general(a, b, (((1,), (1,)), ((), ())), preferred_element_type=F32)


def _rms(x, g):
    ms = jnp.mean(x * x, axis=-1, keepdims=True)
    return x * lax.rsqrt(ms + EPS) * g


def _inproj_kernel(x_ref, g_ref, w_ref, o_ref, h_ref):
    @pl.when(pl.program_id(1) == 0)
    def _():
        h_ref[...] = _rms(x_ref[...], g_ref[...]).astype(BF16)

    o_ref[...] = _dot(h_ref[...], w_ref[...].astype(BF16)).astype(o_ref.dtype)


def _inproj(x, g, w, tm=1024, tn=1024):
    T, D = x.shape
    N = w.shape[1]
    tm = min(tm, T)
    return pl.pallas_call(
        _inproj_kernel,
        grid=(T // tm, N // tn),
        in_specs=[
            pl.BlockSpec((tm, D), lambda i, j: (i, 0)),
            pl.BlockSpec((1, D), lambda i, j: (0, 0)),
            pl.BlockSpec((D, tn), lambda i, j: (0, j)),
        ],
        out_specs=pl.BlockSpec((tm, tn), lambda i, j: (i, j)),
        out_shape=jax.ShapeDtypeStruct((T, N), BF16),
        scratch_shapes=[pltpu.VMEM((tm, D), BF16)],
        compiler_params=_cparams(("parallel", "arbitrary")),
        name="inproj",
    )(x, g, w)


def _qkrope_kernel(q_ref, k_ref, v_ref, qg_ref, kg_ref, c_ref, s_ref, qo_ref, ko_ref, vo_ref, *, hd, scale):
    c = c_ref[...]
    s = s_ref[...]
    lane = lax.broadcasted_iota(I32, (1, hd), 1)
    low_half = (lane % (hd // 2)) < (hd // 4)

    def one(x, g):
        y = _rms(x, g)
        sw = jnp.where(low_half, pltpu.roll(y, hd - hd // 4, 1), pltpu.roll(y, hd // 4, 1))
        return y * c + sw * s

    for h in range(q_ref.shape[1] // hd):
        sl = slice(h * hd, (h + 1) * hd)
        qo_ref[h] = (one(q_ref[:, sl].astype(F32), qg_ref[...]) * scale).astype(qo_ref.dtype)
    pad = vo_ref.shape[2] - hd
    ones_row = jnp.where(lax.broadcasted_iota(I32, (pad, v_ref.shape[0]), 0) == 0, 1.0, 0.0)
    for h in range(k_ref.shape[1] // hd):
        sl = slice(h * hd, (h + 1) * hd)
        ko_ref[:, sl] = one(k_ref[:, sl].astype(F32), kg_ref[...]).astype(ko_ref.dtype)
        vt = jnp.concatenate([v_ref[:, sl].astype(F32).T, ones_row], axis=0)
        vo_ref[h, 0] = vt.astype(vo_ref.dtype)


def _qkrope(proj, wq, wk, qg, kg, cos_t, sin_t, S, ts):
    T = proj.shape[0]
    hd = qg.shape[-1]
    n_q, n_kv = wq // hd, wk // hd
    ns = S // ts
    k_blk = wq // wk
    return pl.pallas_call(
        functools.partial(_qkrope_kernel, hd=hd, scale=hd ** -0.5 * math.log2(math.e)),
        grid=(T // ts,),
        in_specs=[
            pl.BlockSpec((ts, wq), lambda i: (i, 0)),
            pl.BlockSpec((ts, wk), lambda i: (i, k_blk)),
            pl.BlockSpec((ts, wk), lambda i: (i, k_blk + 1)),
            pl.BlockSpec((1, hd), lambda i: (0, 0)),
            pl.BlockSpec((1, hd), lambda i: (0, 0)),
            pl.BlockSpec((ts, hd), lambda i: (i % ns, 0)),
            pl.BlockSpec((ts, hd), lambda i: (i % ns, 0)),
        ],
        out_specs=[
            pl.BlockSpec((n_q, ts, hd), lambda i: (0, i, 0)),
            pl.BlockSpec((ts, wk), lambda i: (i, 0)),
            pl.BlockSpec((n_kv, 1, hd + BF16_ROWS, ts), lambda i: (0, i, 0, 0)),
        ],
        out_shape=[
            jax.ShapeDtypeStruct((n_q, T, hd), BF16),
            jax.ShapeDtypeStruct((T, wk), BF16),
            jax.ShapeDtypeStruct((n_kv, T // ts, hd + BF16_ROWS, ts), BF16),
        ],
        compiler_params=_cparams(("parallel",)),
        name="qkrope",
    )(proj, proj, proj, qg, kg, cos_t, sin_t)


def _rope_tables(S, hd):
    axis = hd // 2
    pos = jnp.arange(S, dtype=F32)
    row = jnp.floor(pos / ROPE_GRID_W)
    col = pos - row * ROPE_GRID_W
    inv_freq = ROPE_THETA ** (-jnp.arange(0, axis, 2, dtype=F32) / axis)
    ang_r = row[:, None] * inv_freq
    ang_c = col[:, None] * inv_freq
    cr, sr, cc, sc = jnp.cos(ang_r), jnp.sin(ang_r), jnp.cos(ang_c), jnp.sin(ang_c)
    cos_t = jnp.concatenate([cr, cr, cc, cc], axis=-1)
    sin_t = jnp.concatenate([-sr, sr, -sc, sc], axis=-1)
    return cos_t, sin_t


def _attn_load_q(q_ref, qt_ref):
    for h in range(q_ref.shape[0]):
        qt_ref[h] = q_ref[h].astype(F32).T.astype(qt_ref.dtype)


def _attn_finish(acc_ref, o_ref, hd):
    for h in range(acc_ref.shape[0]):
        a = acc_ref[h]
        o_ref[:, h * hd:(h + 1) * hd] = (a[:hd] / a[hd:hd + 1]).T.astype(o_ref.dtype)


def _attn_bounded_kernel(q_ref, k_ref, vt_ref, o_ref, qt_ref, acc_ref, p_ref):
    G, tq, hd = q_ref.shape
    n_chunks, _, tk = vt_ref.shape
    _attn_load_q(q_ref, qt_ref)
    acc_ref[...] = jnp.zeros_like(acc_ref)

    def keys(c):
        return k_ref[pl.ds(pl.multiple_of(c * tk, tk), tk), :]

    k0 = keys(0)
    for h in range(G):
        p_ref[h] = jnp.exp2(_dot(k0, qt_ref[h])).astype(BF16)

    def body(c, carry):
        kc = keys(c)
        vt = vt_ref[c - 1]
        for h in range(G):
            s = _dot(kc, qt_ref[h])
            acc_ref[h] += _dot(vt, p_ref[h])
            p_ref[h] = jnp.exp2(s).astype(BF16)
        return carry

    unroll = max(u for u in range(1, 9) if (n_chunks - 1) % u == 0) if n_chunks > 1 else 1
    lax.fori_loop(1, n_chunks, body, 0, unroll=unroll)
    vt = vt_ref[n_chunks - 1]
    for h in range(G):
        acc_ref[h] += _dot(vt, p_ref[h])
    _attn_finish(acc_ref, o_ref, hd)


def _attn_online_kernel(q_ref, k_ref, vt_ref, o_ref, qt_ref, acc_ref, m_ref):
    G, tq, hd = q_ref.shape
    n_chunks, _, tk = vt_ref.shape
    _attn_load_q(q_ref, qt_ref)
    acc_ref[...] = jnp.zeros_like(acc_ref)
    m_ref[...] = jnp.full_like(m_ref, -jnp.inf)

    def body(c, carry):
        kc = k_ref[pl.ds(pl.multiple_of(c * tk, tk), tk), :]
        vt = vt_ref[c]
        for h in range(G):
            s = _dot(kc, qt_ref[h])
            m = m_ref[h]
            m_new = jnp.maximum(m, jnp.max(s, axis=0, keepdims=True))
            p = jnp.exp2(s - m_new).astype(BF16)
            acc_ref[h] = jnp.exp2(m - m_new) * acc_ref[h] + _dot(vt, p)
            m_ref[h] = m_new
        return carry

    lax.fori_loop(0, n_chunks, body, 0)
    _attn_finish(acc_ref, o_ref, hd)


def _attention(q, k, vt, B, S, bounded, tq=256):
    n_q, T, hd = q.shape
    n_kv, _, vr, tk = vt.shape
    G = n_q // n_kv
    tq = min(tq, S)
    nq = S // tq
    nc = S // tk
    scratch = [pltpu.VMEM((G, hd, tq), BF16), pltpu.VMEM((G, vr, tq), F32)]
    scratch.append(pltpu.VMEM((G, tk, tq), BF16) if bounded else pltpu.VMEM((G, 1, tq), F32))
    return pl.pallas_call(
        _attn_bounded_kernel if bounded else _attn_online_kernel,
        grid=(B, n_kv, nq),
        in_specs=[
            pl.BlockSpec((G, tq, hd), lambda b, g, i: (g, b * nq + i, 0)),
            pl.BlockSpec((S, hd), lambda b, g, i: (b, g)),
            pl.BlockSpec((None, nc, vr, tk), lambda b, g, i: (g, b, 0, 0)),
        ],
        out_specs=pl.BlockSpec((tq, G * hd), lambda b, g, i: (b * nq + i, g)),
        out_shape=jax.ShapeDtypeStruct((T, n_q * hd), BF16),
        scratch_shapes=scratch,
        compiler_params=_cparams(("parallel", "parallel", "parallel")),
        name="attention_bounded" if bounded else "attention_online",
    )(q, k, vt)


MAX_UNSHIFTED_LOG2_SCORE = 100.0


def _attention_any(q, kt, v, qg, kg, B, S):
    hd = qg.shape[-1]
    bound = 1.02 * math.sqrt(hd) * math.log2(math.e) * jnp.max(jnp.abs(qg)) * jnp.max(jnp.abs(kg))
    return lax.cond(
        bound <= MAX_UNSHIFTED_LOG2_SCORE,
        lambda: _attention(q, kt, v, B, S, True),
        lambda: _attention(q, kt, v, B, S, False),
    )


def _dft_split(S):
    lg = int(math.log2(S))
    assert 1 << lg == S
    n2 = 1 << ((lg + 1) // 2)
    return S // n2, n2


def _cos_sin(n, m, period):
    ang = (2.0 * math.pi / period) * ((jnp.arange(n, dtype=I32)[:, None] * jnp.arange(m, dtype=I32)[None, :]) % period).astype(F32)
    return jnp.cos(ang), jnp.sin(ang)


def _f0_kernel(u_ref, cs_ref, a_ref, b_ref, *, gw):
    cs = cs_ref[...]
    for g in range(u_ref.shape[1] // gw):
        sl = slice(g * gw, (g + 1) * gw)
        ab = _dot(u_ref[:, sl], cs)
        a_ref[:, sl] = ab[:, :gw]
        b_ref[:, sl] = ab[:, gw:]


def _fa_kernel(a_hbm, b_hbm, ca_ref, sa_ref, ct_ref, st_ref, tr_hbm, ti_hbm, abuf, bbuf, trbuf, tibuf, sems):
    r, n2, W = abuf.shape
    bb = pl.program_id(0)
    j0 = pl.program_id(1) * r
    rows = pl.ds(bb * n2, n2)

    def loads(jj):
        return (
            pltpu.make_async_copy(a_hbm.at[rows, j0 + jj, :], abuf.at[jj], sems.at[0, jj]),
            pltpu.make_async_copy(b_hbm.at[rows, j0 + jj, :], bbuf.at[jj], sems.at[1, jj]),
        )

    def stores(jj):
        return (
            pltpu.make_async_copy(trbuf.at[jj], tr_hbm.at[rows, j0 + jj, :], sems.at[2, jj]),
            pltpu.make_async_copy(tibuf.at[jj], ti_hbm.at[rows, j0 + jj, :], sems.at[3, jj]),
        )

    for jj in range(r):
        for cp in loads(jj):
            cp.start()
    ca = ca_ref[...]
    sa = sa_ref[...]
    reps = W // LANES
    for jj in range(r):
        for cp in loads(jj):
            cp.wait()
        a = abuf[jj].astype(BF16)
        b = bbuf[jj].astype(BF16)
        tr = _dot(ca, a) - _dot(sa, b)
        ti = -(_dot(sa, a) + _dot(ca, b))
        ct = jnp.tile(ct_ref[jj], (1, reps))
        st = jnp.tile(st_ref[jj], (1, reps))
        trbuf[jj] = tr * ct + ti * st
        tibuf[jj] = ti * ct - tr * st
        for cp in stores(jj):
            cp.start()
    for jj in range(r):
        for cp in stores(jj):
            cp.wait()


def _fb_kernel(tr_ref, ti_ref, c1_ref, s1_ref, f_ref, *, scale):
    c1 = c1_ref[...]
    s1 = s1_ref[...]
    for kk in range(tr_ref.shape[0]):
        xr = tr_ref[kk].astype(BF16)
        xi = ti_ref[kk].astype(BF16)
        f_ref[:, kk, :] = (_dot(c1, xr) + _dot(s1, xi)) * scale


def _fourier(proj, u_blk, W, B, S, ts=512, r=8):
    u = proj
    T = proj.shape[0]
    gw = FOURIER_GROUP_W
    ts = min(ts, T)
    n1, n2 = _dft_split(S)

    cc, sc = _cos_sin(gw, gw, gw)
    cs = jnp.concatenate([cc, sc], axis=1).astype(BF16)
    a, b = pl.pallas_call(
        functools.partial(_f0_kernel, gw=gw),
        grid=(T // ts,),
        in_specs=[pl.BlockSpec((ts, W), lambda i: (i, u_blk)), pl.BlockSpec((gw, 2 * gw), lambda i: (0, 0))],
        out_specs=[pl.BlockSpec((ts, W), lambda i: (i, 0))] * 2,
        out_shape=[jax.ShapeDtypeStruct((T, W), F32)] * 2,
        compiler_params=_cparams(("parallel",)),
        name="fourier_channels",
    )(u, cs)

    a3 = a.reshape(B * n2, n1, W)
    b3 = b.reshape(B * n2, n1, W)
    ca, sa = _cos_sin(n2, n2, n2)
    ctw, stw = _cos_sin(n1, n2, S)
    ctw = jnp.broadcast_to(ctw[:, :, None], (n1, n2, LANES))
    stw = jnp.broadcast_to(stw[:, :, None], (n1, n2, LANES))
    tr, ti = pl.pallas_call(
        _fa_kernel,
        grid=(B, n1 // r),
        in_specs=[
            pl.BlockSpec(memory_space=pl.ANY),
            pl.BlockSpec(memory_space=pl.ANY),
            pl.BlockSpec((n2, n2), lambda bb, j: (0, 0)),
            pl.BlockSpec((n2, n2), lambda bb, j: (0, 0)),
            pl.BlockSpec((r, n2, LANES), lambda bb, j: (j, 0, 0)),
            pl.BlockSpec((r, n2, LANES), lambda bb, j: (j, 0, 0)),
        ],
        out_specs=[pl.BlockSpec(memory_space=pl.ANY)] * 2,
        out_shape=[jax.ShapeDtypeStruct((B * n2, n1, W), F32)] * 2,
        scratch_shapes=[pltpu.VMEM((r, n2, W), F32)] * 4 + [pltpu.SemaphoreType.DMA((4, r))],
        compiler_params=_cparams(("parallel", "parallel")),
        name="fourier_stage_a",
    )(a3, b3, ca.astype(BF16), sa.astype(BF16), ctw, stw)

    c1, s1 = _cos_sin(n1, n1, n1)
    f3 = pl.pallas_call(
        functools.partial(_fb_kernel, scale=1.0 / math.sqrt(S * gw)),
        grid=(B, n2 // r),
        in_specs=[
            pl.BlockSpec((r, n1, W), lambda bb, k: (bb * (n2 // r) + k, 0, 0)),
            pl.BlockSpec((r, n1, W), lambda bb, k: (bb * (n2 // r) + k, 0, 0)),
            pl.BlockSpec((n1, n1), lambda bb, k: (0, 0)),
            pl.BlockSpec((n1, n1), lambda bb, k: (0, 0)),
        ],
        out_specs=pl.BlockSpec((n1, r, W), lambda bb, k: (bb, k, 0)),
        out_shape=jax.ShapeDtypeStruct((B * n1, n2, W), F32),
        compiler_params=_cparams(("parallel", "parallel")),
        name="fourier_stage_b",
    )(tr, ti, c1.astype(BF16), s1.astype(BF16))
    return f3.reshape(T, W)


def _merge_kernel(at_ref, f_ref, ga_ref, gf_ref, wap_ref, wfp_ref, o_ref):
    a = _dot(at_ref[...], wap_ref[...])
    fo = _dot(f_ref[...].astype(BF16), wfp_ref[...])
    ga = jax.nn.sigmoid(ga_ref[...].astype(F32))
    gf = jax.nn.sigmoid(gf_ref[...].astype(F32))
    o_ref[...] = (ga * a + gf * fo).astype(o_ref.dtype)


def _merge(attn, f, gates, g_off, w_ap, w_fp, tm=1024, tn=1024):
    T, wa = attn.shape
    wf = f.shape[1]
    D = w_ap.shape[1]
    tm = min(tm, T)
    nj = D // tn
    g0 = g_off // tn
    return pl.pallas_call(
        _merge_kernel,
        grid=(T // tm, nj),
        in_specs=[
            pl.BlockSpec((tm, wa), lambda i, j: (i, 0)),
            pl.BlockSpec((tm, wf), lambda i, j: (i, 0)),
            pl.BlockSpec((tm, tn), lambda i, j: (i, g0 + j)),
            pl.BlockSpec((tm, tn), lambda i, j: (i, g0 + nj + j)),
            pl.BlockSpec((wa, tn), lambda i, j: (0, j)),
            pl.BlockSpec((wf, tn), lambda i, j: (0, j)),
        ],
        out_specs=pl.BlockSpec((tm, tn), lambda i, j: (i, j)),
        out_shape=jax.ShapeDtypeStruct((T, D), BF16),
        compiler_params=_cparams(("parallel", "parallel")),
        name="merge",
    )(attn, f, gates, gates, w_ap, w_fp)


def _outproj_kernel(m_ref, x_ref, w_ref, g_ref, wrt_ref, x1_ref, h_ref, afft_ref):
    x1 = x_ref[...] + _dot(m_ref[...], w_ref[...])
    x1_ref[...] = x1
    h32 = _rms(x1, g_ref[...])
    h_ref[...] = h32
    lt = _dot_nt(wrt_ref[...], h32.astype(BF16))
    et = jnp.exp(lt - jnp.max(lt, axis=0, keepdims=True))
    afft_ref[...] = et / jnp.sum(et, axis=0, keepdims=True)


def _outproj(merged, x, w_out, g, w_router_t, tm=512):
    T, D = x.shape
    E = w_router_t.shape[0]
    tm = min(tm, T)
    return pl.pallas_call(
        _outproj_kernel,
        grid=(T // tm,),
        in_specs=[
            pl.BlockSpec((tm, D), lambda i: (i, 0)),
            pl.BlockSpec((tm, D), lambda i: (i, 0)),
            pl.BlockSpec((D, D), lambda i: (0, 0), pipeline_mode=pl.Buffered(1)),
            pl.BlockSpec((1, D), lambda i: (0, 0)),
            pl.BlockSpec((E, D), lambda i: (0, 0)),
        ],
        out_specs=[
            pl.BlockSpec((tm, D), lambda i: (i, 0)),
            pl.BlockSpec((tm, D), lambda i: (i, 0)),
            pl.BlockSpec((E, tm), lambda i: (0, i)),
        ],
        out_shape=[
            jax.ShapeDtypeStruct((T, D), F32),
            jax.ShapeDtypeStruct((T, D), F32),
            jax.ShapeDtypeStruct((E, T), F32),
        ],
        compiler_params=_cparams(("parallel",)),
        name="outproj_router",
    )(merged, x, w_out, g, w_router_t)


def _route_kernel(aff_ref, q_ref, qt_ref, offs_ref, *, cap):
    E, nb, L = aff_ref.shape
    rows = E * nb
    aff = aff_ref[...]

    def count(mask):
        c = jnp.sum(mask.astype(F32), axis=2, keepdims=True)
        return jnp.sum(c, axis=1, keepdims=True)

    def search(i, prefix):
        cand = prefix | (jnp.int32(1) << (30 - i))
        ge = aff >= lax.bitcast_convert_type(cand, F32)
        return jnp.where(count(ge) >= cap, cand, prefix)

    thr = lax.bitcast_convert_type(lax.fori_loop(0, 31, search, jnp.zeros((E, 1, 1), I32)), F32)

    ri = lax.broadcasted_iota(I32, (L, L), 0)
    ci = lax.broadcasted_iota(I32, (L, L), 1)
    tri = (ri <= ci).astype(BF16)
    ones = jnp.ones((L, L), BF16)
    rr = lax.broadcasted_iota(I32, (rows, rows), 0)
    rc = lax.broadcasted_iota(I32, (rows, rows), 1)
    sh = nb.bit_length() - 1
    same_expert = lax.shift_right_logical(rr, sh) == lax.shift_right_logical(rc, sh)
    earlier = (same_expert & (rc < rr)).astype(BF16)

    def cumsum(mask):
        x = mask.astype(F32).reshape(rows, L).astype(BF16)
        within = _dot(x, tri)
        tot = _dot(x, ones)
        offs = _dot(earlier, tot.astype(BF16))
        return (within + offs).reshape(E, nb, L), offs

    gt = aff > thr
    eq = aff == thr
    need = cap - count(gt)
    sel = gt | (eq & (cumsum(eq)[0] <= need))
    rank, offs = cumsum(sel)
    q_ref[...] = jnp.where(sel, rank - 1.0, -1.0)
    offs_ref[...] = offs

    filler = jnp.full((L - E, L), -1.0, F32)
    for b in range(nb):
        tile = jnp.concatenate([q_ref[:, b, :], filler], axis=0)
        qt_ref[b * L:(b + 1) * L, :] = tile.T


def _route(aff_t, cap):
    E, T = aff_t.shape
    nb = T // LANES
    aff3 = aff_t.reshape(E, nb, LANES)
    return pl.pallas_call(
        functools.partial(_route_kernel, cap=cap),
        out_shape=[
            jax.ShapeDtypeStruct((E, nb, LANES), F32),
            jax.ShapeDtypeStruct((T, LANES), F32),
            jax.ShapeDtypeStruct((E * nb, LANES), F32),
        ],
        compiler_params=pltpu.CompilerParams(vmem_limit_bytes=VMEM_LIMIT),
        name="route",
    )(aff3)


MOE_TOKEN_BLOCK = 256
MOE_WINDOW = 64


def _compact_kernel(cnt_ref, q_ref, aff_ref, idx_ref, gate_ref, *, cap):
    e = pl.program_id(0)
    nbk, kb = q_ref.shape
    L = LANES
    base = e * (nbk + 1)
    slot_col = lax.broadcasted_iota(I32, (L, 1), 0).astype(F32)
    lane_tok = lax.broadcasted_iota(I32, (1, kb), 1).astype(F32)
    b_lo = b_hi = jnp.int32(0)
    for c in range(cap // L):
        lo_slot, hi_slot = c * L, (c + 1) * L
        b_lo = lax.while_loop(lambda b: cnt_ref[base + b + 1] <= lo_slot, lambda b: b + 1, b_lo)
        b_hi = lax.while_loop(lambda b: (b < nbk) & (cnt_ref[base + jnp.minimum(b, nbk)] < hi_slot), lambda b: b + 1, b_hi)

        def body(b, acc, lo_slot=lo_slot):
            ai, ag = acc
            hit = q_ref[pl.ds(b, 1), :] == slot_col + float(lo_slot)
            tok = lane_tok + lax.convert_element_type(b * kb, F32)
            return ai + jnp.where(hit, tok, 0.0), ag + jnp.where(hit, aff_ref[pl.ds(b, 1), :], 0.0)

        zero = jnp.zeros((L, kb), F32)
        ai, ag = lax.fori_loop(b_lo, b_hi, body, (zero, zero))
        folded = functools.reduce(jnp.add, [ai[:, i * L:(i + 1) * L] for i in range(kb // L)])
        idx_ref[:, lo_slot:hi_slot] = jnp.sum(folded.T, axis=0, keepdims=True).astype(I32)
        gate_ref[lo_slot:hi_slot, :] = jnp.sum(ag, axis=1, keepdims=True)


def _compact(cnt, q, aff, cap):
    E, nbk, kb = q.shape
    grid_spec = pltpu.PrefetchScalarGridSpec(
        num_scalar_prefetch=1,
        grid=(E,),
        in_specs=[
            pl.BlockSpec((None, nbk, kb), lambda e, cnt: (e, 0, 0)),
            pl.BlockSpec((None, nbk, kb), lambda e, cnt: (e, 0, 0)),
        ],
        out_specs=[
            pl.BlockSpec((None, 1, cap), lambda e, cnt: (e, 0, 0)),
            pl.BlockSpec((cap, 1), lambda e, cnt: (e, 0)),
        ],
    )
    return pl.pallas_call(
        functools.partial(_compact_kernel, cap=cap),
        grid_spec=grid_spec,
        out_shape=[jax.ShapeDtypeStruct((E, 1, cap), I32), jax.ShapeDtypeStruct((E * cap, 1), F32)],
        compiler_params=_cparams(("parallel",)),
        name="moe_compact",
    )(cnt, q, aff)


def _ffn_kernel(idx_ref, h_hbm, gate_ref, wg_ref, wu_ref, wd_ref, o_ref, xbuf, x_ref, acc_ref, sem, *, cap, n_f):
    e = pl.program_id(0)
    f = pl.program_id(1)
    per_step = cap // n_f

    def fetch_row(expert, r):
        tok = idx_ref[expert * cap + r]
        pltpu.make_async_copy(h_hbm.at[pl.ds(tok, 1), :], xbuf.at[pl.ds(r, 1), :], sem).start()

    @pl.when(f == 0)
    def _():
        @pl.when(e == 0)
        def _():
            lax.fori_loop(0, cap, lambda r, c: (fetch_row(0, r), c)[1], 0, unroll=8)

        pltpu.make_async_copy(h_hbm.at[pl.ds(0, cap), :], xbuf, sem).wait()
        x_ref[...] = xbuf[...].astype(BF16)
        acc_ref[...] = jnp.zeros_like(acc_ref)

    x = x_ref[...]
    hid = jax.nn.silu(_dot(x, wg_ref[...].astype(BF16))) * _dot(x, wu_ref[...].astype(BF16))
    acc_ref[...] += _dot(hid.astype(BF16), wd_ref[...].astype(BF16))

    last_e = e == pl.num_programs(0) - 1
    nxt = jnp.where(last_e, 0, e + 1)
    for r in range(per_step):
        fetch_row(nxt, f * per_step + r)

    @pl.when(f == n_f - 1)
    def _():
        o_ref[:cap, :] = (acc_ref[...] * gate_ref[...]).astype(o_ref.dtype)
        o_ref[cap:, :] = jnp.zeros((o_ref.shape[0] - cap, o_ref.shape[1]), o_ref.dtype)

        @pl.when(last_e)
        def _():
            pltpu.make_async_copy(h_hbm.at[pl.ds(0, cap), :], xbuf, sem).wait()


def _ffn(idx, h, gate, wg, wu, wd, cap, pad, fn=256):
    E, D, Fw = wg.shape
    fn = min(fn, Fw)
    grid_spec = pltpu.PrefetchScalarGridSpec(
        num_scalar_prefetch=1,
        grid=(E, Fw // fn),
        in_specs=[
            pl.BlockSpec(memory_space=pl.ANY),
            pl.BlockSpec((cap, 1), lambda e, f, idx: (e, 0)),
            pl.BlockSpec((None, D, fn), lambda e, f, idx: (e, 0, f)),
            pl.BlockSpec((None, D, fn), lambda e, f, idx: (e, 0, f)),
            pl.BlockSpec((None, fn, D), lambda e, f, idx: (e, f, 0)),
        ],
        out_specs=pl.BlockSpec((cap + pad, D), lambda e, f, idx: (e, 0)),
        scratch_shapes=[
            pltpu.VMEM((cap, D), F32),
            pltpu.VMEM((cap, D), BF16),
            pltpu.VMEM((cap, D), F32),
            pltpu.SemaphoreType.DMA(()),
        ],
    )
    return pl.pallas_call(
        functools.partial(_ffn_kernel, cap=cap, n_f=Fw // fn),
        grid_spec=grid_spec,
        out_shape=jax.ShapeDtypeStruct((E * (cap + pad), D), BF16),
        compiler_params=_cparams(("arbitrary", "arbitrary")),
        name="moe_ffn",
    )(idx, h, gate, wg, wu, wd)


def _combine_kernel(cnt_ref, x_ref, qt_ref, o_hbm, g_ref, y_ref, obuf, acc_ref, sems, *, n_exp, stride, nbk):
    b = pl.program_id(0)
    E = n_exp
    W = obuf.shape[1] // E
    tile = 16

    def first_slots(blk):
        return [lax.shift_left(lax.shift_right_logical(cnt_ref[e * (nbk + 1) + blk], 4), 4) for e in range(E)]

    def copies(starts, buf):
        return [
            pltpu.make_async_copy(
                o_hbm.at[pl.ds(pl.multiple_of(e * stride + starts[e], tile), W), :],
                obuf.at[buf, pl.ds(e * W, W), :],
                sems.at[buf],
            )
            for e in range(E)
        ]

    lane = lax.broadcasted_iota(I32, (1, 2 * W), 1)
    first = lane < W
    j = jnp.where(first, lane, lane - W).astype(F32)

    def scatter(starts, buf):
        seg = []
        for e in range(0, E, 2):
            slot = jnp.where(first, qt_ref[:, e:e + 1], qt_ref[:, e + 1:e + 2])
            want = jnp.where(first, starts[e].astype(F32), starts[e + 1].astype(F32)) + j
            seg.append(jnp.where(slot == want, 1.0, 0.0).astype(BF16))
        acc_ref[...] += _dot(jnp.concatenate(seg, axis=1), obuf[buf])

    lo = first_slots(b)
    hi = [cnt_ref[e * (nbk + 1) + b + 1] for e in range(E)]
    span = functools.reduce(jnp.maximum, [h - l for h, l in zip(hi, lo)])
    n_pass = lax.shift_right_logical(span + (W - 1), W.bit_length() - 1)
    buf = lax.rem(b, 2)

    @pl.when(b == 0)
    def _():
        for cp in copies(lo, 0):
            cp.start()

    for cp in copies(lo, buf):
        cp.wait()

    @pl.when(b + 1 < pl.num_programs(0))
    def _():
        for cp in copies(first_slots(b + 1), 1 - buf):
            cp.start()

    acc_ref[...] = x_ref[...]
    scatter(lo, buf)

    def later_pass(p, carry):
        starts = [l + p * W for l in lo]
        for cp in copies(starts, buf):
            cp.start()
        for cp in copies(starts, buf):
            cp.wait()
        scatter(starts, buf)
        return carry

    lax.fori_loop(1, n_pass, later_pass, 0)
    y_ref[...] = _rms(acc_ref[...], g_ref[...])


def _combine(cnt, x1, qt, o, g, n_exp, stride):
    T, D = x1.shape
    tb = min(MOE_TOKEN_BLOCK, T)
    grid_spec = pltpu.PrefetchScalarGridSpec(
        num_scalar_prefetch=1,
        grid=(T // tb,),
        in_specs=[
            pl.BlockSpec((tb, D), lambda b, cnt: (b, 0)),
            pl.BlockSpec((tb, LANES), lambda b, cnt: (b, 0)),
            pl.BlockSpec(memory_space=pl.ANY),
            pl.BlockSpec((1, D), lambda b, cnt: (0, 0)),
        ],
        out_specs=pl.BlockSpec((tb, D), lambda b, cnt: (b, 0)),
        scratch_shapes=[
            pltpu.VMEM((2, n_exp * MOE_WINDOW, D), BF16),
            pltpu.VMEM((tb, D), F32),
            pltpu.SemaphoreType.DMA((2,)),
        ],
    )
    return pl.pallas_call(
        functools.partial(_combine_kernel, n_exp=n_exp, stride=stride, nbk=T // tb),
        grid_spec=grid_spec,
        out_shape=jax.ShapeDtypeStruct((T, D), F32),
        compiler_params=_cparams(("arbitrary",)),
        name="moe_combine_norm",
    )(cnt, x1, qt, o, g)


def _expert_choice_moe(x1, h2, aff_t, p):
    T, D = x1.shape
    E = aff_t.shape[0]
    cap = CAPACITY_FACTOR * T // E
    q, qt, offs = _route(aff_t, cap)
    nb = T // LANES
    per_blk = min(MOE_TOKEN_BLOCK, T) // LANES
    cnt = offs[:, 0].reshape(E, nb)[:, ::per_blk]
    cnt = jnp.concatenate([cnt, jnp.full((E, 1), cap, F32)], axis=1).astype(I32).reshape(-1)
    blocked = (E, nb // per_blk, per_blk * LANES)
    idx, gate = _compact(cnt, q.reshape(blocked), aff_t.reshape(blocked), cap)
    o = _ffn(idx.reshape(E * cap), h2, gate, p["w_expert_gate"], p["w_expert_up"], p["w_expert_down"], cap, MOE_WINDOW)
    return _combine(cnt, x1, qt, o, p["norm_final_g"], E, cap + MOE_WINDOW)


def _encoder_group(x, p):
    B, S, D = x.shape
    T = B * S
    hd = p["q_norm_g"].shape[-1]
    wq, wk = N_Q_HEADS * hd, N_KV_HEADS * hd
    wu = N_FOURIER_GROUPS * FOURIER_GROUP_W
    xt = x.reshape(T, D)

    proj = _inproj(xt, p["norm_mix_g"], p["w_in"])
    u_off = wq + 2 * wk
    cos_t, sin_t = _rope_tables(S, hd)
    qr, kt, va = _qkrope(proj, wq, wk, p["q_norm_g"], p["k_norm_g"], cos_t, sin_t, S, ts=min(ATTN_KV_CHUNK, S))
    attn = _attention_any(qr, kt, va, p["q_norm_g"], p["k_norm_g"], B, S)
    f = _fourier(proj, u_off // wu, wu, B, S)
    merged = _merge(attn, f, proj, u_off + wu, p["w_attn_proj"], p["w_fourier_proj"])
    x1, h2, aff_t = _outproj(merged, xt, p["w_out"], p["norm_moe_g"], p["w_router_t"])
    return _expert_choice_moe(x1, h2, aff_t, p).reshape(B, S, D)


def kernel(x_prompt, x_sample, norm_mix_g, w_in, q_norm_g, k_norm_g, w_attn_proj, w_fourier_proj, w_out, norm_moe_g, w_router, w_expert_gate, w_expert_up, w_expert_down, norm_final_g):
    p = {
        "norm_mix_g": norm_mix_g[0][None, :],
        "w_in": w_in[0],
        "q_norm_g": q_norm_g[0][None, :],
        "k_norm_g": k_norm_g[0][None, :],
        "w_attn_proj": w_attn_proj[0].astype(BF16),
        "w_fourier_proj": w_fourier_proj[0].astype(BF16),
        "w_out": w_out[0].astype(BF16),
        "norm_moe_g": norm_moe_g[0][None, :],
        "w_router_t": w_router[0].T.astype(BF16),
        "w_expert_gate": w_expert_gate[0],
        "w_expert_up": w_expert_up[0],
        "w_expert_down": w_expert_down[0],
        "norm_final_g": norm_final_g[None, :],
    }
    return (_encoder_group(x_prompt, p), _encoder_group(x_sample, p))
```

```python
import functools
import math

import jax
import jax.numpy as jnp
from jax import lax
from jax.experimental import pallas as pl
from jax.experimental.pallas import tpu as pltpu

F32 = jnp.float32
BF16 = jnp.bfloat16
I32 = jnp.int32

EPS = 1e-6
N_Q_HEADS = 16
N_KV_HEADS = 4
ROPE_GRID_W = 64
ROPE_THETA = 10000.0
N_FOURIER_GROUPS = 4
FOURIER_GROUP_W = 256
CAPACITY_FACTOR = 2
LANES = 128
BF16_ROWS = 16
ATTN_KV_CHUNK = 512
VMEM_LIMIT = 56 * 1024 * 1024


def _cparams(sem):
    return pltpu.CompilerParams(dimension_semantics=sem, vmem_limit_bytes=VMEM_LIMIT)


def _dot(a, b):
    return jnp.dot(a, b, preferred_element_type=F32)


def _dot_nt(a, b):
    return lax.dot_general(a, b, (((1,), (1,)), ((), ())), preferred_element_type=F32)


def _rms(x, g):
    ms = jnp.mean(x * x, axis=-1, keepdims=True)
    return x * lax.rsqrt(ms + EPS) * g


def _inproj_kernel(x_ref, g_ref, w_ref, o_ref, h_ref):
    @pl.when(pl.program_id(1) == 0)
    def _():
        h_ref[...] = _rms(x_ref[...], g_ref[...]).astype(BF16)

    o_ref[...] = _dot(h_ref[...], w_ref[...].astype(BF16)).astype(o_ref.dtype)


def _inproj(x, g, w, tm=1024, tn=1024):
    T, D = x.shape
    N = w.shape[1]
    tm = min(tm, T)
    return pl.pallas_call(
        _inproj_kernel,
        grid=(T // tm, N // tn),
        in_specs=[
            pl.BlockSpec((tm, D), lambda i, j: (i, 0)),
            pl.BlockSpec((1, D), lambda i, j: (0, 0)),
            pl.BlockSpec((D, tn), lambda i, j: (0, j)),
        ],
        out_specs=pl.BlockSpec((tm, tn), lambda i, j: (i, j)),
        out_shape=jax.ShapeDtypeStruct((T, N), BF16),
        scratch_shapes=[pltpu.VMEM((tm, D), BF16)],
        compiler_params=_cparams(("parallel", "arbitrary")),
        name="inproj",
    )(x, g, w)


def _qkrope_kernel(q_ref, k_ref, v_ref, qg_ref, kg_ref, c_ref, s_ref, qo_ref, ko_ref, vo_ref, *, hd, scale):
    c = c_ref[...]
    s = s_ref[...]
    lane = lax.broadcasted_iota(I32, (1, hd), 1)
    low_half = (lane % (hd // 2)) < (hd // 4)

    def one(x, g):
        y = _rms(x, g)
        sw = jnp.where(low_half, pltpu.roll(y, hd - hd // 4, 1), pltpu.roll(y, hd // 4, 1))
        return y * c + sw * s

    for h in range(q_ref.shape[1] // hd):
        sl = slice(h * hd, (h + 1) * hd)
        qo_ref[h] = (one(q_ref[:, sl].astype(F32), qg_ref[...]) * scale).astype(qo_ref.dtype)
    pad = vo_ref.shape[2] - hd
    ones_row = jnp.where(lax.broadcasted_iota(I32, (pad, v_ref.shape[0]), 0) == 0, 1.0, 0.0)
    for h in range(k_ref.shape[1] // hd):
        sl = slice(h * hd, (h + 1) * hd)
        ko_ref[:, sl] = one(k_ref[:, sl].astype(F32), kg_ref[...]).astype(ko_ref.dtype)
        vt = jnp.concatenate([v_ref[:, sl].astype(F32).T, ones_row], axis=0)
        vo_ref[h, 0] = vt.astype(vo_ref.dtype)


def _qkrope(proj, wq, wk, qg, kg, cos_t, sin_t, S, ts):
    T = proj.shape[0]
    hd = qg.shape[-1]
    n_q, n_kv = wq // hd, wk // hd
    ns = S // ts
    k_blk = wq // wk
    return pl.pallas_call(
        functools.partial(_qkrope_kernel, hd=hd, scale=hd ** -0.5 * math.log2(math.e)),
        grid=(T // ts,),
        in_specs=[
            pl.BlockSpec((ts, wq), lambda i: (i, 0)),
            pl.BlockSpec((ts, wk), lambda i: (i, k_blk)),
            pl.BlockSpec((ts, wk), lambda i: (i, k_blk + 1)),
            pl.BlockSpec((1, hd), lambda i: (0, 0)),
            pl.BlockSpec((1, hd), lambda i: (0, 0)),
            pl.BlockSpec((ts, hd), lambda i: (i % ns, 0)),
            pl.BlockSpec((ts, hd), lambda i: (i % ns, 0)),
        ],
        out_specs=[
            pl.BlockSpec((n_q, ts, hd), lambda i: (0, i, 0)),
            pl.BlockSpec((ts, wk), lambda i: (i, 0)),
            pl.BlockSpec((n_kv, 1, hd + BF16_ROWS, ts), lambda i: (0, i, 0, 0)),
        ],
        out_shape=[
            jax.ShapeDtypeStruct((n_q, T, hd), BF16),
            jax.ShapeDtypeStruct((T, wk), BF16),
            jax.ShapeDtypeStruct((n_kv, T // ts, hd + BF16_ROWS, ts), BF16),
        ],
        compiler_params=_cparams(("parallel",)),
        name="qkrope",
    )(proj, proj, proj, qg, kg, cos_t, sin_t)


def _rope_tables(S, hd):
    axis = hd // 2
    pos = jnp.arange(S, dtype=F32)
    row = jnp.floor(pos / ROPE_GRID_W)
    col = pos - row * ROPE_GRID_W
    inv_freq = ROPE_THETA ** (-jnp.arange(0, axis, 2, dtype=F32) / axis)
    ang_r = row[:, None] * inv_freq
    ang_c = col[:, None] * inv_freq
    cr, sr, cc, sc = jnp.cos(ang_r), jnp.sin(ang_r), jnp.cos(ang_c), jnp.sin(ang_c)
    cos_t = jnp.concatenate([cr, cr, cc, cc], axis=-1)
    sin_t = jnp.concatenate([-sr, sr, -sc, sc], axis=-1)
    return cos_t, sin_t


def _attn_load_q(q_ref, qt_ref):
    for h in range(q_ref.shape[0]):
        qt_ref[h] = q_ref[h].astype(F32).T.astype(qt_ref.dtype)


def _attn_finish(acc_ref, o_ref, hd):
    for h in range(acc_ref.shape[0]):
        a = acc_ref[h]
        o_ref[:, h * hd:(h + 1) * hd] = (a[:hd] / a[hd:hd + 1]).T.astype(o_ref.dtype)


def _attn_bounded_kernel(q_ref, k_ref, vt_ref, o_ref, qt_ref, acc_ref, p_ref):
    G, tq, hd = q_ref.shape
    n_chunks, _, tk = vt_ref.shape
    _attn_load_q(q_ref, qt_ref)
    acc_ref[...] = jnp.zeros_like(acc_ref)

    def keys(c):
        return k_ref[pl.ds(pl.multiple_of(c * tk, tk), tk), :]

    k0 = keys(0)
    for h in range(G):
        p_ref[h] = jnp.exp2(_dot(k0, qt_ref[h])).astype(BF16)

    def body(c, carry):
        kc = keys(c)
        vt = vt_ref[c - 1]
        for h in range(G):
            s = _dot(kc, qt_ref[h])
            acc_ref[h] += _dot(vt, p_ref[h])
            p_ref[h] = jnp.exp2(s).astype(BF16)
        return carry

    unroll = max(u for u in range(1, 9) if (n_chunks - 1) % u == 0) if n_chunks > 1 else 1
    lax.fori_loop(1, n_chunks, body, 0, unroll=unroll)
    vt = vt_ref[n_chunks - 1]
    for h in range(G):
        acc_ref[h] += _dot(vt, p_ref[h])
    _attn_finish(acc_ref, o_ref, hd)


def _attn_online_kernel(q_ref, k_ref, vt_ref, o_ref, qt_ref, acc_ref, m_ref):
    G, tq, hd = q_ref.shape
    n_chunks, _, tk = vt_ref.shape
    _attn_load_q(q_ref, qt_ref)
    acc_ref[...] = jnp.zeros_like(acc_ref)
    m_ref[...] = jnp.full_like(m_ref, -jnp.inf)

    def body(c, carry):
        kc = k_ref[pl.ds(pl.multiple_of(c * tk, tk), tk), :]
        vt = vt_ref[c]
        for h in range(G):
            s = _dot(kc, qt_ref[h])
            m = m_ref[h]
            m_new = jnp.maximum(m, jnp.max(s, axis=0, keepdims=True))
            p = jnp.exp2(s - m_new).astype(BF16)
            acc_ref[h] = jnp.exp2(m - m_new) * acc_ref[h] + _dot(vt, p)
            m_ref[h] = m_new
        return carry

    lax.fori_loop(0, n_chunks, body, 0)
    _attn_finish(acc_ref, o_ref, hd)


def _attention(q, k, vt, B, S, bounded, tq=512):
    n_q, T, hd = q.shape
    n_kv, _, vr, tk = vt.shape
    G = n_q // n_kv
    tq = min(tq, S)
    nq = S // tq
    nc = S // tk
    scratch = [pltpu.VMEM((G, hd, tq), BF16), pltpu.VMEM((G, vr, tq), F32)]
    scratch.append(pltpu.VMEM((G, tk, tq), BF16) if bounded else pltpu.VMEM((G, 1, tq), F32))
    return pl.pallas_call(
        _attn_bounded_kernel if bounded else _attn_online_kernel,
        grid=(B, n_kv, nq),
        in_specs=[
            pl.BlockSpec((G, tq, hd), lambda b, g, i: (g, b * nq + i, 0)),
            pl.BlockSpec((S, hd), lambda b, g, i: (b, g)),
            pl.BlockSpec((None, nc, vr, tk), lambda b, g, i: (g, b, 0, 0)),
        ],
        out_specs=pl.BlockSpec((tq, G * hd), lambda b, g, i: (b * nq + i, g)),
        out_shape=jax.ShapeDtypeStruct((T, n_q * hd), BF16),
        scratch_shapes=scratch,
        compiler_params=_cparams(("parallel", "parallel", "parallel")),
        name="attention_bounded" if bounded else "attention_online",
    )(q, k, vt)


MAX_UNSHIFTED_LOG2_SCORE = 100.0


def _attention_any(q, kt, v, qg, kg, B, S):
    hd = qg.shape[-1]
    bound = 1.02 * math.sqrt(hd) * math.log2(math.e) * jnp.max(jnp.abs(qg)) * jnp.max(jnp.abs(kg))
    return lax.cond(
        bound <= MAX_UNSHIFTED_LOG2_SCORE,
        lambda: _attention(q, kt, v, B, S, True),
        lambda: _attention(q, kt, v, B, S, False),
    )


def _dft_split(S):
    lg = int(math.log2(S))
    assert 1 << lg == S
    n2 = 1 << ((lg + 1) // 2)
    return S // n2, n2


def _cos_sin(n, m, period):
    ang = (2.0 * math.pi / period) * ((jnp.arange(n, dtype=I32)[:, None] * jnp.arange(m, dtype=I32)[None, :]) % period).astype(F32)
    return jnp.cos(ang), jnp.sin(ang)


def _f0_kernel(u_ref, cs_ref, a_ref, b_ref, *, gw):
    cs = cs_ref[...]
    for g in range(u_ref.shape[1] // gw):
        sl = slice(g * gw, (g + 1) * gw)
        ab = _dot(u_ref[:, sl], cs)
        a_ref[:, sl] = ab[:, :gw]
        b_ref[:, sl] = ab[:, gw:]


def _fa_kernel(a_hbm, b_hbm, ca_ref, sa_ref, ct_ref, st_ref, tr_hbm, ti_hbm, abuf, bbuf, trbuf, tibuf, sems):
    r, n2, W = abuf.shape
    bb = pl.program_id(0)
    j0 = pl.program_id(1) * r
    rows = pl.ds(bb * n2, n2)

    def loads(jj):
        return (
            pltpu.make_async_copy(a_hbm.at[rows, j0 + jj, :], abuf.at[jj], sems.at[0, jj]),
            pltpu.make_async_copy(b_hbm.at[rows, j0 + jj, :], bbuf.at[jj], sems.at[1, jj]),
        )

    def stores(jj):
        return (
            pltpu.make_async_copy(trbuf.at[jj], tr_hbm.at[rows, j0 + jj, :], sems.at[2, jj]),
            pltpu.make_async_copy(tibuf.at[jj], ti_hbm.at[rows, j0 + jj, :], sems.at[3, jj]),
        )

    for jj in range(r):
        for cp in loads(jj):
            cp.start()
    ca = ca_ref[...]
    sa = sa_ref[...]
    for jj in range(r):
        for cp in loads(jj):
            cp.wait()
        a = abuf[jj].astype(BF16)
        b = bbuf[jj].astype(BF16)
        tr = _dot(ca, a) - _dot(sa, b)
        ti = -(_dot(sa, a) + _dot(ca, b))
        ct = ct_ref[jj]
        st = st_ref[jj]
        trbuf[jj] = tr * ct + ti * st
        tibuf[jj] = ti * ct - tr * st
        for cp in stores(jj):
            cp.start()
    for jj in range(r):
        for cp in stores(jj):
            cp.wait()


def _fb_kernel(tr_ref, ti_ref, c1_ref, s1_ref, f_ref, *, scale):
    c1 = c1_ref[...]
    s1 = s1_ref[...]
    for kk in range(tr_ref.shape[0]):
        xr = tr_ref[kk].astype(BF16)
        xi = ti_ref[kk].astype(BF16)
        f_ref[:, kk, :] = (_dot(c1, xr) + _dot(s1, xi)) * scale


def _fourier(proj, u_blk, W, B, S, ts=512, r=8):
    u = proj
    T = proj.shape[0]
    gw = FOURIER_GROUP_W
    ts = min(ts, T)
    n1, n2 = _dft_split(S)

    cc, sc = _cos_sin(gw, gw, gw)
    cs = jnp.concatenate([cc, sc], axis=1).astype(BF16)
    a, b = pl.pallas_call(
        functools.partial(_f0_kernel, gw=gw),
        grid=(T // ts,),
        in_specs=[pl.BlockSpec((ts, W), lambda i: (i, u_blk)), pl.BlockSpec((gw, 2 * gw), lambda i: (0, 0))],
        out_specs=[pl.BlockSpec((ts, W), lambda i: (i, 0))] * 2,
        out_shape=[jax.ShapeDtypeStruct((T, W), F32)] * 2,
        compiler_params=_cparams(("parallel",)),
        name="fourier_channels",
    )(u, cs)

    a3 = a.reshape(B * n2, n1, W)
    b3 = b.reshape(B * n2, n1, W)
    ca, sa = _cos_sin(n2, n2, n2)
    ctw, stw = _cos_sin(n1, n2, S)
    ctw = ctw[:, :, None]
    stw = stw[:, :, None]
    tr, ti = pl.pallas_call(
        _fa_kernel,
        grid=(B, n1 // r),
        in_specs=[
            pl.BlockSpec(memory_space=pl.ANY),
            pl.BlockSpec(memory_space=pl.ANY),
            pl.BlockSpec((n2, n2), lambda bb, j: (0, 0)),
            pl.BlockSpec((n2, n2), lambda bb, j: (0, 0)),
            pl.BlockSpec((r, n2, 1), lambda bb, j: (j, 0, 0)),
            pl.BlockSpec((r, n2, 1), lambda bb, j: (j, 0, 0)),
        ],
        out_specs=[pl.BlockSpec(memory_space=pl.ANY)] * 2,
        out_shape=[jax.ShapeDtypeStruct((B * n2, n1, W), F32)] * 2,
        scratch_shapes=[pltpu.VMEM((r, n2, W), F32)] * 4 + [pltpu.SemaphoreType.DMA((4, r))],
        compiler_params=_cparams(("parallel", "parallel")),
        name="fourier_stage_a",
    )(a3, b3, ca.astype(BF16), sa.astype(BF16), ctw, stw)

    c1, s1 = _cos_sin(n1, n1, n1)
    f3 = pl.pallas_call(
        functools.partial(_fb_kernel, scale=1.0 / math.sqrt(S * gw)),
        grid=(B, n2 // r),
        in_specs=[
            pl.BlockSpec((r, n1, W), lambda bb, k: (bb * (n2 // r) + k, 0, 0)),
            pl.BlockSpec((r, n1, W), lambda bb, k: (bb * (n2 // r) + k, 0, 0)),
            pl.BlockSpec((n1, n1), lambda bb, k: (0, 0)),
            pl.BlockSpec((n1, n1), lambda bb, k: (0, 0)),
        ],
        out_specs=pl.BlockSpec((n1, r, W), lambda bb, k: (bb, k, 0)),
        out_shape=jax.ShapeDtypeStruct((B * n1, n2, W), F32),
        compiler_params=_cparams(("parallel", "parallel")),
        name="fourier_stage_b",
    )(tr, ti, c1.astype(BF16), s1.astype(BF16))
    return f3.reshape(T, W)


def _merge_kernel(at_ref, f_ref, ga_ref, gf_ref, wap_ref, wfp_ref, o_ref):
    a = _dot(at_ref[...], wap_ref[...])
    fo = _dot(f_ref[...].astype(BF16), wfp_ref[...])
    ga = jax.nn.sigmoid(ga_ref[...].astype(F32))
    gf = jax.nn.sigmoid(gf_ref[...].astype(F32))
    o_ref[...] = (ga * a + gf * fo).astype(o_ref.dtype)


def _merge(attn, f, gates, g_off, w_ap, w_fp, tm=1024, tn=1024):
    T, wa = attn.shape
    wf = f.shape[1]
    D = w_ap.shape[1]
    tm = min(tm, T)
    nj = D // tn
    g0 = g_off // tn
    return pl.pallas_call(
        _merge_kernel,
        grid=(T // tm, nj),
        in_specs=[
            pl.BlockSpec((tm, wa), lambda i, j: (i, 0)),
            pl.BlockSpec((tm, wf), lambda i, j: (i, 0)),
            pl.BlockSpec((tm, tn), lambda i, j: (i, g0 + j)),
            pl.BlockSpec((tm, tn), lambda i, j: (i, g0 + nj + j)),
            pl.BlockSpec((wa, tn), lambda i, j: (0, j)),
            pl.BlockSpec((wf, tn), lambda i, j: (0, j)),
        ],
        out_specs=pl.BlockSpec((tm, tn), lambda i, j: (i, j)),
        out_shape=jax.ShapeDtypeStruct((T, D), BF16),
        compiler_params=_cparams(("parallel", "parallel")),
        name="merge",
    )(attn, f, gates, gates, w_ap, w_fp)


def _outproj_kernel(m_ref, x_ref, w_ref, g_ref, wrt_ref, x1_ref, h_ref, afft_ref):
    x1 = x_ref[...] + _dot(m_ref[...], w_ref[...])
    x1_ref[...] = x1
    h32 = _rms(x1, g_ref[...])
    h_ref[...] = h32
    lt = _dot_nt(wrt_ref[...], h32.astype(BF16))
    et = jnp.exp(lt - jnp.max(lt, axis=0, keepdims=True))
    afft_ref[...] = et / jnp.sum(et, axis=0, keepdims=True)


def _outproj(merged, x, w_out, g, w_router_t, tm=512):
    T, D = x.shape
    E = w_router_t.shape[0]
    tm = min(tm, T)
    return pl.pallas_call(
        _outproj_kernel,
        grid=(T // tm,),
        in_specs=[
            pl.BlockSpec((tm, D), lambda i: (i, 0)),
            pl.BlockSpec((tm, D), lambda i: (i, 0)),
            pl.BlockSpec((D, D), lambda i: (0, 0), pipeline_mode=pl.Buffered(1)),
            pl.BlockSpec((1, D), lambda i: (0, 0)),
            pl.BlockSpec((E, D), lambda i: (0, 0)),
        ],
        out_specs=[
            pl.BlockSpec((tm, D), lambda i: (i, 0)),
            pl.BlockSpec((tm, D), lambda i: (i, 0)),
            pl.BlockSpec((E, tm), lambda i: (0, i)),
        ],
        out_shape=[
            jax.ShapeDtypeStruct((T, D), F32),
            jax.ShapeDtypeStruct((T, D), F32),
            jax.ShapeDtypeStruct((E, T), F32),
        ],
        compiler_params=_cparams(("parallel",)),
        name="outproj_router",
    )(merged, x, w_out, g, w_router_t)


def _route_kernel(aff_ref, q_ref, qt_ref, offs_ref, *, cap):
    E, nb, L = aff_ref.shape
    rows = E * nb
    aff = aff_ref[...]

    def count(mask):
        c = jnp.sum(mask.astype(F32), axis=2, keepdims=True)
        return jnp.sum(c, axis=1, keepdims=True)

    def search(i, prefix):
        cand = prefix | (jnp.int32(1) << (30 - i))
        ge = aff >= lax.bitcast_convert_type(cand, F32)
        return jnp.where(count(ge) >= cap, cand, prefix)

    thr = lax.bitcast_convert_type(lax.fori_loop(0, 31, search, jnp.zeros((E, 1, 1), I32)), F32)

    ri = lax.broadcasted_iota(I32, (L, L), 0)
    ci = lax.broadcasted_iota(I32, (L, L), 1)
    tri = (ri <= ci).astype(BF16)
    ones = jnp.ones((L, L), BF16)
    rr = lax.broadcasted_iota(I32, (rows, rows), 0)
    rc = lax.broadcasted_iota(I32, (rows, rows), 1)
    sh = nb.bit_length() - 1
    same_expert = lax.shift_right_logical(rr, sh) == lax.shift_right_logical(rc, sh)
    earlier = (same_expert & (rc < rr)).astype(BF16)

    def cumsum(mask):
        x = mask.astype(F32).reshape(rows, L).astype(BF16)
        within = _dot(x, tri)
        tot = _dot(x, ones)
        offs = _dot(earlier, tot.astype(BF16))
        return (within + offs).reshape(E, nb, L), offs

    gt = aff > thr
    eq = aff == thr
    need = cap - count(gt)
    sel = gt | (eq & (cumsum(eq)[0] <= need))
    rank, offs = cumsum(sel)
    q_ref[...] = jnp.where(sel, rank - 1.0, -1.0)
    offs_ref[...] = offs

    filler = jnp.full((L - E, L), -1.0, F32)
    for b in range(nb):
        tile = jnp.concatenate([q_ref[:, b, :], filler], axis=0)
        qt_ref[b * L:(b + 1) * L, :] = tile.T


def _route(aff_t, cap):
    E, T = aff_t.shape
    nb = T // LANES
    aff3 = aff_t.reshape(E, nb, LANES)
    return pl.pallas_call(
        functools.partial(_route_kernel, cap=cap),
        out_shape=[
            jax.ShapeDtypeStruct((E, nb, LANES), F32),
            jax.ShapeDtypeStruct((T, LANES), F32),
            jax.ShapeDtypeStruct((E * nb, LANES), F32),
        ],
        compiler_params=pltpu.CompilerParams(vmem_limit_bytes=VMEM_LIMIT),
        name="route",
    )(aff3)


MOE_TOKEN_BLOCK = 256
MOE_WINDOW = 64


def _compact_kernel(cnt_ref, q_ref, aff_ref, idx_ref, gate_ref, *, cap):
    e = pl.program_id(0)
    nbk, kb = q_ref.shape
    L = LANES
    base = e * (nbk + 1)
    slot_col = lax.broadcasted_iota(I32, (L, 1), 0).astype(F32)
    lane_tok = lax.broadcasted_iota(I32, (1, kb), 1).astype(F32)
    b_lo = b_hi = jnp.int32(0)
    for c in range(cap // L):
        lo_slot, hi_slot = c * L, (c + 1) * L
        b_lo = lax.while_loop(lambda b: cnt_ref[base + b + 1] <= lo_slot, lambda b: b + 1, b_lo)
        b_hi = lax.while_loop(lambda b: (b < nbk) & (cnt_ref[base + jnp.minimum(b, nbk)] < hi_slot), lambda b: b + 1, b_hi)

        def body(b, acc, lo_slot=lo_slot):
            ai, ag = acc
            hit = q_ref[pl.ds(b, 1), :] == slot_col + float(lo_slot)
            tok = lane_tok + lax.convert_element_type(b * kb, F32)
            return ai + jnp.where(hit, tok, 0.0), ag + jnp.where(hit, aff_ref[pl.ds(b, 1), :], 0.0)

        zero = jnp.zeros((L, kb), F32)
        ai, ag = lax.fori_loop(b_lo, b_hi, body, (zero, zero))
        folded = functools.reduce(jnp.add, [ai[:, i * L:(i + 1) * L] for i in range(kb // L)])
        idx_ref[:, lo_slot:hi_slot] = jnp.sum(folded.T, axis=0, keepdims=True).astype(I32)
        gate_ref[lo_slot:hi_slot, :] = jnp.sum(ag, axis=1, keepdims=True)


def _compact(cnt, q, aff, cap):
    E, nbk, kb = q.shape
    grid_spec = pltpu.PrefetchScalarGridSpec(
        num_scalar_prefetch=1,
        grid=(E,),
        in_specs=[
            pl.BlockSpec((None, nbk, kb), lambda e, cnt: (e, 0, 0)),
            pl.BlockSpec((None, nbk, kb), lambda e, cnt: (e, 0, 0)),
        ],
        out_specs=[
            pl.BlockSpec((None, 1, cap), lambda e, cnt: (e, 0, 0)),
            pl.BlockSpec((cap, 1), lambda e, cnt: (e, 0)),
        ],
    )
    return pl.pallas_call(
        functools.partial(_compact_kernel, cap=cap),
        grid_spec=grid_spec,
        out_shape=[jax.ShapeDtypeStruct((E, 1, cap), I32), jax.ShapeDtypeStruct((E * cap, 1), F32)],
        compiler_params=_cparams(("parallel",)),
        name="moe_compact",
    )(cnt, q, aff)


def _ffn_kernel(idx_ref, h_hbm, gate_ref, wg_ref, wu_ref, wd_ref, o_ref, xbuf, x_ref, acc_ref, sem, *, cap, n_f):
    e = pl.program_id(0)
    f = pl.program_id(1)
    per_step = cap // n_f

    def fetch_row(expert, r):
        tok = idx_ref[expert * cap + r]
        pltpu.make_async_copy(h_hbm.at[pl.ds(tok, 1), :], xbuf.at[pl.ds(r, 1), :], sem).start()

    @pl.when(f == 0)
    def _():
        @pl.when(e == 0)
        def _():
            lax.fori_loop(0, cap, lambda r, c: (fetch_row(0, r), c)[1], 0, unroll=8)

        pltpu.make_async_copy(h_hbm.at[pl.ds(0, cap), :], xbuf, sem).wait()
        x_ref[...] = xbuf[...].astype(BF16)
        acc_ref[...] = jnp.zeros_like(acc_ref)

    x = x_ref[...]
    hid = jax.nn.silu(_dot(x, wg_ref[...].astype(BF16))) * _dot(x, wu_ref[...].astype(BF16))
    acc_ref[...] += _dot(hid.astype(BF16), wd_ref[...].astype(BF16))

    last_e = e == pl.num_programs(0) - 1
    nxt = jnp.where(last_e, 0, e + 1)
    for r in range(per_step):
        fetch_row(nxt, f * per_step + r)

    @pl.when(f == n_f - 1)
    def _():
        o_ref[:cap, :] = (acc_ref[...] * gate_ref[...]).astype(o_ref.dtype)
        o_ref[cap:, :] = jnp.zeros((o_ref.shape[0] - cap, o_ref.shape[1]), o_ref.dtype)

        @pl.when(last_e)
        def _():
            pltpu.make_async_copy(h_hbm.at[pl.ds(0, cap), :], xbuf, sem).wait()


def _ffn(idx, h, gate, wg, wu, wd, cap, pad, fn=256):
    E, D, Fw = wg.shape
    fn = min(fn, Fw)
    grid_spec = pltpu.PrefetchScalarGridSpec(
        num_scalar_prefetch=1,
        grid=(E, Fw // fn),
        in_specs=[
            pl.BlockSpec(memory_space=pl.ANY),
            pl.BlockSpec((cap, 1), lambda e, f, idx: (e, 0)),
            pl.BlockSpec((None, D, fn), lambda e, f, idx: (e, 0, f)),
            pl.BlockSpec((None, D, fn), lambda e, f, idx: (e, 0, f)),
            pl.BlockSpec((None, fn, D), lambda e, f, idx: (e, f, 0)),
        ],
        out_specs=pl.BlockSpec((cap + pad, D), lambda e, f, idx: (e, 0)),
        scratch_shapes=[
            pltpu.VMEM((cap, D), F32),
            pltpu.VMEM((cap, D), BF16),
            pltpu.VMEM((cap, D), F32),
            pltpu.SemaphoreType.DMA(()),
        ],
    )
    return pl.pallas_call(
        functools.partial(_ffn_kernel, cap=cap, n_f=Fw // fn),
        grid_spec=grid_spec,
        out_shape=jax.ShapeDtypeStruct((E * (cap + pad), D), BF16),
        compiler_params=_cparams(("arbitrary", "arbitrary")),
        name="moe_ffn",
    )(idx, h, gate, wg, wu, wd)


def _combine_kernel(cnt_ref, x_ref, qt_ref, o_hbm, g_ref, y_ref, obuf, acc_ref, sems, *, n_exp, stride, nbk):
    b = pl.program_id(0)
    E = n_exp
    W = obuf.shape[1] // E
    tile = 16

    def first_slots(blk):
        return [lax.shift_left(lax.shift_right_logical(cnt_ref[e * (nbk + 1) + blk], 4), 4) for e in range(E)]

    def copies(starts, buf):
        return [
            pltpu.make_async_copy(
                o_hbm.at[pl.ds(pl.multiple_of(e * stride + starts[e], tile), W), :],
                obuf.at[buf, pl.ds(e * W, W), :],
                sems.at[buf],
            )
            for e in range(E)
        ]

    lane = lax.broadcasted_iota(I32, (1, 2 * W), 1)
    first = lane < W
    j = jnp.where(first, lane, lane - W).astype(F32)

    def scatter(starts, buf):
        seg = []
        for e in range(0, E, 2):
            slot = jnp.where(first, qt_ref[:, e:e + 1], qt_ref[:, e + 1:e + 2])
            want = jnp.where(first, starts[e].astype(F32), starts[e + 1].astype(F32)) + j
            seg.append(jnp.where(slot == want, 1.0, 0.0).astype(BF16))
        acc_ref[...] += _dot(jnp.concatenate(seg, axis=1), obuf[buf])

    lo = first_slots(b)
    hi = [cnt_ref[e * (nbk + 1) + b + 1] for e in range(E)]
    span = functools.reduce(jnp.maximum, [h - l for h, l in zip(hi, lo)])
    n_pass = lax.shift_right_logical(span + (W - 1), W.bit_length() - 1)
    buf = lax.rem(b, 2)

    @pl.when(b == 0)
    def _():
        for cp in copies(lo, 0):
            cp.start()

    for cp in copies(lo, buf):
        cp.wait()

    @pl.when(b + 1 < pl.num_programs(0))
    def _():
        for cp in copies(first_slots(b + 1), 1 - buf):
            cp.start()

    acc_ref[...] = x_ref[...]
    scatter(lo, buf)

    def later_pass(p, carry):
        starts = [l + p * W for l in lo]
        for cp in copies(starts, buf):
            cp.start()
        for cp in copies(starts, buf):
            cp.wait()
        scatter(starts, buf)
        return carry

    lax.fori_loop(1, n_pass, later_pass, 0)
    y_ref[...] = _rms(acc_ref[...], g_ref[...])


def _combine(cnt, x1, qt, o, g, n_exp, stride):
    T, D = x1.shape
    tb = min(MOE_TOKEN_BLOCK, T)
    grid_spec = pltpu.PrefetchScalarGridSpec(
        num_scalar_prefetch=1,
        grid=(T // tb,),
        in_specs=[
            pl.BlockSpec((tb, D), lambda b, cnt: (b, 0)),
            pl.BlockSpec((tb, LANES), lambda b, cnt: (b, 0)),
            pl.BlockSpec(memory_space=pl.ANY),
            pl.BlockSpec((1, D), lambda b, cnt: (0, 0)),
        ],
        out_specs=pl.BlockSpec((tb, D), lambda b, cnt: (b, 0)),
        scratch_shapes=[
            pltpu.VMEM((2, n_exp * MOE_WINDOW, D), BF16),
            pltpu.VMEM((tb, D), F32),
            pltpu.SemaphoreType.DMA((2,)),
        ],
    )
    return pl.pallas_call(
        functools.partial(_combine_kernel, n_exp=n_exp, stride=stride, nbk=T // tb),
        grid_spec=grid_spec,
        out_shape=jax.ShapeDtypeStruct((T, D), F32),
        compiler_params=_cparams(("arbitrary",)),
        name="moe_combine_norm",
    )(cnt, x1, qt, o, g)


def _expert_choice_moe(x1, h2, aff_t, p):
    T, D = x1.shape
    E = aff_t.shape[0]
    cap = CAPACITY_FACTOR * T // E
    q, qt, offs = _route(aff_t, cap)
    nb = T // LANES
    per_blk = min(MOE_TOKEN_BLOCK, T) // LANES
    cnt = offs[:, 0].reshape(E, nb)[:, ::per_blk]
    cnt = jnp.concatenate([cnt, jnp.full((E, 1), cap, F32)], axis=1).astype(I32).reshape(-1)
    blocked = (E, nb // per_blk, per_blk * LANES)
    idx, gate = _compact(cnt, q.reshape(blocked), aff_t.reshape(blocked), cap)
    o = _ffn(idx.reshape(E * cap), h2, gate, p["w_expert_gate"], p["w_expert_up"], p["w_expert_down"], cap, MOE_WINDOW)
    return _combine(cnt, x1, qt, o, p["norm_final_g"], E, cap + MOE_WINDOW)


def _encoder_group(x, p):
    B, S, D = x.shape
    T = B * S
    hd = p["q_norm_g"].shape[-1]
    wq, wk = N_Q_HEADS * hd, N_KV_HEADS * hd
    wu = N_FOURIER_GROUPS * FOURIER_GROUP_W
    xt = x.reshape(T, D)

    proj = _inproj(xt, p["norm_mix_g"], p["w_in"])
    u_off = wq + 2 * wk
    cos_t, sin_t = _rope_tables(S, hd)
    qr, kt, va = _qkrope(proj, wq, wk, p["q_norm_g"], p["k_norm_g"], cos_t, sin_t, S, ts=min(ATTN_KV_CHUNK, S))
    attn = _attention_any(qr, kt, va, p["q_norm_g"], p["k_norm_g"], B, S)
    f = _fourier(proj, u_off // wu, wu, B, S)
    merged = _merge(attn, f, proj, u_off + wu, p["w_attn_proj"], p["w_fourier_proj"])
    x1, h2, aff_t = _outproj(merged, xt, p["w_out"], p["norm_moe_g"], p["w_router_t"])
    return _expert_choice_moe(x1, h2, aff_t, p).reshape(B, S, D)


def kernel(x_prompt, x_sample, norm_mix_g, w_in, q_norm_g, k_norm_g, w_attn_proj, w_fourier_proj, w_out, norm_moe_g, w_router, w_expert_gate, w_expert_up, w_expert_down, norm_final_g):
    p = {
        "norm_mix_g": norm_mix_g[0][None, :],
        "w_in": w_in[0],
        "q_norm_g": q_norm_g[0][None, :],
        "k_norm_g": k_norm_g[0][None, :],
        "w_attn_proj": w_attn_proj[0].astype(BF16),
        "w_fourier_proj": w_fourier_proj[0].astype(BF16),
        "w_out": w_out[0].astype(BF16),
        "norm_moe_g": norm_moe_g[0][None, :],
        "w_router_t": w_router[0].T.astype(BF16),
        "w_expert_gate": w_expert_gate[0],
        "w_expert_up": w_expert_up[0],
        "w_expert_down": w_expert_down[0],
        "norm_final_g": norm_final_g[None, :],
    }
    return (_encoder_group(x_prompt, p), _encoder_group(x_sample, p))
```

```python
import functools
import math

import jax
import jax.numpy as jnp
from jax import lax
from jax.experimental import pallas as pl
from jax.experimental.pallas import tpu as pltpu

F32 = jnp.float32
BF16 = jnp.bfloat16
I32 = jnp.int32

EPS = 1e-6
N_Q_HEADS = 16
N_KV_HEADS = 4
ROPE_GRID_W = 64
ROPE_THETA = 10000.0
N_FOURIER_GROUPS = 4
FOURIER_GROUP_W = 256
CAPACITY_FACTOR = 2
LANES = 128
BF16_ROWS = 16
ATTN_KV_CHUNK = 512
VMEM_LIMIT = 56 * 1024 * 1024


def _cparams(sem):
    return pltpu.CompilerParams(dimension_semantics=sem, vmem_limit_bytes=VMEM_LIMIT)


def _dot(a, b):
    return jnp.dot(a, b, preferred_element_type=F32)


def _dot_nt(a, b):
    return lax.dot_general(a, b, (((1,), (1,)), ((), ())), preferred_element_type=F32)


def _rms(x, g):
    ms = jnp.mean(x * x, axis=-1, keepdims=True)
    return x * lax.rsqrt(ms + EPS) * g


def _inproj_kernel(x_ref, g_ref, w_ref, o_ref, h_ref):
    @pl.when(pl.program_id(1) == 0)
    def _():
        h_ref[...] = _rms(x_ref[...], g_ref[...]).astype(BF16)

    o_ref[...] = _dot(h_ref[...], w_ref[...].astype(BF16)).astype(o_ref.dtype)


def _inproj(x, g, w, tm=1024, tn=1024):
    T, D = x.shape
    N = w.shape[1]
    tm = min(tm, T)
    return pl.pallas_call(
        _inproj_kernel,
        grid=(T // tm, N // tn),
        in_specs=[
            pl.BlockSpec((tm, D), lambda i, j: (i, 0)),
            pl.BlockSpec((1, D), lambda i, j: (0, 0)),
            pl.BlockSpec((D, tn), lambda i, j: (0, j)),
        ],
        out_specs=pl.BlockSpec((tm, tn), lambda i, j: (i, j)),
        out_shape=jax.ShapeDtypeStruct((T, N), BF16),
        scratch_shapes=[pltpu.VMEM((tm, D), BF16)],
        compiler_params=_cparams(("parallel", "arbitrary")),
        name="inproj",
    )(x, g, w)


def _qkrope_kernel(q_ref, k_ref, v_ref, qg_ref, kg_ref, c_ref, s_ref, ct_ref, st_ref, qo_ref, ko_ref, vo_ref, *, hd, scale):
    c = c_ref[...]
    s = s_ref[...]
    lane = lax.broadcasted_iota(I32, (1, hd), 1)
    low_half = (lane % (hd // 2)) < (hd // 4)

    def one(x, g):
        y = _rms(x, g)
        sw = jnp.where(low_half, pltpu.roll(y, hd - hd // 4, 1), pltpu.roll(y, hd // 4, 1))
        return y * c + sw * s

    ct = ct_ref[...]
    st = st_ref[...]
    qg = qg_ref[...]
    quarter = hd // 4
    for h in range(q_ref.shape[1] // hd):
        xt = q_ref[:, h * hd:(h + 1) * hd].astype(F32).T
        ms = jnp.mean(xt * xt, axis=0, keepdims=True)
        y = xt * lax.rsqrt(ms + EPS) * qg
        sw = jnp.concatenate([y[quarter:2 * quarter], y[:quarter], y[3 * quarter:], y[2 * quarter:3 * quarter]], axis=0)
        qo_ref[h] = ((y * ct + sw * st) * scale).astype(qo_ref.dtype)
    pad = vo_ref.shape[2] - hd
    ones_row = jnp.where(lax.broadcasted_iota(I32, (pad, v_ref.shape[0]), 0) == 0, 1.0, 0.0)
    for h in range(k_ref.shape[1] // hd):
        sl = slice(h * hd, (h + 1) * hd)
        ko_ref[:, sl] = one(k_ref[:, sl].astype(F32), kg_ref[...]).astype(ko_ref.dtype)
        vt = jnp.concatenate([v_ref[:, sl].astype(F32).T, ones_row], axis=0)
        vo_ref[h, 0] = vt.astype(vo_ref.dtype)


def _qkrope(proj, wq, wk, qg, kg, cos_t, sin_t, S, ts):
    T = proj.shape[0]
    hd = qg.shape[-1]
    n_q, n_kv = wq // hd, wk // hd
    ns = S // ts
    k_blk = wq // wk
    return pl.pallas_call(
        functools.partial(_qkrope_kernel, hd=hd, scale=hd ** -0.5 * math.log2(math.e)),
        grid=(T // ts,),
        in_specs=[
            pl.BlockSpec((ts, wq), lambda i: (i, 0)),
            pl.BlockSpec((ts, wk), lambda i: (i, k_blk)),
            pl.BlockSpec((ts, wk), lambda i: (i, k_blk + 1)),
            pl.BlockSpec((hd, 1), lambda i: (0, 0)),
            pl.BlockSpec((1, hd), lambda i: (0, 0)),
            pl.BlockSpec((ts, hd), lambda i: (i % ns, 0)),
            pl.BlockSpec((ts, hd), lambda i: (i % ns, 0)),
            pl.BlockSpec((hd, ts), lambda i: (0, i % ns)),
            pl.BlockSpec((hd, ts), lambda i: (0, i % ns)),
        ],
        out_specs=[
            pl.BlockSpec((n_q, hd, ts), lambda i: (0, 0, i)),
            pl.BlockSpec((ts, wk), lambda i: (i, 0)),
            pl.BlockSpec((n_kv, 1, hd + BF16_ROWS, ts), lambda i: (0, i, 0, 0)),
        ],
        out_shape=[
            jax.ShapeDtypeStruct((n_q, hd, T), BF16),
            jax.ShapeDtypeStruct((T, wk), BF16),
            jax.ShapeDtypeStruct((n_kv, T // ts, hd + BF16_ROWS, ts), BF16),
        ],
        compiler_params=_cparams(("parallel",)),
        name="qkrope",
    )(proj, proj, proj, qg.reshape(hd, 1), kg, cos_t, sin_t, cos_t.T, sin_t.T)


def _rope_tables(S, hd):
    axis = hd // 2
    rows = S // ROPE_GRID_W
    inv_freq = ROPE_THETA ** (-jnp.arange(0, axis, 2, dtype=F32) / axis)
    ang_r = jnp.arange(rows, dtype=F32)[:, None] * inv_freq
    ang_c = jnp.arange(ROPE_GRID_W, dtype=F32)[:, None] * inv_freq
    cr, sr = (jnp.repeat(t, ROPE_GRID_W, axis=0) for t in (jnp.cos(ang_r), jnp.sin(ang_r)))
    cc, sc = (jnp.tile(t, (rows, 1)) for t in (jnp.cos(ang_c), jnp.sin(ang_c)))
    cos_t = jnp.concatenate([cr, cr, cc, cc], axis=-1)
    sin_t = jnp.concatenate([-sr, sr, -sc, sc], axis=-1)
    return cos_t, sin_t


def _attn_finish(acc_ref, o_ref, hd):
    for h in range(acc_ref.shape[0]):
        a = acc_ref[h]
        o_ref[:, h * hd:(h + 1) * hd] = (a[:hd] / a[hd:hd + 1]).T.astype(o_ref.dtype)


def _attn_bounded_kernel(qt_ref, k_ref, vt_ref, o_ref, acc_ref, p_ref):
    G, hd, tq = qt_ref.shape
    n_chunks, _, tk = vt_ref.shape
    acc_ref[...] = jnp.zeros_like(acc_ref)

    def keys(c):
        return k_ref[pl.ds(pl.multiple_of(c * tk, tk), tk), :]

    k0 = keys(0)
    for h in range(G):
        p_ref[h] = jnp.exp2(_dot(k0, qt_ref[h])).astype(BF16)

    def body(c, carry):
        kc = keys(c)
        vt = vt_ref[c - 1]
        for h in range(G):
            s = _dot(kc, qt_ref[h])
            acc_ref[h] += _dot(vt, p_ref[h])
            p_ref[h] = jnp.exp2(s).astype(BF16)
        return carry

    unroll = max(u for u in range(1, 9) if (n_chunks - 1) % u == 0) if n_chunks > 1 else 1
    lax.fori_loop(1, n_chunks, body, 0, unroll=unroll)
    vt = vt_ref[n_chunks - 1]
    for h in range(G):
        acc_ref[h] += _dot(vt, p_ref[h])
    _attn_finish(acc_ref, o_ref, hd)


def _attn_online_kernel(qt_ref, k_ref, vt_ref, o_ref, acc_ref, m_ref):
    G, hd, tq = qt_ref.shape
    n_chunks, _, tk = vt_ref.shape
    acc_ref[...] = jnp.zeros_like(acc_ref)
    m_ref[...] = jnp.full_like(m_ref, -jnp.inf)

    def body(c, carry):
        kc = k_ref[pl.ds(pl.multiple_of(c * tk, tk), tk), :]
        vt = vt_ref[c]
        for h in range(G):
            s = _dot(kc, qt_ref[h])
            m = m_ref[h]
            m_new = jnp.maximum(m, jnp.max(s, axis=0, keepdims=True))
            p = jnp.exp2(s - m_new).astype(BF16)
            acc_ref[h] = jnp.exp2(m - m_new) * acc_ref[h] + _dot(vt, p)
            m_ref[h] = m_new
        return carry

    lax.fori_loop(0, n_chunks, body, 0)
    _attn_finish(acc_ref, o_ref, hd)


def _attention(q, k, vt, B, S, bounded, tq=512):
    n_q, hd, T = q.shape
    n_kv, _, vr, tk = vt.shape
    G = n_q // n_kv
    tq = min(tq, S)
    nq = S // tq
    nc = S // tk
    scratch = [pltpu.VMEM((G, vr, tq), F32)]
    scratch.append(pltpu.VMEM((G, tk, tq), BF16) if bounded else pltpu.VMEM((G, 1, tq), F32))
    return pl.pallas_call(
        _attn_bounded_kernel if bounded else _attn_online_kernel,
        grid=(B, n_kv, nq),
        in_specs=[
            pl.BlockSpec((G, hd, tq), lambda b, g, i: (g, 0, b * nq + i)),
            pl.BlockSpec((S, hd), lambda b, g, i: (b, g)),
            pl.BlockSpec((None, nc, vr, tk), lambda b, g, i: (g, b, 0, 0)),
        ],
        out_specs=pl.BlockSpec((tq, G * hd), lambda b, g, i: (b * nq + i, g)),
        out_shape=jax.ShapeDtypeStruct((T, n_q * hd), BF16),
        scratch_shapes=scratch,
        compiler_params=_cparams(("parallel", "parallel", "parallel")),
        name="attention_bounded" if bounded else "attention_online",
    )(q, k, vt)


MAX_UNSHIFTED_LOG2_SCORE = 100.0


def _attention_any(q, kt, v, qg, kg, B, S):
    hd = qg.shape[-1]
    bound = 1.02 * math.sqrt(hd) * math.log2(math.e) * jnp.max(jnp.abs(qg)) * jnp.max(jnp.abs(kg))
    return lax.cond(
        bound <= MAX_UNSHIFTED_LOG2_SCORE,
        lambda: _attention(q, kt, v, B, S, True),
        lambda: _attention(q, kt, v, B, S, False),
    )


def _dft_split(S):
    lg = int(math.log2(S))
    assert 1 << lg == S
    n2 = 1 << ((lg + 1) // 2)
    return S // n2, n2


def _cos_sin(n, m, period):
    ang = (2.0 * math.pi / period) * ((jnp.arange(n, dtype=I32)[:, None] * jnp.arange(m, dtype=I32)[None, :]) % period).astype(F32)
    return jnp.cos(ang), jnp.sin(ang)


def _f0_kernel(u_ref, cs_ref, a_ref, b_ref, *, gw):
    cs = cs_ref[...]
    for g in range(u_ref.shape[1] // gw):
        sl = slice(g * gw, (g + 1) * gw)
        ab = _dot(u_ref[:, sl], cs)
        a_ref[:, sl] = ab[:, :gw]
        b_ref[:, sl] = ab[:, gw:]


def _fa_kernel(a_hbm, b_hbm, ca_ref, sa_ref, ct_ref, st_ref, tr_hbm, ti_hbm, abuf, bbuf, trbuf, tibuf, sems):
    r, n2, W = abuf.shape
    bb = pl.program_id(0)
    j0 = pl.program_id(1) * r
    rows = pl.ds(bb * n2, n2)

    def loads(jj):
        return (
            pltpu.make_async_copy(a_hbm.at[rows, j0 + jj, :], abuf.at[jj], sems.at[0, jj]),
            pltpu.make_async_copy(b_hbm.at[rows, j0 + jj, :], bbuf.at[jj], sems.at[1, jj]),
        )

    def stores(jj):
        return (
            pltpu.make_async_copy(trbuf.at[jj], tr_hbm.at[rows, j0 + jj, :], sems.at[2, jj]),
            pltpu.make_async_copy(tibuf.at[jj], ti_hbm.at[rows, j0 + jj, :], sems.at[3, jj]),
        )

    for jj in range(r):
        for cp in loads(jj):
            cp.start()
    ca = ca_ref[...]
    sa = sa_ref[...]
    for jj in range(r):
        for cp in loads(jj):
            cp.wait()
        a = abuf[jj].astype(BF16)
        b = bbuf[jj].astype(BF16)
        tr = _dot(ca, a) - _dot(sa, b)
        ti = -(_dot(sa, a) + _dot(ca, b))
        ct = ct_ref[jj]
        st = st_ref[jj]
        trbuf[jj] = tr * ct + ti * st
        tibuf[jj] = ti * ct - tr * st
        for cp in stores(jj):
            cp.start()
    for jj in range(r):
        for cp in stores(jj):
            cp.wait()


def _fb_kernel(tr_ref, ti_ref, c1_ref, s1_ref, f_ref, *, scale):
    c1 = c1_ref[...]
    s1 = s1_ref[...]
    for kk in range(tr_ref.shape[0]):
        xr = tr_ref[kk].astype(BF16)
        xi = ti_ref[kk].astype(BF16)
        f_ref[:, kk, :] = (_dot(c1, xr) + _dot(s1, xi)) * scale


def _fourier(proj, u_blk, W, B, S, ts=512, r=8):
    u = proj
    T = proj.shape[0]
    gw = FOURIER_GROUP_W
    ts = min(ts, T)
    n1, n2 = _dft_split(S)

    cc, sc = _cos_sin(gw, gw, gw)
    cs = jnp.concatenate([cc, sc], axis=1).astype(BF16)
    a, b = pl.pallas_call(
        functools.partial(_f0_kernel, gw=gw),
        grid=(T // ts,),
        in_specs=[pl.BlockSpec((ts, W), lambda i: (i, u_blk)), pl.BlockSpec((gw, 2 * gw), lambda i: (0, 0))],
        out_specs=[pl.BlockSpec((ts, W), lambda i: (i, 0))] * 2,
        out_shape=[jax.ShapeDtypeStruct((T, W), F32)] * 2,
        compiler_params=_cparams(("parallel",)),
        name="fourier_channels",
    )(u, cs)

    a3 = a.reshape(B * n2, n1, W)
    b3 = b.reshape(B * n2, n1, W)
    ca, sa = _cos_sin(n2, n2, n2)
    ctw, stw = _cos_sin(n1, n2, S)
    ctw = ctw[:, :, None]
    stw = stw[:, :, None]
    tr, ti = pl.pallas_call(
        _fa_kernel,
        grid=(B, n1 // r),
        in_specs=[
            pl.BlockSpec(memory_space=pl.ANY),
            pl.BlockSpec(memory_space=pl.ANY),
            pl.BlockSpec((n2, n2), lambda bb, j: (0, 0)),
            pl.BlockSpec((n2, n2), lambda bb, j: (0, 0)),
            pl.BlockSpec((r, n2, 1), lambda bb, j: (j, 0, 0)),
            pl.BlockSpec((r, n2, 1), lambda bb, j: (j, 0, 0)),
        ],
        out_specs=[pl.BlockSpec(memory_space=pl.ANY)] * 2,
        out_shape=[jax.ShapeDtypeStruct((B * n2, n1, W), F32)] * 2,
        scratch_shapes=[pltpu.VMEM((r, n2, W), F32)] * 4 + [pltpu.SemaphoreType.DMA((4, r))],
        compiler_params=_cparams(("parallel", "parallel")),
        name="fourier_stage_a",
    )(a3, b3, ca.astype(BF16), sa.astype(BF16), ctw, stw)

    c1, s1 = _cos_sin(n1, n1, n1)
    f3 = pl.pallas_call(
        functools.partial(_fb_kernel, scale=1.0 / math.sqrt(S * gw)),
        grid=(B, n2 // r),
        in_specs=[
            pl.BlockSpec((r, n1, W), lambda bb, k: (bb * (n2 // r) + k, 0, 0)),
            pl.BlockSpec((r, n1, W), lambda bb, k: (bb * (n2 // r) + k, 0, 0)),
            pl.BlockSpec((n1, n1), lambda bb, k: (0, 0)),
            pl.BlockSpec((n1, n1), lambda bb, k: (0, 0)),
        ],
        out_specs=pl.BlockSpec((n1, r, W), lambda bb, k: (bb, k, 0)),
        out_shape=jax.ShapeDtypeStruct((B * n1, n2, W), F32),
        compiler_params=_cparams(("parallel", "parallel")),
        name="fourier_stage_b",
    )(tr, ti, c1.astype(BF16), s1.astype(BF16))
    return f3.reshape(T, W)


def _merge_kernel(at_ref, f_ref, ga_ref, gf_ref, wap_ref, wfp_ref, o_ref):
    a = _dot(at_ref[...], wap_ref[...])
    fo = _dot(f_ref[...].astype(BF16), wfp_ref[...])
    ga = jax.nn.sigmoid(ga_ref[...].astype(F32))
    gf = jax.nn.sigmoid(gf_ref[...].astype(F32))
    o_ref[...] = (ga * a + gf * fo).astype(o_ref.dtype)


def _merge(attn, f, gates, g_off, w_ap, w_fp, tm=1024, tn=1024):
    T, wa = attn.shape
    wf = f.shape[1]
    D = w_ap.shape[1]
    tm = min(tm, T)
    nj = D // tn
    g0 = g_off // tn
    return pl.pallas_call(
        _merge_kernel,
        grid=(T // tm, nj),
        in_specs=[
            pl.BlockSpec((tm, wa), lambda i, j: (i, 0)),
            pl.BlockSpec((tm, wf), lambda i, j: (i, 0)),
            pl.BlockSpec((tm, tn), lambda i, j: (i, g0 + j)),
            pl.BlockSpec((tm, tn), lambda i, j: (i, g0 + nj + j)),
            pl.BlockSpec((wa, tn), lambda i, j: (0, j)),
            pl.BlockSpec((wf, tn), lambda i, j: (0, j)),
        ],
        out_specs=pl.BlockSpec((tm, tn), lambda i, j: (i, j)),
        out_shape=jax.ShapeDtypeStruct((T, D), BF16),
        compiler_params=_cparams(("parallel", "parallel")),
        name="merge",
    )(attn, f, gates, gates, w_ap, w_fp)


def _outproj_kernel(m_ref, x_ref, w_ref, g_ref, wrt_ref, x1_ref, h_ref, afft_ref):
    x1 = x_ref[...] + _dot(m_ref[...], w_ref[...])
    x1_ref[...] = x1
    h32 = _rms(x1, g_ref[...])
    h_ref[...] = h32
    lt = _dot_nt(wrt_ref[...], h32.astype(BF16))
    et = jnp.exp(lt - jnp.max(lt, axis=0, keepdims=True))
    afft_ref[...] = et / jnp.sum(et, axis=0, keepdims=True)


def _outproj(merged, x, w_out, g, w_router_t, tm=512):
    T, D = x.shape
    E = w_router_t.shape[0]
    tm = min(tm, T)
    return pl.pallas_call(
        _outproj_kernel,
        grid=(T // tm,),
        in_specs=[
            pl.BlockSpec((tm, D), lambda i: (i, 0)),
            pl.BlockSpec((tm, D), lambda i: (i, 0)),
            pl.BlockSpec((D, D), lambda i: (0, 0), pipeline_mode=pl.Buffered(1)),
            pl.BlockSpec((1, D), lambda i: (0, 0)),
            pl.BlockSpec((E, D), lambda i: (0, 0)),
        ],
        out_specs=[
            pl.BlockSpec((tm, D), lambda i: (i, 0)),
            pl.BlockSpec((tm, D), lambda i: (i, 0)),
            pl.BlockSpec((E, tm), lambda i: (0, i)),
        ],
        out_shape=[
            jax.ShapeDtypeStruct((T, D), F32),
            jax.ShapeDtypeStruct((T, D), F32),
            jax.ShapeDtypeStruct((E, T), F32),
        ],
        compiler_params=_cparams(("parallel",)),
        name="outproj_router",
    )(merged, x, w_out, g, w_router_t)


def _route_kernel(aff_ref, q_ref, qt_ref, offs_ref, *, cap):
    E, nb, L = aff_ref.shape
    rows = E * nb
    aff = aff_ref[...]

    def count(mask):
        c = jnp.sum(mask.astype(F32), axis=2, keepdims=True)
        return jnp.sum(c, axis=1, keepdims=True)

    def search(i, prefix):
        cand = prefix | (jnp.int32(1) << (30 - i))
        ge = aff >= lax.bitcast_convert_type(cand, F32)
        return jnp.where(count(ge) >= cap, cand, prefix)

    thr = lax.bitcast_convert_type(lax.fori_loop(0, 31, search, jnp.zeros((E, 1, 1), I32)), F32)

    ri = lax.broadcasted_iota(I32, (L, L), 0)
    ci = lax.broadcasted_iota(I32, (L, L), 1)
    tri = (ri <= ci).astype(BF16)
    ones = jnp.ones((L, L), BF16)
    rr = lax.broadcasted_iota(I32, (rows, rows), 0)
    rc = lax.broadcasted_iota(I32, (rows, rows), 1)
    sh = nb.bit_length() - 1
    same_expert = lax.shift_right_logical(rr, sh) == lax.shift_right_logical(rc, sh)
    earlier = (same_expert & (rc < rr)).astype(BF16)

    def cumsum(mask):
        x = mask.astype(F32).reshape(rows, L).astype(BF16)
        within = _dot(x, tri)
        tot = _dot(x, ones)
        offs = _dot(earlier, tot.astype(BF16))
        return (within + offs).reshape(E, nb, L), offs

    gt = aff > thr
    eq = aff == thr
    need = cap - count(gt)
    sel = gt | (eq & (cumsum(eq)[0] <= need))
    rank, offs = cumsum(sel)
    q_ref[...] = jnp.where(sel, rank - 1.0, -1.0)
    offs_ref[...] = offs

    filler = jnp.full((L - E, L), -1.0, F32)
    for b in range(nb):
        tile = jnp.concatenate([q_ref[:, b, :], filler], axis=0)
        qt_ref[b * L:(b + 1) * L, :] = tile.T


def _route(aff_t, cap):
    E, T = aff_t.shape
    nb = T // LANES
    aff3 = aff_t.reshape(E, nb, LANES)
    return pl.pallas_call(
        functools.partial(_route_kernel, cap=cap),
        out_shape=[
            jax.ShapeDtypeStruct((E, nb, LANES), F32),
            jax.ShapeDtypeStruct((T, LANES), F32),
            jax.ShapeDtypeStruct((E * nb, LANES), F32),
        ],
        compiler_params=pltpu.CompilerParams(vmem_limit_bytes=VMEM_LIMIT),
        name="route",
    )(aff3)


MOE_TOKEN_BLOCK = 256
MOE_WINDOW = 64


def _compact_kernel(cnt_ref, q_ref, aff_ref, idx_ref, gate_ref, *, cap):
    e = pl.program_id(0)
    nbk, kb = q_ref.shape
    L = LANES
    base = e * (nbk + 1)
    slot_col = lax.broadcasted_iota(I32, (L, 1), 0).astype(F32)
    lane_tok = lax.broadcasted_iota(I32, (1, kb), 1).astype(F32)
    b_lo = b_hi = jnp.int32(0)
    for c in range(cap // L):
        lo_slot, hi_slot = c * L, (c + 1) * L
        b_lo = lax.while_loop(lambda b: cnt_ref[base + b + 1] <= lo_slot, lambda b: b + 1, b_lo)
        b_hi = lax.while_loop(lambda b: (b < nbk) & (cnt_ref[base + jnp.minimum(b, nbk)] < hi_slot), lambda b: b + 1, b_hi)

        def body(b, acc, lo_slot=lo_slot):
            ai, ag = acc
            hit = q_ref[pl.ds(b, 1), :] == slot_col + float(lo_slot)
            tok = lane_tok + lax.convert_element_type(b * kb, F32)
            return ai + jnp.where(hit, tok, 0.0), ag + jnp.where(hit, aff_ref[pl.ds(b, 1), :], 0.0)

        zero = jnp.zeros((L, kb), F32)
        ai, ag = lax.fori_loop(b_lo, b_hi, body, (zero, zero))
        folded = functools.reduce(jnp.add, [ai[:, i * L:(i + 1) * L] for i in range(kb // L)])
        idx_ref[:, lo_slot:hi_slot] = jnp.sum(folded.T, axis=0, keepdims=True).astype(I32)
        gate_ref[lo_slot:hi_slot, :] = jnp.sum(ag, axis=1, keepdims=True)


def _compact(cnt, q, aff, cap):
    E, nbk, kb = q.shape
    grid_spec = pltpu.PrefetchScalarGridSpec(
        num_scalar_prefetch=1,
        grid=(E,),
        in_specs=[
            pl.BlockSpec((None, nbk, kb), lambda e, cnt: (e, 0, 0)),
            pl.BlockSpec((None, nbk, kb), lambda e, cnt: (e, 0, 0)),
        ],
        out_specs=[
            pl.BlockSpec((None, 1, cap), lambda e, cnt: (e, 0, 0)),
            pl.BlockSpec((cap, 1), lambda e, cnt: (e, 0)),
        ],
    )
    return pl.pallas_call(
        functools.partial(_compact_kernel, cap=cap),
        grid_spec=grid_spec,
        out_shape=[jax.ShapeDtypeStruct((E, 1, cap), I32), jax.ShapeDtypeStruct((E * cap, 1), F32)],
        compiler_params=_cparams(("parallel",)),
        name="moe_compact",
    )(cnt, q, aff)


def _ffn_kernel(idx_ref, h_hbm, gate_ref, wg_ref, wu_ref, wd_ref, o_ref, xbuf, x_ref, acc_ref, sem, *, cap, n_f):
    e = pl.program_id(0)
    f = pl.program_id(1)
    per_step = cap // n_f

    def fetch_row(expert, r):
        tok = idx_ref[expert * cap + r]
        pltpu.make_async_copy(h_hbm.at[pl.ds(tok, 1), :], xbuf.at[pl.ds(r, 1), :], sem).start()

    @pl.when(f == 0)
    def _():
        @pl.when(e == 0)
        def _():
            lax.fori_loop(0, cap, lambda r, c: (fetch_row(0, r), c)[1], 0, unroll=8)

        pltpu.make_async_copy(h_hbm.at[pl.ds(0, cap), :], xbuf, sem).wait()
        x_ref[...] = xbuf[...].astype(BF16)
        acc_ref[...] = jnp.zeros_like(acc_ref)

    x = x_ref[...]
    hid = jax.nn.silu(_dot(x, wg_ref[...].astype(BF16))) * _dot(x, wu_ref[...].astype(BF16))
    acc_ref[...] += _dot(hid.astype(BF16), wd_ref[...].astype(BF16))

    last_e = e == pl.num_programs(0) - 1
    nxt = jnp.where(last_e, 0, e + 1)
    for r in range(per_step):
        fetch_row(nxt, f * per_step + r)

    @pl.when(f == n_f - 1)
    def _():
        o_ref[:cap, :] = (acc_ref[...] * gate_ref[...]).astype(o_ref.dtype)
        o_ref[cap:, :] = jnp.zeros((o_ref.shape[0] - cap, o_ref.shape[1]), o_ref.dtype)

        @pl.when(last_e)
        def _():
            pltpu.make_async_copy(h_hbm.at[pl.ds(0, cap), :], xbuf, sem).wait()


def _ffn(idx, h, gate, wg, wu, wd, cap, pad, fn=256):
    E, D, Fw = wg.shape
    fn = min(fn, Fw)
    grid_spec = pltpu.PrefetchScalarGridSpec(
        num_scalar_prefetch=1,
        grid=(E, Fw // fn),
        in_specs=[
            pl.BlockSpec(memory_space=pl.ANY),
            pl.BlockSpec((cap, 1), lambda e, f, idx: (e, 0)),
            pl.BlockSpec((None, D, fn), lambda e, f, idx: (e, 0, f)),
            pl.BlockSpec((None, D, fn), lambda e, f, idx: (e, 0, f)),
            pl.BlockSpec((None, fn, D), lambda e, f, idx: (e, f, 0)),
        ],
        out_specs=pl.BlockSpec((cap + pad, D), lambda e, f, idx: (e, 0)),
        scratch_shapes=[
            pltpu.VMEM((cap, D), F32),
            pltpu.VMEM((cap, D), BF16),
            pltpu.VMEM((cap, D), F32),
            pltpu.SemaphoreType.DMA(()),
        ],
    )
    return pl.pallas_call(
        functools.partial(_ffn_kernel, cap=cap, n_f=Fw // fn),
        grid_spec=grid_spec,
        out_shape=jax.ShapeDtypeStruct((E * (cap + pad), D), BF16),
        compiler_params=_cparams(("arbitrary", "arbitrary")),
        name="moe_ffn",
    )(idx, h, gate, wg, wu, wd)


def _combine_kernel(cnt_ref, x_ref, qt_ref, o_hbm, g_ref, y_ref, obuf, acc_ref, sems, *, n_exp, stride, nbk):
    b = pl.program_id(0)
    E = n_exp
    W = obuf.shape[1] // E
    tile = 16

    def first_slots(blk):
        return [lax.shift_left(lax.shift_right_logical(cnt_ref[e * (nbk + 1) + blk], 4), 4) for e in range(E)]

    def copies(starts, buf):
        return [
            pltpu.make_async_copy(
                o_hbm.at[pl.ds(pl.multiple_of(e * stride + starts[e], tile), W), :],
                obuf.at[buf, pl.ds(e * W, W), :],
                sems.at[buf],
            )
            for e in range(E)
        ]

    lane = lax.broadcasted_iota(I32, (1, 2 * W), 1)
    first = lane < W
    j = jnp.where(first, lane, lane - W).astype(F32)

    def scatter(starts, buf):
        seg = []
        for e in range(0, E, 2):
            slot = jnp.where(first, qt_ref[:, e:e + 1], qt_ref[:, e + 1:e + 2])
            want = jnp.where(first, starts[e].astype(F32), starts[e + 1].astype(F32)) + j
            seg.append(jnp.where(slot == want, 1.0, 0.0).astype(BF16))
        acc_ref[...] += _dot(jnp.concatenate(seg, axis=1), obuf[buf])

    lo = first_slots(b)
    hi = [cnt_ref[e * (nbk + 1) + b + 1] for e in range(E)]
    span = functools.reduce(jnp.maximum, [h - l for h, l in zip(hi, lo)])
    n_pass = lax.shift_right_logical(span + (W - 1), W.bit_length() - 1)
    buf = lax.rem(b, 2)

    @pl.when(b == 0)
    def _():
        for cp in copies(lo, 0):
            cp.start()

    for cp in copies(lo, buf):
        cp.wait()

    @pl.when(b + 1 < pl.num_programs(0))
    def _():
        for cp in copies(first_slots(b + 1), 1 - buf):
            cp.start()

    acc_ref[...] = x_ref[...]
    scatter(lo, buf)

    def later_pass(p, carry):
        starts = [l + p * W for l in lo]
        for cp in copies(starts, buf):
            cp.start()
        for cp in copies(starts, buf):
            cp.wait()
        scatter(starts, buf)
        return carry

    lax.fori_loop(1, n_pass, later_pass, 0)
    y_ref[...] = _rms(acc_ref[...], g_ref[...])


def _combine(cnt, x1, qt, o, g, n_exp, stride):
    T, D = x1.shape
    tb = min(MOE_TOKEN_BLOCK, T)
    grid_spec = pltpu.PrefetchScalarGridSpec(
        num_scalar_prefetch=1,
        grid=(T // tb,),
        in_specs=[
            pl.BlockSpec((tb, D), lambda b, cnt: (b, 0)),
            pl.BlockSpec((tb, LANES), lambda b, cnt: (b, 0)),
            pl.BlockSpec(memory_space=pl.ANY),
            pl.BlockSpec((1, D), lambda b, cnt: (0, 0)),
        ],
        out_specs=pl.BlockSpec((tb, D), lambda b, cnt: (b, 0)),
        scratch_shapes=[
            pltpu.VMEM((2, n_exp * MOE_WINDOW, D), BF16),
            pltpu.VMEM((tb, D), F32),
            pltpu.SemaphoreType.DMA((2,)),
        ],
    )
    return pl.pallas_call(
        functools.partial(_combine_kernel, n_exp=n_exp, stride=stride, nbk=T // tb),
        grid_spec=grid_spec,
        out_shape=jax.ShapeDtypeStruct((T, D), F32),
        compiler_params=_cparams(("arbitrary",)),
        name="moe_combine_norm",
    )(cnt, x1, qt, o, g)


def _expert_choice_moe(x1, h2, aff_t, p):
    T, D = x1.shape
    E = aff_t.shape[0]
    cap = CAPACITY_FACTOR * T // E
    q, qt, offs = _route(aff_t, cap)
    nb = T // LANES
    per_blk = min(MOE_TOKEN_BLOCK, T) // LANES
    cnt = offs[:, 0].reshape(E, nb)[:, ::per_blk]
    cnt = jnp.concatenate([cnt, jnp.full((E, 1), cap, F32)], axis=1).astype(I32).reshape(-1)
    blocked = (E, nb // per_blk, per_blk * LANES)
    idx, gate = _compact(cnt, q.reshape(blocked), aff_t.reshape(blocked), cap)
    o = _ffn(idx.reshape(E * cap), h2, gate, p["w_expert_gate"], p["w_expert_up"], p["w_expert_down"], cap, MOE_WINDOW)
    return _combine(cnt, x1, qt, o, p["norm_final_g"], E, cap + MOE_WINDOW)


def _encoder_group(x, p):
    B, S, D = x.shape
    T = B * S
    hd = p["q_norm_g"].shape[-1]
    wq, wk = N_Q_HEADS * hd, N_KV_HEADS * hd
    wu = N_FOURIER_GROUPS * FOURIER_GROUP_W
    xt = x.reshape(T, D)

    proj = _inproj(xt, p["norm_mix_g"], p["w_in"])
    u_off = wq + 2 * wk
    cos_t, sin_t = _rope_tables(S, hd)
    qr, kt, va = _qkrope(proj, wq, wk, p["q_norm_g"], p["k_norm_g"], cos_t, sin_t, S, ts=min(ATTN_KV_CHUNK, S))
    attn = _attention_any(qr, kt, va, p["q_norm_g"], p["k_norm_g"], B, S)
    f = _fourier(proj, u_off // wu, wu, B, S)
    merged = _merge(attn, f, proj, u_off + wu, p["w_attn_proj"], p["w_fourier_proj"])
    x1, h2, aff_t = _outproj(merged, xt, p["w_out"], p["norm_moe_g"], p["w_router_t"])
    return _expert_choice_moe(x1, h2, aff_t, p).reshape(B, S, D)


def kernel(x_prompt, x_sample, norm_mix_g, w_in, q_norm_g, k_norm_g, w_attn_proj, w_fourier_proj, w_out, norm_moe_g, w_router, w_expert_gate, w_expert_up, w_expert_down, norm_final_g):
    p = {
        "norm_mix_g": norm_mix_g[0][None, :],
        "w_in": w_in[0],
        "q_norm_g": q_norm_g[0][None, :],
        "k_norm_g": k_norm_g[0][None, :],
        "w_attn_proj": w_attn_proj[0].astype(BF16),
        "w_fourier_proj": w_fourier_proj[0].astype(BF16),
        "w_out": w_out[0].astype(BF16),
        "norm_moe_g": norm_moe_g[0][None, :],
        "w_router_t": w_router[0].T.astype(BF16),
        "w_expert_gate": w_expert_gate[0],
        "w_expert_up": w_expert_up[0],
        "w_expert_down": w_expert_down[0],
        "norm_final_g": norm_final_g[None, :],
    }
    return (_encoder_group(x_prompt, p), _encoder_group(x_sample, p))
```

```python
import functools
import math

import jax
import jax.numpy as jnp
from jax import lax
from jax.experimental import pallas as pl
from jax.experimental.pallas import tpu as pltpu

F32 = jnp.float32
BF16 = jnp.bfloat16
I32 = jnp.int32

EPS = 1e-6
N_Q_HEADS = 16
N_KV_HEADS = 4
ROPE_GRID_W = 64
ROPE_THETA = 10000.0
N_FOURIER_GROUPS = 4
FOURIER_GROUP_W = 256
CAPACITY_FACTOR = 2
LANES = 128
BF16_ROWS = 16
ATTN_KV_CHUNK = 512
VMEM_LIMIT = 56 * 1024 * 1024


def _cparams(sem):
    return pltpu.CompilerParams(dimension_semantics=sem, vmem_limit_bytes=VMEM_LIMIT)


def _dot(a, b):
    return jnp.dot(a, b, preferred_element_type=F32)


def _dot_nt(a, b):
    return lax.dot_general(a, b, (((1,), (1,)), ((), ())), preferred_element_type=F32)


def _rms(x, g):
    ms = jnp.mean(x * x, axis=-1, keepdims=True)
    return x * lax.rsqrt(ms + EPS) * g


def _inproj_kernel(x_ref, g_ref, w_ref, o_ref, h_ref):
    @pl.when(pl.program_id(1) == 0)
    def _():
        h_ref[...] = _rms(x_ref[...], g_ref[...]).astype(BF16)

    o_ref[...] = _dot(h_ref[...], w_ref[...].astype(BF16)).astype(o_ref.dtype)


def _inproj(x, g, w, tm=1024, tn=1024):
    T, D = x.shape
    N = w.shape[1]
    tm = min(tm, T)
    return pl.pallas_call(
        _inproj_kernel,
        grid=(T // tm, N // tn),
        in_specs=[
            pl.BlockSpec((tm, D), lambda i, j: (i, 0)),
            pl.BlockSpec((1, D), lambda i, j: (0, 0)),
            pl.BlockSpec((D, tn), lambda i, j: (0, j)),
        ],
        out_specs=pl.BlockSpec((tm, tn), lambda i, j: (i, j)),
        out_shape=jax.ShapeDtypeStruct((T, N), BF16),
        scratch_shapes=[pltpu.VMEM((tm, D), BF16)],
        compiler_params=_cparams(("parallel", "arbitrary")),
        name="inproj",
    )(x, g, w)


def _qkrope_kernel(q_ref, k_ref, v_ref, qg_ref, kg_ref, c_ref, s_ref, ct_ref, st_ref, qo_ref, ko_ref, vo_ref, *, hd, scale):
    c = c_ref[...]
    s = s_ref[...]
    lane = lax.broadcasted_iota(I32, (1, hd), 1)
    low_half = (lane % (hd // 2)) < (hd // 4)

    def one(x, g):
        y = _rms(x, g)
        sw = jnp.where(low_half, pltpu.roll(y, hd - hd // 4, 1), pltpu.roll(y, hd // 4, 1))
        return y * c + sw * s

    ct = ct_ref[...]
    st = st_ref[...]
    qg = qg_ref[...]
    quarter = hd // 4
    for h in range(q_ref.shape[1] // hd):
        xt = q_ref[:, h * hd:(h + 1) * hd].astype(F32).T
        ms = jnp.mean(xt * xt, axis=0, keepdims=True)
        y = xt * lax.rsqrt(ms + EPS) * qg
        sw = jnp.concatenate([y[quarter:2 * quarter], y[:quarter], y[3 * quarter:], y[2 * quarter:3 * quarter]], axis=0)
        qo_ref[h] = ((y * ct + sw * st) * scale).astype(qo_ref.dtype)
    pad = vo_ref.shape[2] - hd
    ones_row = jnp.where(lax.broadcasted_iota(I32, (pad, v_ref.shape[0]), 0) == 0, 1.0, 0.0)
    for h in range(k_ref.shape[1] // hd):
        sl = slice(h * hd, (h + 1) * hd)
        ko_ref[:, sl] = one(k_ref[:, sl].astype(F32), kg_ref[...]).astype(ko_ref.dtype)
        vt = jnp.concatenate([v_ref[:, sl].astype(F32).T, ones_row], axis=0)
        vo_ref[h, 0] = vt.astype(vo_ref.dtype)


def _qkrope(proj, wq, wk, qg, kg, rope, S, ts):
    T = proj.shape[0]
    hd = qg.shape[-1]
    n_q, n_kv = wq // hd, wk // hd
    ns = S // ts
    k_blk = wq // wk
    return pl.pallas_call(
        functools.partial(_qkrope_kernel, hd=hd, scale=hd ** -0.5 * math.log2(math.e)),
        grid=(T // ts,),
        in_specs=[
            pl.BlockSpec((ts, wq), lambda i: (i, 0)),
            pl.BlockSpec((ts, wk), lambda i: (i, k_blk)),
            pl.BlockSpec((ts, wk), lambda i: (i, k_blk + 1)),
            pl.BlockSpec((hd, 1), lambda i: (0, 0)),
            pl.BlockSpec((1, hd), lambda i: (0, 0)),
            pl.BlockSpec((ts, hd), lambda i: (i % ns, 0)),
            pl.BlockSpec((ts, hd), lambda i: (i % ns, 0)),
            pl.BlockSpec((hd, ts), lambda i: (0, i % ns)),
            pl.BlockSpec((hd, ts), lambda i: (0, i % ns)),
        ],
        out_specs=[
            pl.BlockSpec((n_q, hd, ts), lambda i: (0, 0, i)),
            pl.BlockSpec((ts, wk), lambda i: (i, 0)),
            pl.BlockSpec((n_kv, 1, hd + BF16_ROWS, ts), lambda i: (0, i, 0, 0)),
        ],
        out_shape=[
            jax.ShapeDtypeStruct((n_q, hd, T), BF16),
            jax.ShapeDtypeStruct((T, wk), BF16),
            jax.ShapeDtypeStruct((n_kv, T // ts, hd + BF16_ROWS, ts), BF16),
        ],
        compiler_params=_cparams(("parallel",)),
        name="qkrope",
    )(proj, proj, proj, qg.reshape(hd, 1), kg, *rope)


def _rope_tables(S, hd):
    axis = hd // 2
    rows = S // ROPE_GRID_W
    inv_freq = ROPE_THETA ** (-jnp.arange(0, axis, 2, dtype=F32) / axis)
    ang_r = jnp.arange(rows, dtype=F32)[:, None] * inv_freq
    ang_c = jnp.arange(ROPE_GRID_W, dtype=F32)[:, None] * inv_freq
    cr, sr = (jnp.repeat(t, ROPE_GRID_W, axis=0) for t in (jnp.cos(ang_r), jnp.sin(ang_r)))
    cc, sc = (jnp.tile(t, (rows, 1)) for t in (jnp.cos(ang_c), jnp.sin(ang_c)))
    cos_t = jnp.concatenate([cr, cr, cc, cc], axis=-1)
    sin_t = jnp.concatenate([-sr, sr, -sc, sc], axis=-1)
    return cos_t, sin_t, cos_t.T, sin_t.T


def _attn_finish(acc_ref, o_ref, hd):
    for h in range(acc_ref.shape[0]):
        a = acc_ref[h]
        o_ref[:, h * hd:(h + 1) * hd] = (a[:hd] / a[hd:hd + 1]).T.astype(o_ref.dtype)


def _attn_bounded_kernel(qt_ref, k_ref, vt_ref, o_ref, acc_ref, p_ref):
    G, hd, tq = qt_ref.shape
    n_chunks, _, tk = vt_ref.shape
    acc_ref[...] = jnp.zeros_like(acc_ref)

    def keys(c):
        return k_ref[pl.ds(pl.multiple_of(c * tk, tk), tk), :]

    k0 = keys(0)
    for h in range(G):
        p_ref[h] = jnp.exp2(_dot(k0, qt_ref[h])).astype(BF16)

    def body(c, carry):
        kc = keys(c)
        vt = vt_ref[c - 1]
        for h in range(G):
            s = _dot(kc, qt_ref[h])
            acc_ref[h] += _dot(vt, p_ref[h])
            p_ref[h] = jnp.exp2(s).astype(BF16)
        return carry

    unroll = max(u for u in range(1, 9) if (n_chunks - 1) % u == 0) if n_chunks > 1 else 1
    lax.fori_loop(1, n_chunks, body, 0, unroll=unroll)
    vt = vt_ref[n_chunks - 1]
    for h in range(G):
        acc_ref[h] += _dot(vt, p_ref[h])
    _attn_finish(acc_ref, o_ref, hd)


def _attn_online_kernel(qt_ref, k_ref, vt_ref, o_ref, acc_ref, m_ref):
    G, hd, tq = qt_ref.shape
    n_chunks, _, tk = vt_ref.shape
    acc_ref[...] = jnp.zeros_like(acc_ref)
    m_ref[...] = jnp.full_like(m_ref, -jnp.inf)

    def body(c, carry):
        kc = k_ref[pl.ds(pl.multiple_of(c * tk, tk), tk), :]
        vt = vt_ref[c]
        for h in range(G):
            s = _dot(kc, qt_ref[h])
            m = m_ref[h]
            m_new = jnp.maximum(m, jnp.max(s, axis=0, keepdims=True))
            p = jnp.exp2(s - m_new).astype(BF16)
            acc_ref[h] = jnp.exp2(m - m_new) * acc_ref[h] + _dot(vt, p)
            m_ref[h] = m_new
        return carry

    lax.fori_loop(0, n_chunks, body, 0)
    _attn_finish(acc_ref, o_ref, hd)


def _attention(q, k, vt, B, S, bounded, tq=512):
    n_q, hd, T = q.shape
    n_kv, _, vr, tk = vt.shape
    G = n_q // n_kv
    tq = min(tq, S)
    nq = S // tq
    nc = S // tk
    scratch = [pltpu.VMEM((G, vr, tq), F32)]
    scratch.append(pltpu.VMEM((G, tk, tq), BF16) if bounded else pltpu.VMEM((G, 1, tq), F32))
    return pl.pallas_call(
        _attn_bounded_kernel if bounded else _attn_online_kernel,
        grid=(B, n_kv, nq),
        in_specs=[
            pl.BlockSpec((G, hd, tq), lambda b, g, i: (g, 0, b * nq + i)),
            pl.BlockSpec((S, hd), lambda b, g, i: (b, g)),
            pl.BlockSpec((None, nc, vr, tk), lambda b, g, i: (g, b, 0, 0)),
        ],
        out_specs=pl.BlockSpec((tq, G * hd), lambda b, g, i: (b * nq + i, g)),
        out_shape=jax.ShapeDtypeStruct((T, n_q * hd), BF16),
        scratch_shapes=scratch,
        compiler_params=_cparams(("parallel", "parallel", "parallel")),
        name="attention_bounded" if bounded else "attention_online",
    )(q, k, vt)


MAX_UNSHIFTED_LOG2_SCORE = 100.0


def _attention_any(q, kt, v, qg, kg, B, S):
    hd = qg.shape[-1]
    bound = 1.02 * math.sqrt(hd) * math.log2(math.e) * jnp.max(jnp.abs(qg)) * jnp.max(jnp.abs(kg))
    return lax.cond(
        bound <= MAX_UNSHIFTED_LOG2_SCORE,
        lambda: _attention(q, kt, v, B, S, True),
        lambda: _attention(q, kt, v, B, S, False),
    )


def _dft_split(S):
    lg = int(math.log2(S))
    assert 1 << lg == S
    n2 = 1 << ((lg + 1) // 2)
    return S // n2, n2


def _cos_sin(n, m, period):
    ang = (2.0 * math.pi / period) * ((jnp.arange(n, dtype=I32)[:, None] * jnp.arange(m, dtype=I32)[None, :]) % period).astype(F32)
    return jnp.cos(ang), jnp.sin(ang)


def _f0_kernel(u_ref, cs_ref, a_ref, b_ref, *, gw):
    cs = cs_ref[...]
    for g in range(u_ref.shape[1] // gw):
        sl = slice(g * gw, (g + 1) * gw)
        ab = _dot(u_ref[:, sl], cs)
        a_ref[:, sl] = ab[:, :gw]
        b_ref[:, sl] = ab[:, gw:]


def _fa_kernel(a_hbm, b_hbm, ca_ref, sa_ref, ct_ref, st_ref, tr_hbm, ti_hbm, abuf, bbuf, trbuf, tibuf, sems):
    r, n2, W = abuf.shape
    bb = pl.program_id(0)
    j0 = pl.program_id(1) * r
    rows = pl.ds(bb * n2, n2)

    def loads(jj):
        return (
            pltpu.make_async_copy(a_hbm.at[rows, j0 + jj, :], abuf.at[jj], sems.at[0, jj]),
            pltpu.make_async_copy(b_hbm.at[rows, j0 + jj, :], bbuf.at[jj], sems.at[1, jj]),
        )

    def stores(jj):
        return (
            pltpu.make_async_copy(trbuf.at[jj], tr_hbm.at[rows, j0 + jj, :], sems.at[2, jj]),
            pltpu.make_async_copy(tibuf.at[jj], ti_hbm.at[rows, j0 + jj, :], sems.at[3, jj]),
        )

    for jj in range(r):
        for cp in loads(jj):
            cp.start()
    ca = ca_ref[...]
    sa = sa_ref[...]
    for jj in range(r):
        for cp in loads(jj):
            cp.wait()
        a = abuf[jj].astype(BF16)
        b = bbuf[jj].astype(BF16)
        tr = _dot(ca, a) - _dot(sa, b)
        ti = -(_dot(sa, a) + _dot(ca, b))
        ct = ct_ref[jj]
        st = st_ref[jj]
        trbuf[jj] = tr * ct + ti * st
        tibuf[jj] = ti * ct - tr * st
        for cp in stores(jj):
            cp.start()
    for jj in range(r):
        for cp in stores(jj):
            cp.wait()


def _fb_kernel(tr_ref, ti_ref, c1_ref, s1_ref, f_ref, *, scale):
    c1 = c1_ref[...]
    s1 = s1_ref[...]
    for kk in range(tr_ref.shape[0]):
        xr = tr_ref[kk].astype(BF16)
        xi = ti_ref[kk].astype(BF16)
        f_ref[:, kk, :] = (_dot(c1, xr) + _dot(s1, xi)) * scale


def _fourier(proj, u_blk, W, B, S, ts=512, r=8):
    u = proj
    T = proj.shape[0]
    gw = FOURIER_GROUP_W
    ts = min(ts, T)
    n1, n2 = _dft_split(S)

    cc, sc = _cos_sin(gw, gw, gw)
    cs = jnp.concatenate([cc, sc], axis=1).astype(BF16)
    a, b = pl.pallas_call(
        functools.partial(_f0_kernel, gw=gw),
        grid=(T // ts,),
        in_specs=[pl.BlockSpec((ts, W), lambda i: (i, u_blk)), pl.BlockSpec((gw, 2 * gw), lambda i: (0, 0))],
        out_specs=[pl.BlockSpec((ts, W), lambda i: (i, 0))] * 2,
        out_shape=[jax.ShapeDtypeStruct((T, W), F32)] * 2,
        compiler_params=_cparams(("parallel",)),
        name="fourier_channels",
    )(u, cs)

    a3 = a.reshape(B * n2, n1, W)
    b3 = b.reshape(B * n2, n1, W)
    ca, sa = _cos_sin(n2, n2, n2)
    ctw, stw = _cos_sin(n1, n2, S)
    ctw = ctw[:, :, None]
    stw = stw[:, :, None]
    tr, ti = pl.pallas_call(
        _fa_kernel,
        grid=(B, n1 // r),
        in_specs=[
            pl.BlockSpec(memory_space=pl.ANY),
            pl.BlockSpec(memory_space=pl.ANY),
            pl.BlockSpec((n2, n2), lambda bb, j: (0, 0)),
            pl.BlockSpec((n2, n2), lambda bb, j: (0, 0)),
            pl.BlockSpec((r, n2, 1), lambda bb, j: (j, 0, 0)),
            pl.BlockSpec((r, n2, 1), lambda bb, j: (j, 0, 0)),
        ],
        out_specs=[pl.BlockSpec(memory_space=pl.ANY)] * 2,
        out_shape=[jax.ShapeDtypeStruct((B * n2, n1, W), F32)] * 2,
        scratch_shapes=[pltpu.VMEM((r, n2, W), F32)] * 4 + [pltpu.SemaphoreType.DMA((4, r))],
        compiler_params=_cparams(("parallel", "parallel")),
        name="fourier_stage_a",
    )(a3, b3, ca.astype(BF16), sa.astype(BF16), ctw, stw)

    c1, s1 = _cos_sin(n1, n1, n1)
    f3 = pl.pallas_call(
        functools.partial(_fb_kernel, scale=1.0 / math.sqrt(S * gw)),
        grid=(B, n2 // r),
        in_specs=[
            pl.BlockSpec((r, n1, W), lambda bb, k: (bb * (n2 // r) + k, 0, 0)),
            pl.BlockSpec((r, n1, W), lambda bb, k: (bb * (n2 // r) + k, 0, 0)),
            pl.BlockSpec((n1, n1), lambda bb, k: (0, 0)),
            pl.BlockSpec((n1, n1), lambda bb, k: (0, 0)),
        ],
        out_specs=pl.BlockSpec((n1, r, W), lambda bb, k: (bb, k, 0)),
        out_shape=jax.ShapeDtypeStruct((B * n1, n2, W), F32),
        compiler_params=_cparams(("parallel", "parallel")),
        name="fourier_stage_b",
    )(tr, ti, c1.astype(BF16), s1.astype(BF16))
    return f3.reshape(T, W)


def _merge_kernel(at_ref, f_ref, ga_ref, gf_ref, wap_ref, wfp_ref, o_ref):
    a = _dot(at_ref[...], wap_ref[...])
    fo = _dot(f_ref[...].astype(BF16), wfp_ref[...])
    ga = jax.nn.sigmoid(ga_ref[...].astype(F32))
    gf = jax.nn.sigmoid(gf_ref[...].astype(F32))
    o_ref[...] = (ga * a + gf * fo).astype(o_ref.dtype)


def _merge(attn, f, gates, g_off, w_ap, w_fp, tm=1024, tn=1024):
    T, wa = attn.shape
    wf = f.shape[1]
    D = w_ap.shape[1]
    tm = min(tm, T)
    nj = D // tn
    g0 = g_off // tn
    return pl.pallas_call(
        _merge_kernel,
        grid=(T // tm, nj),
        in_specs=[
            pl.BlockSpec((tm, wa), lambda i, j: (i, 0)),
            pl.BlockSpec((tm, wf), lambda i, j: (i, 0)),
            pl.BlockSpec((tm, tn), lambda i, j: (i, g0 + j)),
            pl.BlockSpec((tm, tn), lambda i, j: (i, g0 + nj + j)),
            pl.BlockSpec((wa, tn), lambda i, j: (0, j)),
            pl.BlockSpec((wf, tn), lambda i, j: (0, j)),
        ],
        out_specs=pl.BlockSpec((tm, tn), lambda i, j: (i, j)),
        out_shape=jax.ShapeDtypeStruct((T, D), BF16),
        compiler_params=_cparams(("parallel", "parallel")),
        name="merge",
    )(attn, f, gates, gates, w_ap, w_fp)


def _outproj_kernel(m_ref, x_ref, w_ref, g_ref, wrt_ref, x1_ref, h_ref, afft_ref):
    x1 = x_ref[...] + _dot(m_ref[...], w_ref[...])
    x1_ref[...] = x1
    h32 = _rms(x1, g_ref[...])
    h_ref[...] = h32
    lt = _dot_nt(wrt_ref[...], h32.astype(BF16))
    et = jnp.exp(lt - jnp.max(lt, axis=0, keepdims=True))
    afft_ref[...] = et / jnp.sum(et, axis=0, keepdims=True)


def _outproj(merged, x, w_out, g, w_router_t, tm=512):
    T, D = x.shape
    E = w_router_t.shape[0]
    tm = min(tm, T)
    return pl.pallas_call(
        _outproj_kernel,
        grid=(T // tm,),
        in_specs=[
            pl.BlockSpec((tm, D), lambda i: (i, 0)),
            pl.BlockSpec((tm, D), lambda i: (i, 0)),
            pl.BlockSpec((D, D), lambda i: (0, 0), pipeline_mode=pl.Buffered(1)),
            pl.BlockSpec((1, D), lambda i: (0, 0)),
            pl.BlockSpec((E, D), lambda i: (0, 0)),
        ],
        out_specs=[
            pl.BlockSpec((tm, D), lambda i: (i, 0)),
            pl.BlockSpec((tm, D), lambda i: (i, 0)),
            pl.BlockSpec((E, tm), lambda i: (0, i)),
        ],
        out_shape=[
            jax.ShapeDtypeStruct((T, D), F32),
            jax.ShapeDtypeStruct((T, D), F32),
            jax.ShapeDtypeStruct((E, T), F32),
        ],
        compiler_params=_cparams(("parallel",)),
        name="outproj_router",
    )(merged, x, w_out, g, w_router_t)


def _route_kernel(aff_ref, q_ref, qt_ref, offs_ref, *, cap):
    E, nb, L = aff_ref.shape
    rows = E * nb
    aff = aff_ref[...]

    def count(mask):
        c = jnp.sum(mask.astype(F32), axis=2, keepdims=True)
        return jnp.sum(c, axis=1, keepdims=True)

    def search(i, prefix):
        cand = prefix | (jnp.int32(1) << (30 - i))
        ge = aff >= lax.bitcast_convert_type(cand, F32)
        return jnp.where(count(ge) >= cap, cand, prefix)

    thr = lax.bitcast_convert_type(lax.fori_loop(0, 31, search, jnp.zeros((E, 1, 1), I32)), F32)

    ri = lax.broadcasted_iota(I32, (L, L), 0)
    ci = lax.broadcasted_iota(I32, (L, L), 1)
    tri = (ri <= ci).astype(BF16)
    ones = jnp.ones((L, L), BF16)
    rr = lax.broadcasted_iota(I32, (rows, rows), 0)
    rc = lax.broadcasted_iota(I32, (rows, rows), 1)
    sh = nb.bit_length() - 1
    same_expert = lax.shift_right_logical(rr, sh) == lax.shift_right_logical(rc, sh)
    earlier = (same_expert & (rc < rr)).astype(BF16)

    def cumsum(mask):
        x = mask.astype(F32).reshape(rows, L).astype(BF16)
        within = _dot(x, tri)
        tot = _dot(x, ones)
        offs = _dot(earlier, tot.astype(BF16))
        return (within + offs).reshape(E, nb, L), offs

    gt = aff > thr
    eq = aff == thr
    need = cap - count(gt)
    sel = gt | (eq & (cumsum(eq)[0] <= need))
    rank, offs = cumsum(sel)
    q_ref[...] = jnp.where(sel, rank - 1.0, -1.0)
    offs_ref[...] = offs

    filler = jnp.full((L - E, L), -1.0, F32)
    for b in range(nb):
        tile = jnp.concatenate([q_ref[:, b, :], filler], axis=0)
        qt_ref[b * L:(b + 1) * L, :] = tile.T


def _route(aff_t, cap):
    E, T = aff_t.shape
    nb = T // LANES
    aff3 = aff_t.reshape(E, nb, LANES)
    return pl.pallas_call(
        functools.partial(_route_kernel, cap=cap),
        out_shape=[
            jax.ShapeDtypeStruct((E, nb, LANES), F32),
            jax.ShapeDtypeStruct((T, LANES), F32),
            jax.ShapeDtypeStruct((E * nb, LANES), F32),
        ],
        compiler_params=pltpu.CompilerParams(vmem_limit_bytes=VMEM_LIMIT),
        name="route",
    )(aff3)


MOE_TOKEN_BLOCK = 256
MOE_WINDOW = 64


def _compact_kernel(cnt_ref, q_ref, aff_ref, idx_ref, gate_ref, *, cap):
    e = pl.program_id(0)
    nbk, kb = q_ref.shape
    L = LANES
    base = e * (nbk + 1)
    slot_col = lax.broadcasted_iota(I32, (L, 1), 0).astype(F32)
    lane_tok = lax.broadcasted_iota(I32, (1, kb), 1).astype(F32)
    b_lo = b_hi = jnp.int32(0)
    for c in range(cap // L):
        lo_slot, hi_slot = c * L, (c + 1) * L
        b_lo = lax.while_loop(lambda b: cnt_ref[base + b + 1] <= lo_slot, lambda b: b + 1, b_lo)
        b_hi = lax.while_loop(lambda b: (b < nbk) & (cnt_ref[base + jnp.minimum(b, nbk)] < hi_slot), lambda b: b + 1, b_hi)

        def body(b, acc, lo_slot=lo_slot):
            ai, ag = acc
            hit = q_ref[pl.ds(b, 1), :] == slot_col + float(lo_slot)
            tok = lane_tok + lax.convert_element_type(b * kb, F32)
            return ai + jnp.where(hit, tok, 0.0), ag + jnp.where(hit, aff_ref[pl.ds(b, 1), :], 0.0)

        zero = jnp.zeros((L, kb), F32)
        ai, ag = lax.fori_loop(b_lo, b_hi, body, (zero, zero))
        folded = functools.reduce(jnp.add, [ai[:, i * L:(i + 1) * L] for i in range(kb // L)])
        idx_ref[:, lo_slot:hi_slot] = jnp.sum(folded.T, axis=0, keepdims=True).astype(I32)
        gate_ref[lo_slot:hi_slot, :] = jnp.sum(ag, axis=1, keepdims=True)


def _compact(cnt, q, aff, cap):
    E, nbk, kb = q.shape
    grid_spec = pltpu.PrefetchScalarGridSpec(
        num_scalar_prefetch=1,
        grid=(E,),
        in_specs=[
            pl.BlockSpec((None, nbk, kb), lambda e, cnt: (e, 0, 0)),
            pl.BlockSpec((None, nbk, kb), lambda e, cnt: (e, 0, 0)),
        ],
        out_specs=[
            pl.BlockSpec((None, 1, cap), lambda e, cnt: (e, 0, 0)),
            pl.BlockSpec((cap, 1), lambda e, cnt: (e, 0)),
        ],
    )
    return pl.pallas_call(
        functools.partial(_compact_kernel, cap=cap),
        grid_spec=grid_spec,
        out_shape=[jax.ShapeDtypeStruct((E, 1, cap), I32), jax.ShapeDtypeStruct((E * cap, 1), F32)],
        compiler_params=_cparams(("parallel",)),
        name="moe_compact",
    )(cnt, q, aff)


def _ffn_kernel(idx_ref, h_hbm, gate_ref, wg_ref, wu_ref, wd_ref, o_ref, xbuf, x_ref, acc_ref, sem, *, cap, n_f):
    e = pl.program_id(0)
    f = pl.program_id(1)
    per_step = cap // n_f

    def fetch_row(expert, r):
        tok = idx_ref[expert * cap + r]
        pltpu.make_async_copy(h_hbm.at[pl.ds(tok, 1), :], xbuf.at[pl.ds(r, 1), :], sem).start()

    @pl.when(f == 0)
    def _():
        @pl.when(e == 0)
        def _():
            lax.fori_loop(0, cap, lambda r, c: (fetch_row(0, r), c)[1], 0, unroll=8)

        pltpu.make_async_copy(h_hbm.at[pl.ds(0, cap), :], xbuf, sem).wait()
        x_ref[...] = xbuf[...].astype(BF16)
        acc_ref[...] = jnp.zeros_like(acc_ref)

    x = x_ref[...]
    hid = jax.nn.silu(_dot(x, wg_ref[...].astype(BF16))) * _dot(x, wu_ref[...].astype(BF16))
    acc_ref[...] += _dot(hid.astype(BF16), wd_ref[...].astype(BF16))

    last_e = e == pl.num_programs(0) - 1
    nxt = jnp.where(last_e, 0, e + 1)
    for r in range(per_step):
        fetch_row(nxt, f * per_step + r)

    @pl.when(f == n_f - 1)
    def _():
        o_ref[:cap, :] = (acc_ref[...] * gate_ref[...]).astype(o_ref.dtype)
        o_ref[cap:, :] = jnp.zeros((o_ref.shape[0] - cap, o_ref.shape[1]), o_ref.dtype)

        @pl.when(last_e)
        def _():
            pltpu.make_async_copy(h_hbm.at[pl.ds(0, cap), :], xbuf, sem).wait()


def _ffn(idx, h, gate, wg, wu, wd, cap, pad, fn=256):
    E, D, Fw = wg.shape
    fn = min(fn, Fw)
    grid_spec = pltpu.PrefetchScalarGridSpec(
        num_scalar_prefetch=1,
        grid=(E, Fw // fn),
        in_specs=[
            pl.BlockSpec(memory_space=pl.ANY),
            pl.BlockSpec((cap, 1), lambda e, f, idx: (e, 0)),
            pl.BlockSpec((None, D, fn), lambda e, f, idx: (e, 0, f)),
            pl.BlockSpec((None, D, fn), lambda e, f, idx: (e, 0, f)),
            pl.BlockSpec((None, fn, D), lambda e, f, idx: (e, f, 0)),
        ],
        out_specs=pl.BlockSpec((cap + pad, D), lambda e, f, idx: (e, 0)),
        scratch_shapes=[
            pltpu.VMEM((cap, D), F32),
            pltpu.VMEM((cap, D), BF16),
            pltpu.VMEM((cap, D), F32),
            pltpu.SemaphoreType.DMA(()),
        ],
    )
    return pl.pallas_call(
        functools.partial(_ffn_kernel, cap=cap, n_f=Fw // fn),
        grid_spec=grid_spec,
        out_shape=jax.ShapeDtypeStruct((E * (cap + pad), D), BF16),
        compiler_params=_cparams(("arbitrary", "arbitrary")),
        name="moe_ffn",
    )(idx, h, gate, wg, wu, wd)


def _combine_kernel(cnt_ref, x_ref, qt_ref, o_hbm, g_ref, y_ref, obuf, acc_ref, sems, *, n_exp, stride, nbk):
    b = pl.program_id(0)
    E = n_exp
    W = obuf.shape[1] // E
    tile = 16

    def first_slots(blk):
        return [lax.shift_left(lax.shift_right_logical(cnt_ref[e * (nbk + 1) + blk], 4), 4) for e in range(E)]

    def copies(starts, buf):
        return [
            pltpu.make_async_copy(
                o_hbm.at[pl.ds(pl.multiple_of(e * stride + starts[e], tile), W), :],
                obuf.at[buf, pl.ds(e * W, W), :],
                sems.at[buf],
            )
            for e in range(E)
        ]

    lane = lax.broadcasted_iota(I32, (1, 2 * W), 1)
    first = lane < W
    j = jnp.where(first, lane, lane - W).astype(F32)

    def scatter(starts, buf):
        seg = []
        for e in range(0, E, 2):
            slot = jnp.where(first, qt_ref[:, e:e + 1], qt_ref[:, e + 1:e + 2])
            want = jnp.where(first, starts[e].astype(F32), starts[e + 1].astype(F32)) + j
            seg.append(jnp.where(slot == want, 1.0, 0.0).astype(BF16))
        acc_ref[...] += _dot(jnp.concatenate(seg, axis=1), obuf[buf])

    lo = first_slots(b)
    hi = [cnt_ref[e * (nbk + 1) + b + 1] for e in range(E)]
    span = functools.reduce(jnp.maximum, [h - l for h, l in zip(hi, lo)])
    n_pass = lax.shift_right_logical(span + (W - 1), W.bit_length() - 1)
    buf = lax.rem(b, 2)

    @pl.when(b == 0)
    def _():
        for cp in copies(lo, 0):
            cp.start()

    for cp in copies(lo, buf):
        cp.wait()

    @pl.when(b + 1 < pl.num_programs(0))
    def _():
        for cp in copies(first_slots(b + 1), 1 - buf):
            cp.start()

    acc_ref[...] = x_ref[...]
    scatter(lo, buf)

    def later_pass(p, carry):
        starts = [l + p * W for l in lo]
        for cp in copies(starts, buf):
            cp.start()
        for cp in copies(starts, buf):
            cp.wait()
        scatter(starts, buf)
        return carry

    lax.fori_loop(1, n_pass, later_pass, 0)
    y_ref[...] = _rms(acc_ref[...], g_ref[...])


def _combine(cnt, x1, qt, o, g, n_exp, stride):
    T, D = x1.shape
    tb = min(MOE_TOKEN_BLOCK, T)
    grid_spec = pltpu.PrefetchScalarGridSpec(
        num_scalar_prefetch=1,
        grid=(T // tb,),
        in_specs=[
            pl.BlockSpec((tb, D), lambda b, cnt: (b, 0)),
            pl.BlockSpec((tb, LANES), lambda b, cnt: (b, 0)),
            pl.BlockSpec(memory_space=pl.ANY),
            pl.BlockSpec((1, D), lambda b, cnt: (0, 0)),
        ],
        out_specs=pl.BlockSpec((tb, D), lambda b, cnt: (b, 0)),
        scratch_shapes=[
            pltpu.VMEM((2, n_exp * MOE_WINDOW, D), BF16),
            pltpu.VMEM((tb, D), F32),
            pltpu.SemaphoreType.DMA((2,)),
        ],
    )
    return pl.pallas_call(
        functools.partial(_combine_kernel, n_exp=n_exp, stride=stride, nbk=T // tb),
        grid_spec=grid_spec,
        out_shape=jax.ShapeDtypeStruct((T, D), F32),
        compiler_params=_cparams(("arbitrary",)),
        name="moe_combine_norm",
    )(cnt, x1, qt, o, g)


def _expert_choice_moe(x1, h2, aff_t, p):
    T, D = x1.shape
    E = aff_t.shape[0]
    cap = CAPACITY_FACTOR * T // E
    q, qt, offs = _route(aff_t, cap)
    nb = T // LANES
    per_blk = min(MOE_TOKEN_BLOCK, T) // LANES
    cnt = offs[:, 0].reshape(E, nb)[:, ::per_blk]
    cnt = jnp.concatenate([cnt, jnp.full((E, 1), cap, F32)], axis=1).astype(I32).reshape(-1)
    blocked = (E, nb // per_blk, per_blk * LANES)
    idx, gate = _compact(cnt, q.reshape(blocked), aff_t.reshape(blocked), cap)
    o = _ffn(idx.reshape(E * cap), h2, gate, p["w_expert_gate"], p["w_expert_up"], p["w_expert_down"], cap, MOE_WINDOW)
    return _combine(cnt, x1, qt, o, p["norm_final_g"], E, cap + MOE_WINDOW)


def _encoder_group(x, p):
    B, S, D = x.shape
    T = B * S
    hd = p["q_norm_g"].shape[-1]
    wq, wk = N_Q_HEADS * hd, N_KV_HEADS * hd
    wu = N_FOURIER_GROUPS * FOURIER_GROUP_W
    xt = x.reshape(T, D)

    proj = _inproj(xt, p["norm_mix_g"], p["w_in"])
    u_off = wq + 2 * wk
    qr, kt, va = _qkrope(proj, wq, wk, p["q_norm_g"], p["k_norm_g"], p["rope"], S, ts=min(ATTN_KV_CHUNK, S))
    attn = _attention_any(qr, kt, va, p["q_norm_g"], p["k_norm_g"], B, S)
    f = _fourier(proj, u_off // wu, wu, B, S)
    merged = _merge(attn, f, proj, u_off + wu, p["w_attn_proj"], p["w_fourier_proj"])
    x1, h2, aff_t = _outproj(merged, xt, p["w_out"], p["norm_moe_g"], p["w_router_t"])
    return _expert_choice_moe(x1, h2, aff_t, p).reshape(B, S, D)


def kernel(x_prompt, x_sample, norm_mix_g, w_in, q_norm_g, k_norm_g, w_attn_proj, w_fourier_proj, w_out, norm_moe_g, w_router, w_expert_gate, w_expert_up, w_expert_down, norm_final_g):
    p = {
        "norm_mix_g": norm_mix_g[0][None, :],
        "w_in": w_in[0],
        "q_norm_g": q_norm_g[0][None, :],
        "k_norm_g": k_norm_g[0][None, :],
        "w_attn_proj": w_attn_proj[0].astype(BF16),
        "w_fourier_proj": w_fourier_proj[0].astype(BF16),
        "w_out": w_out[0].astype(BF16),
        "norm_moe_g": norm_moe_g[0][None, :],
        "w_router_t": w_router[0].T.astype(BF16),
        "w_expert_gate": w_expert_gate[0],
        "w_expert_up": w_expert_up[0],
        "w_expert_down": w_expert_down[0],
        "norm_final_g": norm_final_g[None, :],
        "rope": _rope_tables(max(x_prompt.shape[1], x_sample.shape[1]), q_norm_g.shape[-1]),
    }
    return (_encoder_group(x_prompt, p), _encoder_group(x_sample, p))
```

```python
import functools
import math

import jax
import jax.numpy as jnp
from jax import lax
from jax.experimental import pallas as pl
from jax.experimental.pallas import tpu as pltpu

F32 = jnp.float32
BF16 = jnp.bfloat16
I32 = jnp.int32

EPS = 1e-6
N_Q_HEADS = 16
N_KV_HEADS = 4
ROPE_GRID_W = 64
ROPE_THETA = 10000.0
N_FOURIER_GROUPS = 4
FOURIER_GROUP_W = 256
CAPACITY_FACTOR = 2
LANES = 128
BF16_ROWS = 16
ATTN_KV_CHUNK = 512
VMEM_LIMIT = 56 * 1024 * 1024


def _cparams(sem):
    return pltpu.CompilerParams(dimension_semantics=sem, vmem_limit_bytes=VMEM_LIMIT)


def _dot(a, b):
    return jnp.dot(a, b, preferred_element_type=F32)


def _dot_nt(a, b):
    return lax.dot_general(a, b, (((1,), (1,)), ((), ())), preferred_element_type=F32)


def _rms(x, g):
    ms = jnp.mean(x * x, axis=-1, keepdims=True)
    return x * lax.rsqrt(ms + EPS) * g


def _inproj_kernel(x_ref, g_ref, w_ref, o_ref, h_ref):
    @pl.when(pl.program_id(1) == 0)
    def _():
        h_ref[...] = _rms(x_ref[...], g_ref[...]).astype(BF16)

    o_ref[...] = _dot(h_ref[...], w_ref[...].astype(BF16)).astype(o_ref.dtype)


def _inproj(x, g, w, tm=1024, tn=1024):
    T, D = x.shape
    N = w.shape[1]
    tm = min(tm, T)
    return pl.pallas_call(
        _inproj_kernel,
        grid=(T // tm, N // tn),
        in_specs=[
            pl.BlockSpec((tm, D), lambda i, j: (i, 0)),
            pl.BlockSpec((1, D), lambda i, j: (0, 0)),
            pl.BlockSpec((D, tn), lambda i, j: (0, j)),
        ],
        out_specs=pl.BlockSpec((tm, tn), lambda i, j: (i, j)),
        out_shape=jax.ShapeDtypeStruct((T, N), BF16),
        scratch_shapes=[pltpu.VMEM((tm, D), BF16)],
        compiler_params=_cparams(("parallel", "arbitrary")),
        name="inproj",
    )(x, g, w)


def _qkrope_kernel(q_ref, k_ref, v_ref, qg_ref, kg_ref, c_ref, s_ref, ct_ref, st_ref, qo_ref, ko_ref, vo_ref, *, hd, scale):
    c = c_ref[...]
    s = s_ref[...]
    lane = lax.broadcasted_iota(I32, (1, hd), 1)
    low_half = (lane % (hd // 2)) < (hd // 4)

    def one(x, g):
        y = _rms(x, g)
        sw = jnp.where(low_half, pltpu.roll(y, hd - hd // 4, 1), pltpu.roll(y, hd // 4, 1))
        return y * c + sw * s

    ct = ct_ref[...]
    st = st_ref[...]
    qg = qg_ref[...]
    quarter = hd // 4
    for h in range(q_ref.shape[1] // hd):
        xt = q_ref[:, h * hd:(h + 1) * hd].astype(F32).T
        ms = jnp.mean(xt * xt, axis=0, keepdims=True)
        y = xt * lax.rsqrt(ms + EPS) * qg
        sw = jnp.concatenate([y[quarter:2 * quarter], y[:quarter], y[3 * quarter:], y[2 * quarter:3 * quarter]], axis=0)
        qo_ref[h] = ((y * ct + sw * st) * scale).astype(qo_ref.dtype)
    pad = vo_ref.shape[2] - hd
    ones_row = jnp.where(lax.broadcasted_iota(I32, (pad, v_ref.shape[0]), 0) == 0, 1.0, 0.0)
    for h in range(k_ref.shape[1] // hd):
        sl = slice(h * hd, (h + 1) * hd)
        ko_ref[:, sl] = one(k_ref[:, sl].astype(F32), kg_ref[...]).astype(ko_ref.dtype)
        vt = jnp.concatenate([v_ref[:, sl].astype(F32).T, ones_row], axis=0)
        vo_ref[h, 0] = vt.astype(vo_ref.dtype)


def _qkrope(proj, wq, wk, qg, kg, rope, S, ts):
    T = proj.shape[0]
    hd = qg.shape[-1]
    n_q, n_kv = wq // hd, wk // hd
    ns = S // ts
    k_blk = wq // wk
    return pl.pallas_call(
        functools.partial(_qkrope_kernel, hd=hd, scale=hd ** -0.5 * math.log2(math.e)),
        grid=(T // ts,),
        in_specs=[
            pl.BlockSpec((ts, wq), lambda i: (i, 0)),
            pl.BlockSpec((ts, wk), lambda i: (i, k_blk)),
            pl.BlockSpec((ts, wk), lambda i: (i, k_blk + 1)),
            pl.BlockSpec((hd, 1), lambda i: (0, 0)),
            pl.BlockSpec((1, hd), lambda i: (0, 0)),
            pl.BlockSpec((ts, hd), lambda i: (i % ns, 0)),
            pl.BlockSpec((ts, hd), lambda i: (i % ns, 0)),
            pl.BlockSpec((hd, ts), lambda i: (0, i % ns)),
            pl.BlockSpec((hd, ts), lambda i: (0, i % ns)),
        ],
        out_specs=[
            pl.BlockSpec((n_q, hd, ts), lambda i: (0, 0, i)),
            pl.BlockSpec((ts, wk), lambda i: (i, 0)),
            pl.BlockSpec((n_kv, 1, hd + BF16_ROWS, ts), lambda i: (0, i, 0, 0)),
        ],
        out_shape=[
            jax.ShapeDtypeStruct((n_q, hd, T), BF16),
            jax.ShapeDtypeStruct((T, wk), BF16),
            jax.ShapeDtypeStruct((n_kv, T // ts, hd + BF16_ROWS, ts), BF16),
        ],
        compiler_params=_cparams(("parallel",)),
        name="qkrope",
    )(proj, proj, proj, qg.reshape(hd, 1), kg, *rope)


def _rope_tables(S, hd):
    axis = hd // 2
    rows = S // ROPE_GRID_W
    inv_freq = ROPE_THETA ** (-jnp.arange(0, axis, 2, dtype=F32) / axis)
    ang_r = jnp.arange(rows, dtype=F32)[:, None] * inv_freq
    ang_c = jnp.arange(ROPE_GRID_W, dtype=F32)[:, None] * inv_freq
    cr, sr = (jnp.repeat(t, ROPE_GRID_W, axis=0) for t in (jnp.cos(ang_r), jnp.sin(ang_r)))
    cc, sc = (jnp.tile(t, (rows, 1)) for t in (jnp.cos(ang_c), jnp.sin(ang_c)))
    cos_t = jnp.concatenate([cr, cr, cc, cc], axis=-1)
    sin_t = jnp.concatenate([-sr, sr, -sc, sc], axis=-1)
    return cos_t, sin_t, cos_t.T, sin_t.T


def _attn_finish(acc_ref, o_ref, hd):
    for h in range(acc_ref.shape[0]):
        a = acc_ref[h]
        o_ref[:, h * hd:(h + 1) * hd] = (a[:hd] / a[hd:hd + 1]).T.astype(o_ref.dtype)


def _attn_bounded_kernel(qt_ref, k_ref, vt_ref, o_ref, acc_ref, p_ref):
    G, hd, tq = qt_ref.shape
    n_chunks, _, tk = vt_ref.shape
    acc_ref[...] = jnp.zeros_like(acc_ref)

    def keys(c):
        return k_ref[pl.ds(pl.multiple_of(c * tk, tk), tk), :]

    k0 = keys(0)
    for h in range(G):
        p_ref[h] = jnp.exp2(_dot(k0, qt_ref[h])).astype(BF16)

    def body(c, carry):
        kc = keys(c)
        vt = vt_ref[c - 1]
        for h in range(G):
            s = _dot(kc, qt_ref[h])
            acc_ref[h] += _dot(vt, p_ref[h])
            p_ref[h] = jnp.exp2(s).astype(BF16)
        return carry

    unroll = max(u for u in range(1, 9) if (n_chunks - 1) % u == 0) if n_chunks > 1 else 1
    lax.fori_loop(1, n_chunks, body, 0, unroll=unroll)
    vt = vt_ref[n_chunks - 1]
    for h in range(G):
        acc_ref[h] += _dot(vt, p_ref[h])
    _attn_finish(acc_ref, o_ref, hd)


def _attn_online_kernel(qt_ref, k_ref, vt_ref, o_ref, acc_ref, m_ref):
    G, hd, tq = qt_ref.shape
    n_chunks, _, tk = vt_ref.shape
    acc_ref[...] = jnp.zeros_like(acc_ref)
    m_ref[...] = jnp.full_like(m_ref, -jnp.inf)

    def body(c, carry):
        kc = k_ref[pl.ds(pl.multiple_of(c * tk, tk), tk), :]
        vt = vt_ref[c]
        for h in range(G):
            s = _dot(kc, qt_ref[h])
            m = m_ref[h]
            m_new = jnp.maximum(m, jnp.max(s, axis=0, keepdims=True))
            p = jnp.exp2(s - m_new).astype(BF16)
            acc_ref[h] = jnp.exp2(m - m_new) * acc_ref[h] + _dot(vt, p)
            m_ref[h] = m_new
        return carry

    lax.fori_loop(0, n_chunks, body, 0)
    _attn_finish(acc_ref, o_ref, hd)


def _attention(q, k, vt, B, S, bounded, tq=1024):
    n_q, hd, T = q.shape
    n_kv, _, vr, tk = vt.shape
    G = n_q // n_kv
    tq = min(tq, S)
    nq = S // tq
    nc = S // tk
    scratch = [pltpu.VMEM((G, vr, tq), F32)]
    scratch.append(pltpu.VMEM((G, tk, tq), BF16) if bounded else pltpu.VMEM((G, 1, tq), F32))
    return pl.pallas_call(
        _attn_bounded_kernel if bounded else _attn_online_kernel,
        grid=(B, n_kv, nq),
        in_specs=[
            pl.BlockSpec((G, hd, tq), lambda b, g, i: (g, 0, b * nq + i)),
            pl.BlockSpec((S, hd), lambda b, g, i: (b, g)),
            pl.BlockSpec((None, nc, vr, tk), lambda b, g, i: (g, b, 0, 0)),
        ],
        out_specs=pl.BlockSpec((tq, G * hd), lambda b, g, i: (b * nq + i, g)),
        out_shape=jax.ShapeDtypeStruct((T, n_q * hd), BF16),
        scratch_shapes=scratch,
        compiler_params=_cparams(("parallel", "parallel", "parallel")),
        name="attention_bounded" if bounded else "attention_online",
    )(q, k, vt)


MAX_UNSHIFTED_LOG2_SCORE = 100.0


def _attention_any(q, kt, v, qg, kg, B, S):
    hd = qg.shape[-1]
    bound = 1.02 * math.sqrt(hd) * math.log2(math.e) * jnp.max(jnp.abs(qg)) * jnp.max(jnp.abs(kg))
    return lax.cond(
        bound <= MAX_UNSHIFTED_LOG2_SCORE,
        lambda: _attention(q, kt, v, B, S, True),
        lambda: _attention(q, kt, v, B, S, False),
    )


def _dft_split(S):
    lg = int(math.log2(S))
    assert 1 << lg == S
    n2 = 1 << ((lg + 1) // 2)
    return S // n2, n2


def _cos_sin(n, m, period):
    ang = (2.0 * math.pi / period) * ((jnp.arange(n, dtype=I32)[:, None] * jnp.arange(m, dtype=I32)[None, :]) % period).astype(F32)
    return jnp.cos(ang), jnp.sin(ang)


def _f0_kernel(u_ref, cs_ref, a_ref, b_ref, *, gw):
    cs = cs_ref[...]
    for g in range(u_ref.shape[1] // gw):
        sl = slice(g * gw, (g + 1) * gw)
        ab = _dot(u_ref[:, sl], cs)
        a_ref[:, sl] = ab[:, :gw]
        b_ref[:, sl] = ab[:, gw:]


def _fa_kernel(a_hbm, b_hbm, ca_ref, sa_ref, ct_ref, st_ref, tr_hbm, ti_hbm, abuf, bbuf, trbuf, tibuf, sems):
    r, n2, W = abuf.shape
    bb = pl.program_id(0)
    j0 = pl.program_id(1) * r
    rows = pl.ds(bb * n2, n2)

    def loads(jj):
        return (
            pltpu.make_async_copy(a_hbm.at[rows, j0 + jj, :], abuf.at[jj], sems.at[0, jj]),
            pltpu.make_async_copy(b_hbm.at[rows, j0 + jj, :], bbuf.at[jj], sems.at[1, jj]),
        )

    def stores(jj):
        return (
            pltpu.make_async_copy(trbuf.at[jj], tr_hbm.at[rows, j0 + jj, :], sems.at[2, jj]),
            pltpu.make_async_copy(tibuf.at[jj], ti_hbm.at[rows, j0 + jj, :], sems.at[3, jj]),
        )

    for jj in range(r):
        for cp in loads(jj):
            cp.start()
    ca = ca_ref[...]
    sa = sa_ref[...]
    for jj in range(r):
        for cp in loads(jj):
            cp.wait()
        a = abuf[jj].astype(BF16)
        b = bbuf[jj].astype(BF16)
        tr = _dot(ca, a) - _dot(sa, b)
        ti = -(_dot(sa, a) + _dot(ca, b))
        ct = ct_ref[jj]
        st = st_ref[jj]
        trbuf[jj] = tr * ct + ti * st
        tibuf[jj] = ti * ct - tr * st
        for cp in stores(jj):
            cp.start()
    for jj in range(r):
        for cp in stores(jj):
            cp.wait()


def _fb_kernel(tr_ref, ti_ref, c1_ref, s1_ref, f_ref, *, scale):
    c1 = c1_ref[...]
    s1 = s1_ref[...]
    for kk in range(tr_ref.shape[0]):
        xr = tr_ref[kk].astype(BF16)
        xi = ti_ref[kk].astype(BF16)
        f_ref[:, kk, :] = (_dot(c1, xr) + _dot(s1, xi)) * scale


def _fourier(proj, u_blk, W, B, S, ts=512, r=8):
    u = proj
    T = proj.shape[0]
    gw = FOURIER_GROUP_W
    ts = min(ts, T)
    n1, n2 = _dft_split(S)

    cc, sc = _cos_sin(gw, gw, gw)
    cs = jnp.concatenate([cc, sc], axis=1).astype(BF16)
    a, b = pl.pallas_call(
        functools.partial(_f0_kernel, gw=gw),
        grid=(T // ts,),
        in_specs=[pl.BlockSpec((ts, W), lambda i: (i, u_blk)), pl.BlockSpec((gw, 2 * gw), lambda i: (0, 0))],
        out_specs=[pl.BlockSpec((ts, W), lambda i: (i, 0))] * 2,
        out_shape=[jax.ShapeDtypeStruct((T, W), F32)] * 2,
        compiler_params=_cparams(("parallel",)),
        name="fourier_channels",
    )(u, cs)

    a3 = a.reshape(B * n2, n1, W)
    b3 = b.reshape(B * n2, n1, W)
    ca, sa = _cos_sin(n2, n2, n2)
    ctw, stw = _cos_sin(n1, n2, S)
    ctw = ctw[:, :, None]
    stw = stw[:, :, None]
    tr, ti = pl.pallas_call(
        _fa_kernel,
        grid=(B, n1 // r),
        in_specs=[
            pl.BlockSpec(memory_space=pl.ANY),
            pl.BlockSpec(memory_space=pl.ANY),
            pl.BlockSpec((n2, n2), lambda bb, j: (0, 0)),
            pl.BlockSpec((n2, n2), lambda bb, j: (0, 0)),
            pl.BlockSpec((r, n2, 1), lambda bb, j: (j, 0, 0)),
            pl.BlockSpec((r, n2, 1), lambda bb, j: (j, 0, 0)),
        ],
        out_specs=[pl.BlockSpec(memory_space=pl.ANY)] * 2,
        out_shape=[jax.ShapeDtypeStruct((B * n2, n1, W), F32)] * 2,
        scratch_shapes=[pltpu.VMEM((r, n2, W), F32)] * 4 + [pltpu.SemaphoreType.DMA((4, r))],
        compiler_params=_cparams(("parallel", "parallel")),
        name="fourier_stage_a",
    )(a3, b3, ca.astype(BF16), sa.astype(BF16), ctw, stw)

    c1, s1 = _cos_sin(n1, n1, n1)
    f3 = pl.pallas_call(
        functools.partial(_fb_kernel, scale=1.0 / math.sqrt(S * gw)),
        grid=(B, n2 // r),
        in_specs=[
            pl.BlockSpec((r, n1, W), lambda bb, k: (bb * (n2 // r) + k, 0, 0)),
            pl.BlockSpec((r, n1, W), lambda bb, k: (bb * (n2 // r) + k, 0, 0)),
            pl.BlockSpec((n1, n1), lambda bb, k: (0, 0)),
            pl.BlockSpec((n1, n1), lambda bb, k: (0, 0)),
        ],
        out_specs=pl.BlockSpec((n1, r, W), lambda bb, k: (bb, k, 0)),
        out_shape=jax.ShapeDtypeStruct((B * n1, n2, W), F32),
        compiler_params=_cparams(("parallel", "parallel")),
        name="fourier_stage_b",
    )(tr, ti, c1.astype(BF16), s1.astype(BF16))
    return f3.reshape(T, W)


def _merge_kernel(at_ref, f_ref, ga_ref, gf_ref, wap_ref, wfp_ref, o_ref):
    a = _dot(at_ref[...], wap_ref[...])
    fo = _dot(f_ref[...].astype(BF16), wfp_ref[...])
    ga = jax.nn.sigmoid(ga_ref[...].astype(F32))
    gf = jax.nn.sigmoid(gf_ref[...].astype(F32))
    o_ref[...] = (ga * a + gf * fo).astype(o_ref.dtype)


def _merge(attn, f, gates, g_off, w_ap, w_fp, tm=1024, tn=1024):
    T, wa = attn.shape
    wf = f.shape[1]
    D = w_ap.shape[1]
    tm = min(tm, T)
    nj = D // tn
    g0 = g_off // tn
    return pl.pallas_call(
        _merge_kernel,
        grid=(T // tm, nj),
        in_specs=[
            pl.BlockSpec((tm, wa), lambda i, j: (i, 0)),
            pl.BlockSpec((tm, wf), lambda i, j: (i, 0)),
            pl.BlockSpec((tm, tn), lambda i, j: (i, g0 + j)),
            pl.BlockSpec((tm, tn), lambda i, j: (i, g0 + nj + j)),
            pl.BlockSpec((wa, tn), lambda i, j: (0, j)),
            pl.BlockSpec((wf, tn), lambda i, j: (0, j)),
        ],
        out_specs=pl.BlockSpec((tm, tn), lambda i, j: (i, j)),
        out_shape=jax.ShapeDtypeStruct((T, D), BF16),
        compiler_params=_cparams(("parallel", "parallel")),
        name="merge",
    )(attn, f, gates, gates, w_ap, w_fp)


def _outproj_kernel(m_ref, x_ref, w_ref, g_ref, wrt_ref, x1_ref, h_ref, afft_ref):
    x1 = x_ref[...] + _dot(m_ref[...], w_ref[...])
    x1_ref[...] = x1
    h32 = _rms(x1, g_ref[...])
    h_ref[...] = h32
    lt = _dot_nt(wrt_ref[...], h32.astype(BF16))
    et = jnp.exp(lt - jnp.max(lt, axis=0, keepdims=True))
    afft_ref[...] = et / jnp.sum(et, axis=0, keepdims=True)


def _outproj(merged, x, w_out, g, w_router_t, tm=512):
    T, D = x.shape
    E = w_router_t.shape[0]
    tm = min(tm, T)
    return pl.pallas_call(
        _outproj_kernel,
        grid=(T // tm,),
        in_specs=[
            pl.BlockSpec((tm, D), lambda i: (i, 0)),
            pl.BlockSpec((tm, D), lambda i: (i, 0)),
            pl.BlockSpec((D, D), lambda i: (0, 0), pipeline_mode=pl.Buffered(1)),
            pl.BlockSpec((1, D), lambda i: (0, 0)),
            pl.BlockSpec((E, D), lambda i: (0, 0)),
        ],
        out_specs=[
            pl.BlockSpec((tm, D), lambda i: (i, 0)),
            pl.BlockSpec((tm, D), lambda i: (i, 0)),
            pl.BlockSpec((E, tm), lambda i: (0, i)),
        ],
        out_shape=[
            jax.ShapeDtypeStruct((T, D), F32),
            jax.ShapeDtypeStruct((T, D), F32),
            jax.ShapeDtypeStruct((E, T), F32),
        ],
        compiler_params=_cparams(("parallel",)),
        name="outproj_router",
    )(merged, x, w_out, g, w_router_t)


def _route_kernel(aff_ref, q_ref, qt_ref, offs_ref, *, cap):
    E, nb, L = aff_ref.shape
    rows = E * nb
    aff = aff_ref[...]

    def count(mask):
        c = jnp.sum(mask.astype(F32), axis=2, keepdims=True)
        return jnp.sum(c, axis=1, keepdims=True)

    def search(i, prefix):
        cand = prefix | (jnp.int32(1) << (30 - i))
        ge = aff >= lax.bitcast_convert_type(cand, F32)
        return jnp.where(count(ge) >= cap, cand, prefix)

    thr = lax.bitcast_convert_type(lax.fori_loop(0, 31, search, jnp.zeros((E, 1, 1), I32)), F32)

    ri = lax.broadcasted_iota(I32, (L, L), 0)
    ci = lax.broadcasted_iota(I32, (L, L), 1)
    tri = (ri <= ci).astype(BF16)
    ones = jnp.ones((L, L), BF16)
    rr = lax.broadcasted_iota(I32, (rows, rows), 0)
    rc = lax.broadcasted_iota(I32, (rows, rows), 1)
    sh = nb.bit_length() - 1
    same_expert = lax.shift_right_logical(rr, sh) == lax.shift_right_logical(rc, sh)
    earlier = (same_expert & (rc < rr)).astype(BF16)

    def cumsum(mask):
        x = mask.astype(F32).reshape(rows, L).astype(BF16)
        within = _dot(x, tri)
        tot = _dot(x, ones)
        offs = _dot(earlier, tot.astype(BF16))
        return (within + offs).reshape(E, nb, L), offs

    gt = aff > thr
    eq = aff == thr
    need = cap - count(gt)
    sel = gt | (eq & (cumsum(eq)[0] <= need))
    rank, offs = cumsum(sel)
    q_ref[...] = jnp.where(sel, rank - 1.0, -1.0)
    offs_ref[...] = offs

    filler = jnp.full((L - E, L), -1.0, F32)
    for b in range(nb):
        tile = jnp.concatenate([q_ref[:, b, :], filler], axis=0)
        qt_ref[b * L:(b + 1) * L, :] = tile.T


def _route(aff_t, cap):
    E, T = aff_t.shape
    nb = T // LANES
    aff3 = aff_t.reshape(E, nb, LANES)
    return pl.pallas_call(
        functools.partial(_route_kernel, cap=cap),
        out_shape=[
            jax.ShapeDtypeStruct((E, nb, LANES), F32),
            jax.ShapeDtypeStruct((T, LANES), F32),
            jax.ShapeDtypeStruct((E * nb, LANES), F32),
        ],
        compiler_params=pltpu.CompilerParams(vmem_limit_bytes=VMEM_LIMIT),
        name="route",
    )(aff3)


MOE_TOKEN_BLOCK = 256
MOE_WINDOW = 64


def _compact_kernel(cnt_ref, q_ref, aff_ref, idx_ref, gate_ref, *, cap):
    e = pl.program_id(0)
    nbk, kb = q_ref.shape
    L = LANES
    base = e * (nbk + 1)
    slot_col = lax.broadcasted_iota(I32, (L, 1), 0).astype(F32)
    lane_tok = lax.broadcasted_iota(I32, (1, kb), 1).astype(F32)
    b_lo = b_hi = jnp.int32(0)
    for c in range(cap // L):
        lo_slot, hi_slot = c * L, (c + 1) * L
        b_lo = lax.while_loop(lambda b: cnt_ref[base + b + 1] <= lo_slot, lambda b: b + 1, b_lo)
        b_hi = lax.while_loop(lambda b: (b < nbk) & (cnt_ref[base + jnp.minimum(b, nbk)] < hi_slot), lambda b: b + 1, b_hi)

        def body(b, acc, lo_slot=lo_slot):
            ai, ag = acc
            hit = q_ref[pl.ds(b, 1), :] == slot_col + float(lo_slot)
            tok = lane_tok + lax.convert_element_type(b * kb, F32)
            return ai + jnp.where(hit, tok, 0.0), ag + jnp.where(hit, aff_ref[pl.ds(b, 1), :], 0.0)

        zero = jnp.zeros((L, kb), F32)
        ai, ag = lax.fori_loop(b_lo, b_hi, body, (zero, zero))
        folded = functools.reduce(jnp.add, [ai[:, i * L:(i + 1) * L] for i in range(kb // L)])
        idx_ref[:, lo_slot:hi_slot] = jnp.sum(folded.T, axis=0, keepdims=True).astype(I32)
        gate_ref[lo_slot:hi_slot, :] = jnp.sum(ag, axis=1, keepdims=True)


def _compact(cnt, q, aff, cap):
    E, nbk, kb = q.shape
    grid_spec = pltpu.PrefetchScalarGridSpec(
        num_scalar_prefetch=1,
        grid=(E,),
        in_specs=[
            pl.BlockSpec((None, nbk, kb), lambda e, cnt: (e, 0, 0)),
            pl.BlockSpec((None, nbk, kb), lambda e, cnt: (e, 0, 0)),
        ],
        out_specs=[
            pl.BlockSpec((None, 1, cap), lambda e, cnt: (e, 0, 0)),
            pl.BlockSpec((cap, 1), lambda e, cnt: (e, 0)),
        ],
    )
    return pl.pallas_call(
        functools.partial(_compact_kernel, cap=cap),
        grid_spec=grid_spec,
        out_shape=[jax.ShapeDtypeStruct((E, 1, cap), I32), jax.ShapeDtypeStruct((E * cap, 1), F32)],
        compiler_params=_cparams(("parallel",)),
        name="moe_compact",
    )(cnt, q, aff)


def _ffn_kernel(idx_ref, h_hbm, gate_ref, wg_ref, wu_ref, wd_ref, o_ref, xbuf, x_ref, acc_ref, sem, *, cap, n_f):
    e = pl.program_id(0)
    f = pl.program_id(1)
    per_step = cap // n_f

    def fetch_row(expert, r):
        tok = idx_ref[expert * cap + r]
        pltpu.make_async_copy(h_hbm.at[pl.ds(tok, 1), :], xbuf.at[pl.ds(r, 1), :], sem).start()

    @pl.when(f == 0)
    def _():
        @pl.when(e == 0)
        def _():
            lax.fori_loop(0, cap, lambda r, c: (fetch_row(0, r), c)[1], 0, unroll=8)

        pltpu.make_async_copy(h_hbm.at[pl.ds(0, cap), :], xbuf, sem).wait()
        x_ref[...] = xbuf[...].astype(BF16)
        acc_ref[...] = jnp.zeros_like(acc_ref)

    x = x_ref[...]
    hid = jax.nn.silu(_dot(x, wg_ref[...].astype(BF16))) * _dot(x, wu_ref[...].astype(BF16))
    acc_ref[...] += _dot(hid.astype(BF16), wd_ref[...].astype(BF16))

    last_e = e == pl.num_programs(0) - 1
    nxt = jnp.where(last_e, 0, e + 1)
    for r in range(per_step):
        fetch_row(nxt, f * per_step + r)

    @pl.when(f == n_f - 1)
    def _():
        o_ref[:cap, :] = (acc_ref[...] * gate_ref[...]).astype(o_ref.dtype)
        o_ref[cap:, :] = jnp.zeros((o_ref.shape[0] - cap, o_ref.shape[1]), o_ref.dtype)

        @pl.when(last_e)
        def _():
            pltpu.make_async_copy(h_hbm.at[pl.ds(0, cap), :], xbuf, sem).wait()


def _ffn(idx, h, gate, wg, wu, wd, cap, pad, fn=256):
    E, D, Fw = wg.shape
    fn = min(fn, Fw)
    grid_spec = pltpu.PrefetchScalarGridSpec(
        num_scalar_prefetch=1,
        grid=(E, Fw // fn),
        in_specs=[
            pl.BlockSpec(memory_space=pl.ANY),
            pl.BlockSpec((cap, 1), lambda e, f, idx: (e, 0)),
            pl.BlockSpec((None, D, fn), lambda e, f, idx: (e, 0, f)),
            pl.BlockSpec((None, D, fn), lambda e, f, idx: (e, 0, f)),
            pl.BlockSpec((None, fn, D), lambda e, f, idx: (e, f, 0)),
        ],
        out_specs=pl.BlockSpec((cap + pad, D), lambda e, f, idx: (e, 0)),
        scratch_shapes=[
            pltpu.VMEM((cap, D), F32),
            pltpu.VMEM((cap, D), BF16),
            pltpu.VMEM((cap, D), F32),
            pltpu.SemaphoreType.DMA(()),
        ],
    )
    return pl.pallas_call(
        functools.partial(_ffn_kernel, cap=cap, n_f=Fw // fn),
        grid_spec=grid_spec,
        out_shape=jax.ShapeDtypeStruct((E * (cap + pad), D), BF16),
        compiler_params=_cparams(("arbitrary", "arbitrary")),
        name="moe_ffn",
    )(idx, h, gate, wg, wu, wd)


def _combine_kernel(cnt_ref, x_ref, qt_ref, o_hbm, g_ref, y_ref, obuf, acc_ref, sems, *, n_exp, stride, nbk):
    b = pl.program_id(0)
    E = n_exp
    W = obuf.shape[1] // E
    tile = 16

    def first_slots(blk):
        return [lax.shift_left(lax.shift_right_logical(cnt_ref[e * (nbk + 1) + blk], 4), 4) for e in range(E)]

    def copies(starts, buf):
        return [
            pltpu.make_async_copy(
                o_hbm.at[pl.ds(pl.multiple_of(e * stride + starts[e], tile), W), :],
                obuf.at[buf, pl.ds(e * W, W), :],
                sems.at[buf],
            )
            for e in range(E)
        ]

    lane = lax.broadcasted_iota(I32, (1, 2 * W), 1)
    first = lane < W
    j = jnp.where(first, lane, lane - W).astype(F32)

    def scatter(starts, buf):
        seg = []
        for e in range(0, E, 2):
            slot = jnp.where(first, qt_ref[:, e:e + 1], qt_ref[:, e + 1:e + 2])
            want = jnp.where(first, starts[e].astype(F32), starts[e + 1].astype(F32)) + j
            seg.append(jnp.where(slot == want, 1.0, 0.0).astype(BF16))
        acc_ref[...] += _dot(jnp.concatenate(seg, axis=1), obuf[buf])

    lo = first_slots(b)
    hi = [cnt_ref[e * (nbk + 1) + b + 1] for e in range(E)]
    span = functools.reduce(jnp.maximum, [h - l for h, l in zip(hi, lo)])
    n_pass = lax.shift_right_logical(span + (W - 1), W.bit_length() - 1)
    buf = lax.rem(b, 2)

    @pl.when(b == 0)
    def _():
        for cp in copies(lo, 0):
            cp.start()

    for cp in copies(lo, buf):
        cp.wait()

    @pl.when(b + 1 < pl.num_programs(0))
    def _():
        for cp in copies(first_slots(b + 1), 1 - buf):
            cp.start()

    acc_ref[...] = x_ref[...]
    scatter(lo, buf)

    def later_pass(p, carry):
        starts = [l + p * W for l in lo]
        for cp in copies(starts, buf):
            cp.start()
        for cp in copies(starts, buf):
            cp.wait()
        scatter(starts, buf)
        return carry

    lax.fori_loop(1, n_pass, later_pass, 0)
    y_ref[...] = _rms(acc_ref[...], g_ref[...])


def _combine(cnt, x1, qt, o, g, n_exp, stride):
    T, D = x1.shape
    tb = min(MOE_TOKEN_BLOCK, T)
    grid_spec = pltpu.PrefetchScalarGridSpec(
        num_scalar_prefetch=1,
        grid=(T // tb,),
        in_specs=[
            pl.BlockSpec((tb, D), lambda b, cnt: (b, 0)),
            pl.BlockSpec((tb, LANES), lambda b, cnt: (b, 0)),
            pl.BlockSpec(memory_space=pl.ANY),
            pl.BlockSpec((1, D), lambda b, cnt: (0, 0)),
        ],
        out_specs=pl.BlockSpec((tb, D), lambda b, cnt: (b, 0)),
        scratch_shapes=[
            pltpu.VMEM((2, n_exp * MOE_WINDOW, D), BF16),
            pltpu.VMEM((tb, D), F32),
            pltpu.SemaphoreType.DMA((2,)),
        ],
    )
    return pl.pallas_call(
        functools.partial(_combine_kernel, n_exp=n_exp, stride=stride, nbk=T // tb),
        grid_spec=grid_spec,
        out_shape=jax.ShapeDtypeStruct((T, D), F32),
        compiler_params=_cparams(("arbitrary",)),
        name="moe_combine_norm",
    )(cnt, x1, qt, o, g)


def _expert_choice_moe(x1, h2, aff_t, p):
    T, D = x1.shape
    E = aff_t.shape[0]
    cap = CAPACITY_FACTOR * T // E
    q, qt, offs = _route(aff_t, cap)
    nb = T // LANES
    per_blk = min(MOE_TOKEN_BLOCK, T) // LANES
    cnt = offs[:, 0].reshape(E, nb)[:, ::per_blk]
    cnt = jnp.concatenate([cnt, jnp.full((E, 1), cap, F32)], axis=1).astype(I32).reshape(-1)
    blocked = (E, nb // per_blk, per_blk * LANES)
    idx, gate = _compact(cnt, q.reshape(blocked), aff_t.reshape(blocked), cap)
    o = _ffn(idx.reshape(E * cap), h2, gate, p["w_expert_gate"], p["w_expert_up"], p["w_expert_down"], cap, MOE_WINDOW)
    return _combine(cnt, x1, qt, o, p["norm_final_g"], E, cap + MOE_WINDOW)


def _encoder_group(x, p):
    B, S, D = x.shape
    T = B * S
    hd = p["q_norm_g"].shape[-1]
    wq, wk = N_Q_HEADS * hd, N_KV_HEADS * hd
    wu = N_FOURIER_GROUPS * FOURIER_GROUP_W
    xt = x.reshape(T, D)

    proj = _inproj(xt, p["norm_mix_g"], p["w_in"])
    u_off = wq + 2 * wk
    qr, kt, va = _qkrope(proj, wq, wk, p["q_norm_g"], p["k_norm_g"], p["rope"], S, ts=min(ATTN_KV_CHUNK, S))
    attn = _attention_any(qr, kt, va, p["q_norm_g"], p["k_norm_g"], B, S)
    f = _fourier(proj, u_off // wu, wu, B, S)
    merged = _merge(attn, f, proj, u_off + wu, p["w_attn_proj"], p["w_fourier_proj"])
    x1, h2, aff_t = _outproj(merged, xt, p["w_out"], p["norm_moe_g"], p["w_router_t"])
    return _expert_choice_moe(x1, h2, aff_t, p).reshape(B, S, D)


def kernel(x_prompt, x_sample, norm_mix_g, w_in, q_norm_g, k_norm_g, w_attn_proj, w_fourier_proj, w_out, norm_moe_g, w_router, w_expert_gate, w_expert_up, w_expert_down, norm_final_g):
    p = {
        "norm_mix_g": norm_mix_g[0][None, :],
        "w_in": w_in[0],
        "q_norm_g": q_norm_g[0][None, :],
        "k_norm_g": k_norm_g[0][None, :],
        "w_attn_proj": w_attn_proj[0].astype(BF16),
        "w_fourier_proj": w_fourier_proj[0].astype(BF16),
        "w_out": w_out[0].astype(BF16),
        "norm_moe_g": norm_moe_g[0][None, :],
        "w_router_t": w_router[0].T.astype(BF16),
        "w_expert_gate": w_expert_gate[0],
        "w_expert_up": w_expert_up[0],
        "w_expert_down": w_expert_down[0],
        "norm_final_g": norm_final_g[None, :],
        "rope": _rope_tables(max(x_prompt.shape[1], x_sample.shape[1]), q_norm_g.shape[-1]),
    }
    return (_encoder_group(x_prompt, p), _encoder_group(x_sample, p))
```
